```python
import jax, jax.numpy as jnp
from jax import lax
import numpy as np

D_MODEL = 2048
BATCH = 2
SEQ = 4096
DEPTH = 2

HEAD_DIM = 128
DILATED_PATTERNS = ((128, 1), (512, 4), (2048, 16))
N_GROUPS_A = len(DILATED_PATTERNS)
N_HEADS_A = 6
HPG_A = N_HEADS_A // N_GROUPS_A
N_HEADS_B = 6
N_HEADS_C = 4
N_HEADS_MIX = N_HEADS_A + N_HEADS_B + N_HEADS_C
MIX_WIDTH = N_HEADS_MIX * HEAD_DIM
OUT_A = HPG_A * HEAD_DIM
OUT_B = N_HEADS_B * HEAD_DIM
OUT_C = N_HEADS_C * HEAD_DIM
N_BRANCHES = 3
MOBA_BLOCK = 256
MOBA_TOPK = 3
MOBA_Q_CHUNK = 32
Q_BLOCK = 128
N_MEM = 256
N_HEADS_MEM = 4
MEM_WIDTH = N_HEADS_MEM * HEAD_DIM
ROPE_THETA = 500000.0
ROT_DIM = HEAD_DIM // 4
D_FF = ((8 * D_MODEL // 3 + 127) // 128) * 128
CONV_WIDTH = 3
EPS = 1e-6

kernel_name = "hybrid_gated_dilated_moba_stickbreak_block"


def rmsnorm(x, gain):
    xf = x.astype(jnp.float32)
    y = xf * lax.rsqrt(jnp.mean(xf * xf, axis=-1, keepdims=True) + EPS)
    return (y * gain.astype(jnp.float32)).astype(x.dtype)


def rope_tables(positions):
    inv_freq = ROPE_THETA ** (-jnp.arange(0, ROT_DIM, 2, dtype=jnp.float32) / ROT_DIM)
    ang = positions.astype(jnp.float32)[..., None] * inv_freq
    return jnp.cos(ang)[:, :, None, :], jnp.sin(ang)[:, :, None, :]


def partial_rope(x, cos, sin):
    half = ROT_DIM // 2
    xf = x.astype(jnp.float32)
    x1, x2, rest = xf[..., :half], xf[..., half:ROT_DIM], xf[..., ROT_DIM:]
    return jnp.concatenate([x1 * cos - x2 * sin, x2 * cos + x1 * sin, rest], axis=-1).astype(x.dtype)


def dilated_attention(q, k, v):
    b, s, _, d = q.shape
    nblk = s // Q_BLOCK
    scale = d ** -0.5
    qb = q.reshape(b, nblk, Q_BLOCK, N_HEADS_A, d).transpose(1, 0, 2, 3, 4)
    kgs = [k[:, :, g * HPG_A:(g + 1) * HPG_A] for g in range(N_GROUPS_A)]
    vgs = [v[:, :, g * HPG_A:(g + 1) * HPG_A] for g in range(N_GROUPS_A)]

    def one_block(args):
        qblk, bi = args
        t = bi * Q_BLOCK + jnp.arange(Q_BLOCK)
        outs, lses = [], []
        for g, (window, dilation) in enumerate(DILATED_PATTERNS):
            offs = jnp.arange(window // dilation + 1) * dilation
            idx = t[:, None] - offs[None, :]
            valid = idx >= 0
            idx = jnp.maximum(idx, 0)
            kg = jnp.take(kgs[g], idx, axis=1)
            vg = jnp.take(vgs[g], idx, axis=1)
            qg = qblk[:, :, g * HPG_A:(g + 1) * HPG_A]
            sc = jnp.einsum('bqgd,bqngd->bgqn', qg, kg, preferred_element_type=jnp.float32) * scale
            sc = jnp.where(valid[None, None], sc, -jnp.inf)
            lse = jax.nn.logsumexp(sc, axis=-1)
            p = jnp.exp(sc - lse[..., None])
            outs.append(jnp.einsum('bgqn,bqngd->bqgd', p.astype(vg.dtype), vg,
                                   preferred_element_type=jnp.float32))
            lses.append(lse)
        mix = jax.nn.softmax(jnp.stack(lses, 0), axis=0)
        o = jnp.einsum('nbgq,nbqgd->bqgd', mix, jnp.stack(outs, 0))
        return o.astype(q.dtype)

    out = lax.map(one_block, (qb, jnp.arange(nblk)))
    return out.transpose(1, 0, 2, 3, 4).reshape(b, s, OUT_A)


def moba_attention(q, k, v):
    b, s, h, d = q.shape
    scale = d ** -0.5
    nblk = -(-s // MOBA_BLOCK)
    pad = nblk * MOBA_BLOCK - s
    kp = jnp.pad(k, ((0, 0), (0, pad), (0, 0), (0, 0)))
    vp = jnp.pad(v, ((0, 0), (0, pad), (0, 0), (0, 0)))
    kb = kp.reshape(b, nblk, MOBA_BLOCK, h, d).transpose(0, 3, 1, 2, 4)
    vb = vp.reshape(b, nblk, MOBA_BLOCK, h, d).transpose(0, 3, 1, 2, 4)
    n_gate = max(nblk, MOBA_TOPK)
    kmean = jnp.mean(kb.astype(jnp.float32), axis=3)
    kmean = jnp.pad(kmean, ((0, 0), (0, 0), (0, n_gate - nblk), (0, 0)))
    nch = s // MOBA_Q_CHUNK
    qc = q.reshape(b, nch, MOBA_Q_CHUNK, h, d).transpose(1, 0, 3, 2, 4)
    bidx = jnp.arange(b)[:, None, None, None]
    hidx = jnp.arange(h)[None, :, None, None]
    n_sel = MOBA_TOPK * MOBA_BLOCK

    def one_chunk(args):
        qblk, ci = args
        t = ci * MOBA_Q_CHUNK + jnp.arange(MOBA_Q_CHUNK)
        own = (ci * MOBA_Q_CHUNK) // MOBA_BLOCK
        gate = jnp.einsum('bhqd,bhnd->bhqn', qblk.astype(jnp.float32), kmean)
        gate = jnp.where(jnp.arange(n_gate) < own, gate, -jnp.inf)
        _, sel = lax.top_k(gate, MOBA_TOPK)
        sel_valid = sel < own
        sel = jnp.minimum(sel, nblk - 1)
        ks = kb[bidx, hidx, sel]
        vs = vb[bidx, hidx, sel]
        s_sel = jnp.einsum('bhqd,bhqkjd->bhqkj', qblk, ks, preferred_element_type=jnp.float32) * scale
        s_sel = jnp.where(sel_valid[..., None], s_sel, -jnp.inf).reshape(b, h, MOBA_Q_CHUNK, n_sel)
        k_own = lax.dynamic_index_in_dim(kb, own, axis=2, keepdims=False)
        v_own = lax.dynamic_index_in_dim(vb, own, axis=2, keepdims=False)
        s_own = jnp.einsum('bhqd,bhjd->bhqj', qblk, k_own, preferred_element_type=jnp.float32) * scale
        key_pos = own * MOBA_BLOCK + jnp.arange(MOBA_BLOCK)
        s_own = jnp.where(key_pos[None, :] <= t[:, None], s_own, -jnp.inf)
        p = jax.nn.softmax(jnp.concatenate([s_sel, s_own], axis=-1), axis=-1)
        p_sel = p[..., :n_sel].reshape(b, h, MOBA_Q_CHUNK, MOBA_TOPK, MOBA_BLOCK).astype(v.dtype)
        p_own = p[..., n_sel:].astype(v.dtype)
        o = (jnp.einsum('bhqkj,bhqkjd->bhqd', p_sel, vs, preferred_element_type=jnp.float32)
             + jnp.einsum('bhqj,bhjd->bhqd', p_own, v_own, preferred_element_type=jnp.float32))
        return o.astype(q.dtype)

    out = lax.map(one_chunk, (qc, jnp.arange(nch)))
    return out.transpose(1, 0, 3, 2, 4).reshape(b, s, OUT_B)


def stick_breaking_attention(q, k, v):
    b, s, h, d = q.shape
    scale = d ** -0.5
    nblk = s // Q_BLOCK
    qb = q.reshape(b, nblk, Q_BLOCK, h, d).transpose(1, 0, 3, 2, 4)
    kt = k.transpose(0, 2, 1, 3)
    vt = v.transpose(0, 2, 1, 3)
    key_pos = jnp.arange(s)

    def one_block(args):
        qblk, bi = args
        t = bi * Q_BLOCK + jnp.arange(Q_BLOCK)
        z = jnp.einsum('bhqd,bhsd->bhqs', qblk, kt, preferred_element_type=jnp.float32) * scale
        causal = key_pos[None, :] < t[:, None]
        log_1m = jnp.where(causal, jax.nn.log_sigmoid(-z), 0.0)
        after = lax.cumsum(log_1m, axis=3, reverse=True) - log_1m
        a = jnp.where(causal, jnp.exp(jax.nn.log_sigmoid(z) + after), 0.0)
        o = jnp.einsum('bhqs,bhsd->bhqd', a.astype(v.dtype), vt, preferred_element_type=jnp.float32)
        return o.astype(q.dtype)

    out = lax.map(one_block, (qb, jnp.arange(nblk)))
    return out.transpose(1, 0, 3, 2, 4).reshape(b, s, OUT_C)


def hybrid_mixer(h, cos, sin, w_qkv, qk_gain, w_br_a, w_br_b, w_br_c, w_gate, b_gate, w_o):
    b, s, _ = h.shape
    qkv = h @ w_qkv
    q, k, v = jnp.split(qkv, 3, axis=-1)
    q = q.reshape(b, s, N_HEADS_MIX, HEAD_DIM)
    k = k.reshape(b, s, N_HEADS_MIX, HEAD_DIM)
    v = v.reshape(b, s, N_HEADS_MIX, HEAD_DIM)
    a0, a1, c0 = N_HEADS_A, N_HEADS_A + N_HEADS_B, N_HEADS_A + N_HEADS_B
    qa = partial_rope(rmsnorm(q[:, :, :a0], qk_gain[0]), cos, sin)
    ka = partial_rope(rmsnorm(k[:, :, :a0], qk_gain[1]), cos, sin)
    o_a = dilated_attention(qa, ka, v[:, :, :a0])
    qb = partial_rope(rmsnorm(q[:, :, a0:a1], qk_gain[2]), cos, sin)
    kb = partial_rope(rmsnorm(k[:, :, a0:a1], qk_gain[3]), cos, sin)
    o_b = moba_attention(qb, kb, v[:, :, a0:a1])
    o_c = stick_breaking_attention(q[:, :, c0:], k[:, :, c0:], v[:, :, c0:])
    gates = jax.nn.sigmoid((h @ w_gate + b_gate).astype(jnp.float32)).astype(h.dtype)
    gates = gates.reshape(b, s, N_BRANCHES, D_MODEL)
    merged = (gates[:, :, 0] * (o_a @ w_br_a) + gates[:, :, 1] * (o_b @ w_br_b)
              + gates[:, :, 2] * (o_c @ w_br_c))
    return merged @ w_o


def memory_cross_attention(h, m, wm_q, wm_kv, wm_o, gains):
    b, s, _ = h.shape
    n = m.shape[1]
    q = rmsnorm((h @ wm_q).reshape(b, s, N_HEADS_MEM, HEAD_DIM), gains[0])
    k, v = jnp.split(m @ wm_kv, 2, axis=-1)
    k = rmsnorm(k.reshape(b, n, N_HEADS_MEM, HEAD_DIM), gains[1])
    v = v.reshape(b, n, N_HEADS_MEM, HEAD_DIM)
    sc = jnp.einsum('bqhd,bnhd->bhqn', q, k, preferred_element_type=jnp.float32) * HEAD_DIM ** -0.5
    p = jax.nn.softmax(sc, axis=-1).astype(v.dtype)
    o = jnp.einsum('bhqn,bnhd->bqhd', p, v).reshape(b, s, MEM_WIDTH)
    return o @ wm_o


def conv_ffn(h, w_up, conv_w, conv_b, w_down):
    u = h @ w_up
    c = u.shape[-1]
    rhs = conv_w[:, None, :].astype(u.dtype)
    y = lax.conv_general_dilated(u, rhs, window_strides=(1,), padding=[(CONV_WIDTH - 1, 0)],
                                 dimension_numbers=('NWC', 'WIO', 'NWC'), feature_group_count=c)
    y = y + conv_b
    g, val = jnp.split(y, 2, axis=-1)
    return (jax.nn.silu(g) * val) @ w_down


def setup_inputs(seed: int = 0) -> dict:
    key = jax.random.key(seed)
    ks = jax.random.split(key, 24)
    L = DEPTH

    def nrm(k, shape, scale):
        return jax.random.normal(k, shape, jnp.float32) * scale

    start = jax.random.randint(ks[2], (BATCH, 1), 0, 1024, dtype=jnp.int32)
    return {
        "x": nrm(ks[0], (BATCH, SEQ, D_MODEL), 1.0),
        "mem": nrm(ks[1], (BATCH, N_MEM, D_MODEL), 1.0),
        "positions": start + jnp.arange(SEQ, dtype=jnp.int32)[None, :],
        "ln_mix": 1.0 + nrm(ks[3], (L, D_MODEL), 0.02),
        "w_qkv": nrm(ks[4], (L, D_MODEL, 3 * MIX_WIDTH), D_MODEL ** -0.5),
        "qk_gain": 1.0 + nrm(ks[5], (L, 4, HEAD_DIM), 0.02),
        "w_br_a": nrm(ks[6], (L, OUT_A, D_MODEL), OUT_A ** -0.5),
        "w_br_b": nrm(ks[7], (L, OUT_B, D_MODEL), OUT_B ** -0.5),
        "w_br_c": nrm(ks[8], (L, OUT_C, D_MODEL), OUT_C ** -0.5),
        "w_gate": nrm(ks[9], (L, D_MODEL, N_BRANCHES * D_MODEL), D_MODEL ** -0.5),
        "b_gate": nrm(ks[10], (L, N_BRANCHES * D_MODEL), 0.01),
        "w_o": nrm(ks[11], (L, D_MODEL, D_MODEL), D_MODEL ** -0.5),
        "ln_mem_q": 1.0 + nrm(ks[12], (L, D_MODEL), 0.02),
        "ln_mem_kv": 1.0 + nrm(ks[13], (L, D_MODEL), 0.02),
        "wm_q": nrm(ks[14], (L, D_MODEL, MEM_WIDTH), D_MODEL ** -0.5),
        "wm_kv": nrm(ks[15], (L, D_MODEL, 2 * MEM_WIDTH), D_MODEL ** -0.5),
        "wm_o": nrm(ks[16], (L, MEM_WIDTH, D_MODEL), MEM_WIDTH ** -0.5),
        "mem_qk_gain": 1.0 + nrm(ks[17], (L, 2, HEAD_DIM), 0.02),
        "ln_ffn": 1.0 + nrm(ks[18], (L, D_MODEL), 0.02),
        "w_up": nrm(ks[19], (L, D_MODEL, 2 * D_FF), D_MODEL ** -0.5),
        "conv_w": nrm(ks[20], (L, CONV_WIDTH, 2 * D_FF), CONV_WIDTH ** -0.5),
        "conv_b": nrm(ks[21], (L, 2 * D_FF), 0.01),
        "w_down": nrm(ks[22], (L, D_FF, D_MODEL), D_FF ** -0.5),
    }


def reference(x, mem, positions, ln_mix, w_qkv, qk_gain, w_br_a, w_br_b, w_br_c, w_gate, b_gate, w_o,
              ln_mem_q, ln_mem_kv, wm_q, wm_kv, wm_o, mem_qk_gain, ln_ffn, w_up, conv_w, conv_b, w_down):
    cos, sin = rope_tables(positions)
    for l in range(DEPTH):
        x = x + hybrid_mixer(rmsnorm(x, ln_mix[l]), cos, sin, w_qkv[l], qk_gain[l], w_br_a[l], w_br_b[l],
                             w_br_c[l], w_gate[l], b_gate[l], w_o[l])
        x = x + memory_cross_attention(rmsnorm(x, ln_mem_q[l]), rmsnorm(mem, ln_mem_kv[l]), wm_q[l], wm_kv[l],
                                       wm_o[l], mem_qk_gain[l])
        x = x + conv_ffn(rmsnorm(x, ln_ffn[l]), w_up[l], conv_w[l], conv_b[l], w_down[l])
    return x
```

```python
import functools

import jax
import jax.numpy as jnp
from jax import lax
from jax.experimental import pallas as pl
from jax.experimental.pallas import tpu as pltpu

HEAD_DIM = 128
DILATIONS = (1, 4, 16)
WINDOW_STEPS = 128
HEADS_PER_GROUP = 2
N_HEADS_A = 6
N_HEADS_B = 6
N_HEADS_C = 4
N_HEADS_MIX = 16
MIX_WIDTH = N_HEADS_MIX * HEAD_DIM
MOBA_BLOCK = 256
MOBA_TOPK = 3
N_HEADS_MEM = 4
ROPE_THETA = 500000.0
ROT_DIM = HEAD_DIM // 4
ROT_HALF = ROT_DIM // 2
CONV_WIDTH = 3
EPS = 1e-6
SCALE = HEAD_DIM ** -0.5
FF_TILE = 512
HALO = 8
VMEM_LIMIT = 56 * 1024 * 1024

_NT = (((1,), (1,)), ((), ()))


def _params(*sem):
    return pltpu.CompilerParams(dimension_semantics=sem, vmem_limit_bytes=VMEM_LIMIT)


def _rms(y, gain):
    return y * lax.rsqrt(jnp.mean(y * y, axis=-1, keepdims=True) + EPS) * gain


def _sigmoid(y):
    return 1.0 / (1.0 + jnp.exp(-y))


def _rope_table_kernel(pos_ref, inv_ref, cos_ref, sin_lo_ref, sin_hi_ref):
    ang = pos_ref[...].astype(jnp.float32) * inv_ref[...]
    lane = lax.broadcasted_iota(jnp.int32, ang.shape, 1)
    s = jnp.sin(ang)
    cos_ref[...] = jnp.cos(ang)
    sin_lo_ref[...] = jnp.where(lane < ROT_HALF, -s, 0.0)
    sin_hi_ref[...] = jnp.where((lane >= ROT_HALF) & (lane < ROT_DIM), s, 0.0)


def rope_tables(positions):
    m = positions.size
    tm = 1024
    inv = ROPE_THETA ** (-jnp.arange(0, ROT_DIM, 2, dtype=jnp.float32) / ROT_DIM)
    inv_row = jnp.zeros((1, HEAD_DIM), jnp.float32).at[0, :ROT_DIM].set(jnp.concatenate([inv, inv]))
    tab = jax.ShapeDtypeStruct((m, HEAD_DIM), jnp.float32)
    spec = pl.BlockSpec((tm, HEAD_DIM), lambda i: (i, 0))
    return pl.pallas_call(
        _rope_table_kernel, grid=(m // tm,),
        in_specs=[pl.BlockSpec((tm, 1), lambda i: (i, 0)), pl.BlockSpec((1, HEAD_DIM), lambda i: (0, 0))],
        out_specs=[spec, spec, spec], out_shape=[tab, tab, tab],
        compiler_params=_params("parallel"), name="rope_tables",
    )(positions.reshape(m, 1), inv_row)


def _rmsnorm_kernel(x_ref, g_ref, o_ref):
    o_ref[...] = _rms(x_ref[...], g_ref[...]).astype(o_ref.dtype)


def rmsnorm_bf16(x, gain):
    m, d = x.shape
    tm = 512
    return pl.pallas_call(
        _rmsnorm_kernel, grid=(m // tm,),
        in_specs=[pl.BlockSpec((tm, d), lambda i: (i, 0)), pl.BlockSpec((1, d), lambda i: (0, 0))],
        out_specs=pl.BlockSpec((tm, d), lambda i: (i, 0)),
        out_shape=jax.ShapeDtypeStruct((m, d), jnp.bfloat16),
        compiler_params=_params("parallel"), name="rmsnorm",
    )(x, gain.reshape(1, d))


def _qkv_kernel(h_ref, w_ref, gain_ref, cos_ref, sin_lo_ref, sin_hi_ref, o_ref, *, heads_per_tile):
    j = pl.program_id(1)
    acc = jnp.dot(h_ref[...], w_ref[...], preferred_element_type=jnp.float32)
    tiles_per_part = MIX_WIDTH // (heads_per_tile * HEAD_DIM)
    normed = (j < 2 * tiles_per_part) & (j % tiles_per_part != tiles_per_part - 1)

    @pl.when(normed)
    def _():
        c, s_lo, s_hi = cos_ref[...], sin_lo_ref[...], sin_hi_ref[...]
        for hd in range(heads_per_tile):
            sl = slice(hd * HEAD_DIM, (hd + 1) * HEAD_DIM)
            y = _rms(acc[:, sl], gain_ref[:, sl])
            y = (y * c + pltpu.roll(y, HEAD_DIM - ROT_HALF, axis=1) * s_lo
                 + pltpu.roll(y, ROT_HALF, axis=1) * s_hi)
            o_ref[:, sl] = y.astype(o_ref.dtype)

    @pl.when(jnp.logical_not(normed))
    def _():
        o_ref[...] = acc.astype(o_ref.dtype)


def qkv_project(h, w_qkv, gain_cols, tables):
    m, d = h.shape
    n = w_qkv.shape[1]
    tm, tn = 1024, 512
    assert (N_HEADS_A + N_HEADS_B) * HEAD_DIM == MIX_WIDTH - tn
    tab_spec = pl.BlockSpec((tm, HEAD_DIM), lambda i, j: (i, 0))
    return pl.pallas_call(
        functools.partial(_qkv_kernel, heads_per_tile=tn // HEAD_DIM), grid=(m // tm, n // tn),
        in_specs=[pl.BlockSpec((tm, d), lambda i, j: (i, 0)), pl.BlockSpec((d, tn), lambda i, j: (0, j)),
                  pl.BlockSpec((1, tn), lambda i, j: (0, j)), tab_spec, tab_spec, tab_spec],
        out_specs=pl.BlockSpec((tm, tn), lambda i, j: (i, j)),
        out_shape=jax.ShapeDtypeStruct((m, n), jnp.bfloat16),
        compiler_params=_params("parallel", "arbitrary"), name="qkv_project",
    )(h, w_qkv, gain_cols, *tables)


def _gate_kernel(h_ref, w_ref, b_ref, o_ref):
    acc = jnp.dot(h_ref[...], w_ref[...], preferred_element_type=jnp.float32)
    o_ref[...] = _sigmoid(acc + b_ref[...]).astype(o_ref.dtype)


def gate_project(h, w_gate, b_gate):
    m, d = h.shape
    n = w_gate.shape[1]
    tm, tn = 1024, 512
    return pl.pallas_call(
        _gate_kernel, grid=(m // tm, n // tn),
        in_specs=[pl.BlockSpec((tm, d), lambda i, j: (i, 0)), pl.BlockSpec((d, tn), lambda i, j: (0, j)),
                  pl.BlockSpec((1, tn), lambda i, j: (0, j))],
        out_specs=pl.BlockSpec((tm, tn), lambda i, j: (i, j)),
        out_shape=jax.ShapeDtypeStruct((m, n), jnp.bfloat16),
        compiler_params=_params("parallel", "arbitrary"), name="gate_project",
    )(h, w_gate, b_gate.reshape(1, n))


def _window_kernel(q_ref, kp_ref, kc_ref, vp_ref, vc_ref, o_ref, lse_ref):
    it = pl.program_id(3)
    q = q_ref[0]
    k2 = jnp.concatenate([kp_ref[0], kc_ref[0]], axis=0)
    v2 = jnp.concatenate([vp_ref[0], vc_ref[0]], axis=0)
    s = lax.dot_general(q, k2, _NT, preferred_element_type=jnp.float32) * SCALE
    row = lax.broadcasted_iota(jnp.int32, s.shape, 0)
    col = lax.broadcasted_iota(jnp.int32, s.shape, 1)
    first_col = jnp.where(it > 0, row, WINDOW_STEPS)
    valid = (col >= first_col) & (col <= row + WINDOW_STEPS)
    s = jnp.where(valid, s, -jnp.inf)
    m = jnp.max(s, axis=1, keepdims=True)
    p = jnp.exp(s - m)
    l = jnp.sum(p, axis=1, keepdims=True)
    o = jnp.dot(p.astype(v2.dtype), v2, preferred_element_type=jnp.float32)
    o_ref[0] = o / l
    lse_ref[0] = jnp.broadcast_to(m + jnp.log(l), o.shape)


def window_attention(qkv, group):
    b, s, width = qkv.shape
    d = DILATIONS[group]
    cols = width // HEAD_DIM
    view = qkv.reshape(b, s // d, d * width)
    nt = s // d // WINDOW_STEPS
    head0 = group * HEADS_PER_GROUP

    def in_spec(part, prev):
        def index(bi, j, r, it):
            t = jnp.maximum(it - 1, 0) if prev else it
            return (bi, t, r * cols + part * N_HEADS_MIX + head0 + j)
        return pl.BlockSpec((1, WINDOW_STEPS, HEAD_DIM), index)

    out_spec = pl.BlockSpec((1, WINDOW_STEPS, HEAD_DIM), lambda bi, j, r, it: (bi, it, r * HEADS_PER_GROUP + j))
    out_sds = jax.ShapeDtypeStruct((b, s // d, d * HEADS_PER_GROUP * HEAD_DIM), jnp.float32)
    o, lse = pl.pallas_call(
        _window_kernel, grid=(b, HEADS_PER_GROUP, d, nt),
        in_specs=[in_spec(0, False), in_spec(1, True), in_spec(1, False), in_spec(2, True), in_spec(2, False)],
        out_specs=[out_spec, out_spec], out_shape=[out_sds, out_sds],
        compiler_params=_params("parallel", "parallel", "parallel", "arbitrary"), name=f"window_attention_d{d}",
    )(view, view, view, view, view)
    shape = (b * s, HEADS_PER_GROUP * HEAD_DIM)
    return o.reshape(shape), lse.reshape(shape)


def _moba_kernel(q_ref, k_ref, v_ref, o_ref, kmean_ref, m_ref, l_ref, acc_ref, *, n_blocks):
    own = pl.program_id(2)

    @pl.when(own == 0)
    def _():
        for n in range(n_blocks):
            kb = k_ref[0, n * MOBA_BLOCK:(n + 1) * MOBA_BLOCK, :].astype(jnp.float32)
            kmean_ref[n:n + 1, :] = jnp.sum(kb, axis=0, keepdims=True) / MOBA_BLOCK

    q = q_ref[0]
    gate = lax.dot_general(q.astype(jnp.float32), kmean_ref[...], _NT,
                           precision=lax.Precision.HIGHEST, preferred_element_type=jnp.float32)
    blk = lax.broadcasted_iota(jnp.int32, gate.shape, 1)
    gate = jnp.where(blk < own, gate, -jnp.inf)
    rank = jnp.zeros(gate.shape, jnp.int32)
    for mth in range(n_blocks):
        gm = gate[:, mth:mth + 1]
        beats = (gm > gate) | ((gm == gate) & (mth < blk))
        rank = rank + jnp.where(beats, 1, 0)
    chosen = jnp.where((rank < MOBA_TOPK) & (blk < own), 1.0, 0.0)

    row = lax.broadcasted_iota(jnp.int32, (MOBA_BLOCK, MOBA_BLOCK), 0)
    col = lax.broadcasted_iota(jnp.int32, (MOBA_BLOCK, MOBA_BLOCK), 1)
    start = pl.multiple_of(own * MOBA_BLOCK, MOBA_BLOCK)
    s = lax.dot_general(q, k_ref[0, pl.ds(start, MOBA_BLOCK), :], _NT,
                        preferred_element_type=jnp.float32) * SCALE
    s = jnp.where(col <= row, s, -jnp.inf)
    m0 = jnp.max(s, axis=1, keepdims=True)
    p = jnp.exp(s - m0)
    m_ref[...] = m0
    l_ref[...] = jnp.sum(p, axis=1, keepdims=True)
    acc_ref[...] = jnp.dot(p.astype(jnp.bfloat16), v_ref[0, pl.ds(start, MOBA_BLOCK), :],
                           preferred_element_type=jnp.float32)

    def past_block(n, carry):
        st = pl.multiple_of(n * MOBA_BLOCK, MOBA_BLOCK)
        sn = lax.dot_general(q, k_ref[0, pl.ds(st, MOBA_BLOCK), :], _NT,
                             preferred_element_type=jnp.float32) * SCALE
        picked = jnp.sum(jnp.where(blk == n, chosen, 0.0), axis=1, keepdims=True)
        sn = jnp.where(picked > 0.0, sn, -jnp.inf)
        m_old = m_ref[...]
        m_new = jnp.maximum(m_old, jnp.max(sn, axis=1, keepdims=True))
        alpha = jnp.exp(m_old - m_new)
        pn = jnp.exp(sn - m_new)
        m_ref[...] = m_new
        l_ref[...] = alpha * l_ref[...] + jnp.sum(pn, axis=1, keepdims=True)
        acc_ref[...] = alpha * acc_ref[...] + jnp.dot(
            pn.astype(jnp.bfloat16), v_ref[0, pl.ds(st, MOBA_BLOCK), :], preferred_element_type=jnp.float32)
        return carry

    lax.fori_loop(0, own, past_block, 0)
    o_ref[0] = (acc_ref[...] / l_ref[...]).astype(o_ref.dtype)


def moba_attention(qkv):
    b, s, _ = qkv.shape
    n_blocks = s // MOBA_BLOCK
    assert s % MOBA_BLOCK == 0 and n_blocks >= MOBA_TOPK

    def full(part):
        return pl.BlockSpec((1, s, HEAD_DIM), lambda bi, h, i: (bi, 0, part * N_HEADS_MIX + N_HEADS_A + h))

    return pl.pallas_call(
        functools.partial(_moba_kernel, n_blocks=n_blocks), grid=(b, N_HEADS_B, n_blocks),
        in_specs=[pl.BlockSpec((1, MOBA_BLOCK, HEAD_DIM), lambda bi, h, i: (bi, i, N_HEADS_A + h)),
                  full(1), full(2)],
        out_specs=pl.BlockSpec((1, MOBA_BLOCK, HEAD_DIM), lambda bi, h, i: (bi, i, h)),
        out_shape=jax.ShapeDtypeStruct((b, s, N_HEADS_B * HEAD_DIM), jnp.bfloat16),
        scratch_shapes=[pltpu.VMEM((n_blocks, HEAD_DIM), jnp.float32), pltpu.VMEM((MOBA_BLOCK, 1), jnp.float32),
                        pltpu.VMEM((MOBA_BLOCK, 1), jnp.float32), pltpu.VMEM((MOBA_BLOCK, HEAD_DIM), jnp.float32)],
        compiler_params=_params("parallel", "parallel", "arbitrary"), name="moba_attention",
    )(qkv, qkv, qkv).reshape(b * s, N_HEADS_B * HEAD_DIM)


STICK_TILE = 256


def _stick_kernel(q_ref, k_ref, v_ref, o_ref, run_ref, acc_ref):
    qi = pl.program_id(2)
    q = q_ref[0]
    t = STICK_TILE
    row = lax.broadcasted_iota(jnp.int32, (t, t), 0)
    col = lax.broadcasted_iota(jnp.int32, (t, t), 1)
    later = jnp.where(row > col, 1.0, 0.0).astype(jnp.bfloat16)

    def block(kb, diagonal):
        st = pl.multiple_of(kb * t, t)
        z = lax.dot_general(q, k_ref[0, pl.ds(st, t), :], _NT, preferred_element_type=jnp.float32) * SCALE
        log_1m = -(jnp.maximum(z, 0.0) + jnp.log1p(jnp.exp(-jnp.abs(z))))
        if diagonal:
            log_1m = jnp.where(col < row, log_1m, 0.0)
        hi = log_1m.astype(jnp.bfloat16)
        lo = (log_1m - hi.astype(jnp.float32)).astype(jnp.bfloat16)
        inside = (jnp.dot(hi, later, preferred_element_type=jnp.float32)
                  + jnp.dot(lo, later, preferred_element_type=jnp.float32))
        a = jnp.exp(z + log_1m + inside + run_ref[...])
        if diagonal:
            a = jnp.where(col < row, a, 0.0)
        acc_ref[...] += jnp.dot(a.astype(jnp.bfloat16), v_ref[0, pl.ds(st, t), :],
                                preferred_element_type=jnp.float32)
        run_ref[...] += jnp.sum(log_1m, axis=1, keepdims=True)

    run_ref[...] = jnp.zeros_like(run_ref)
    acc_ref[...] = jnp.zeros_like(acc_ref)
    block(qi, True)

    def earlier(i, carry):
        block(qi - 1 - i, False)
        return carry

    lax.fori_loop(0, qi, earlier, 0)
    o_ref[0] = acc_ref[...].astype(o_ref.dtype)


def stick_attention(qkv):
    b, s, _ = qkv.shape
    t = STICK_TILE
    head0 = N_HEADS_A + N_HEADS_B

    def full(part):
        return pl.BlockSpec((1, s, HEAD_DIM), lambda bi, h, i: (bi, 0, part * N_HEADS_MIX + head0 + h))

    return pl.pallas_call(
        _stick_kernel, grid=(b, N_HEADS_C, s // t),
        in_specs=[pl.BlockSpec((1, t, HEAD_DIM), lambda bi, h, i: (bi, i, head0 + h)), full(1), full(2)],
        out_specs=pl.BlockSpec((1, t, HEAD_DIM), lambda bi, h, i: (bi, i, h)),
        out_shape=jax.ShapeDtypeStruct((b, s, N_HEADS_C * HEAD_DIM), jnp.bfloat16),
        scratch_shapes=[pltpu.VMEM((t, 1), jnp.float32), pltpu.VMEM((t, HEAD_DIM), jnp.float32)],
        compiler_params=_params("parallel", "parallel", "arbitrary"), name="stick_attention",
    )(qkv, qkv, qkv).reshape(b * s, N_HEADS_C * HEAD_DIM)


def _merge_kernel(o0_ref, o1_ref, o2_ref, l0_ref, l1_ref, l2_ref, ob_ref, oc_ref, g_ref,
                  wa_ref, wb_ref, wc_ref, out_ref):
    l0, l1, l2 = l0_ref[...], l1_ref[...], l2_ref[...]
    top = jnp.maximum(jnp.maximum(l0, l1), l2)
    e0, e1, e2 = jnp.exp(l0 - top), jnp.exp(l1 - top), jnp.exp(l2 - top)
    o_a = (e0 * o0_ref[...] + e1 * o1_ref[...] + e2 * o2_ref[...]) / (e0 + e1 + e2)
    d = out_ref.shape[1]
    ya = jnp.dot(o_a.astype(jnp.bfloat16), wa_ref[...], preferred_element_type=jnp.float32)
    yb = jnp.dot(ob_ref[...], wb_ref[...], preferred_element_type=jnp.float32)
    yc = jnp.dot(oc_ref[...], wc_ref[...], preferred_element_type=jnp.float32)
    merged = (g_ref[:, 0:d].astype(jnp.float32) * ya + g_ref[:, d:2 * d].astype(jnp.float32) * yb
              + g_ref[:, 2 * d:3 * d].astype(jnp.float32) * yc)
    out_ref[...] = merged.astype(out_ref.dtype)


def merge_branches(outs_a, lses_a, o_b, o_c, gates, w_a, w_b, w_c):
    m = o_b.shape[0]
    d = w_a.shape[1]
    tm = 512

    def rows(width):
        return pl.BlockSpec((tm, width), lambda i: (i, 0))

    def whole(w):
        return pl.BlockSpec(w.shape, lambda i: (0, 0))

    wa = HEADS_PER_GROUP * HEAD_DIM
    return pl.pallas_call(
        _merge_kernel, grid=(m // tm,),
        in_specs=[rows(wa)] * 6 + [rows(o_b.shape[1]), rows(o_c.shape[1]), rows(gates.shape[1]),
                                   whole(w_a), whole(w_b), whole(w_c)],
        out_specs=rows(d), out_shape=jax.ShapeDtypeStruct((m, d), jnp.bfloat16),
        compiler_params=_params("parallel"), name="merge_branches",
    )(*outs_a, *lses_a, o_b, o_c, gates, w_a, w_b, w_c)


def _out_proj_kernel(a_ref, w_ref, x_ref, g_ref, xo_ref, ho_ref):
    xn = x_ref[...] + jnp.dot(a_ref[...], w_ref[...], preferred_element_type=jnp.float32)
    xo_ref[...] = xn
    ho_ref[...] = _rms(xn, g_ref[...]).astype(ho_ref.dtype)


def out_project(a, w, x, next_gain):
    m, k = a.shape
    d = w.shape[1]
    tm = 512
    return pl.pallas_call(
        _out_proj_kernel, grid=(m // tm,),
        in_specs=[pl.BlockSpec((tm, k), lambda i: (i, 0)), pl.BlockSpec((k, d), lambda i: (0, 0)),
                  pl.BlockSpec((tm, d), lambda i: (i, 0)), pl.BlockSpec((1, d), lambda i: (0, 0))],
        out_specs=[pl.BlockSpec((tm, d), lambda i: (i, 0)), pl.BlockSpec((tm, d), lambda i: (i, 0))],
        out_shape=[jax.ShapeDtypeStruct((m, d), jnp.float32), jax.ShapeDtypeStruct((m, d), jnp.bfloat16)],
        compiler_params=_params("parallel"), name="out_project",
    )(a, w, x, next_gain.reshape(1, d))


def _mem_kv_kernel(mem_ref, ln_ref, w_ref, gk_ref, kv_ref):
    hm = _rms(mem_ref[...], ln_ref[...]).astype(jnp.bfloat16)
    kv = jnp.dot(hm, w_ref[...], preferred_element_type=jnp.float32)
    half = kv.shape[1] // 2
    for hd in range(N_HEADS_MEM):
        sl = slice(hd * HEAD_DIM, (hd + 1) * HEAD_DIM)
        kv_ref[:, sl] = _rms(kv[:, sl], gk_ref[...]).astype(kv_ref.dtype)
    kv_ref[:, half:] = kv[:, half:].astype(kv_ref.dtype)


def mem_kv(mem2d, ln, wm_kv, gain_k):
    n, d = mem2d.shape
    w = wm_kv.shape[1]
    return pl.pallas_call(
        _mem_kv_kernel, grid=(1,),
        in_specs=[pl.BlockSpec((n, d), lambda i: (0, 0)), pl.BlockSpec((1, d), lambda i: (0, 0)),
                  pl.BlockSpec((d, w), lambda i: (0, 0)), pl.BlockSpec((1, HEAD_DIM), lambda i: (0, 0))],
        out_specs=pl.BlockSpec((n, w), lambda i: (0, 0)),
        out_shape=jax.ShapeDtypeStruct((n, w), jnp.bfloat16),
        compiler_params=_params("arbitrary"), name="mem_kv",
    )(mem2d, ln.reshape(1, d), wm_kv, gain_k.reshape(1, HEAD_DIM))


def _mem_attn_kernel(h_ref, wq_ref, gq_ref, kv_ref, wo_ref, x_ref, g_ref, xo_ref, ho_ref):
    qf = jnp.dot(h_ref[...], wq_ref[...], preferred_element_type=jnp.float32)
    half = kv_ref.shape[2] // 2
    outs = []
    for hd in range(N_HEADS_MEM):
        sl = slice(hd * HEAD_DIM, (hd + 1) * HEAD_DIM)
        qh = _rms(qf[:, sl], gq_ref[...]).astype(jnp.bfloat16)
        s = lax.dot_general(qh, kv_ref[0, :, sl], _NT, preferred_element_type=jnp.float32) * SCALE
        e = jnp.exp(s - jnp.max(s, axis=1, keepdims=True))
        vh = kv_ref[0, :, half + hd * HEAD_DIM:half + (hd + 1) * HEAD_DIM]
        o = jnp.dot(e.astype(jnp.bfloat16), vh, preferred_element_type=jnp.float32)
        outs.append((o / jnp.sum(e, axis=1, keepdims=True)).astype(jnp.bfloat16))
    o_all = jnp.concatenate(outs, axis=1)
    xn = x_ref[...] + jnp.dot(o_all, wo_ref[...], preferred_element_type=jnp.float32)
    xo_ref[...] = xn
    ho_ref[...] = _rms(xn, g_ref[...]).astype(ho_ref.dtype)


def mem_attention(h, wm_q, gain_q, kv, wm_o, x, next_gain, seq):
    m, d = h.shape
    wq = wm_q.shape[1]
    tm = 512
    per_batch = seq // tm
    return pl.pallas_call(
        _mem_attn_kernel, grid=(m // tm,),
        in_specs=[pl.BlockSpec((tm, d), lambda i: (i, 0)), pl.BlockSpec((d, wq), lambda i: (0, 0)),
                  pl.BlockSpec((1, HEAD_DIM), lambda i: (0, 0)),
                  pl.BlockSpec((1,) + kv.shape[1:], lambda i: (i // per_batch, 0, 0)),
                  pl.BlockSpec((wq, d), lambda i: (0, 0)), pl.BlockSpec((tm, d), lambda i: (i, 0)),
                  pl.BlockSpec((1, d), lambda i: (0, 0))],
        out_specs=[pl.BlockSpec((tm, d), lambda i: (i, 0)), pl.BlockSpec((tm, d), lambda i: (i, 0))],
        out_shape=[jax.ShapeDtypeStruct((m, d), jnp.float32), jax.ShapeDtypeStruct((m, d), jnp.bfloat16)],
        compiler_params=_params("parallel"), name="mem_attention",
    )(h, wm_q, gain_q.reshape(1, HEAD_DIM), kv, wm_o, x, next_gain.reshape(1, d))


def _ffn_kernel(h_ref, wg_ref, wv_ref, cwg_ref, cwv_ref, cbg_ref, cbv_ref, wd_ref, x_ref, o_ref,
                halo_ref, ug_ref, uv_ref, acc_ref, *, tiles_per_seq):
    i, f = pl.program_id(0), pl.program_id(1)
    tm = h_ref.shape[0]

    def conv(w_ref, cw_ref, cb_ref, u_ref, part):
        u = jnp.dot(h_ref[...], w_ref[...], preferred_element_type=jnp.float32)
        prev = halo_ref[f, part]
        u_ref[0:HALO, :] = jnp.where(i % tiles_per_seq == 0, jnp.zeros_like(prev), prev)
        u_ref[HALO:, :] = u
        halo_ref[f, part] = u[tm - HALO:, :]
        return (cw_ref[0:1, :] * u_ref[HALO - 2:HALO - 2 + tm, :] + cw_ref[1:2, :] * u_ref[HALO - 1:HALO - 1 + tm, :]
                + cw_ref[2:3, :] * u + cb_ref[...])

    yg = conv(wg_ref, cwg_ref, cbg_ref, ug_ref, 0)
    yv = conv(wv_ref, cwv_ref, cbv_ref, uv_ref, 1)
    act = (yg * _sigmoid(yg) * yv).astype(jnp.bfloat16)
    part = jnp.dot(act, wd_ref[...], preferred_element_type=jnp.float32)

    @pl.when(f == 0)
    def _():
        acc_ref[...] = part

    @pl.when(f > 0)
    def _():
        acc_ref[...] += part

    @pl.when(f == pl.num_programs(1) - 1)
    def _():
        o_ref[...] = x_ref[...] + acc_ref[...]


def conv_ffn(h, w_g, w_v, cw_g, cw_v, cb_g, cb_v, w_down, x, seq):
    m, d = h.shape
    fp = w_g.shape[1]
    tm, tf = 512, FF_TILE
    nf = fp // tf
    assert CONV_WIDTH - 1 <= HALO

    def cols(rows):
        return pl.BlockSpec((rows, tf), lambda i, f: (0, f))

    return pl.pallas_call(
        functools.partial(_ffn_kernel, tiles_per_seq=seq // tm), grid=(m // tm, nf),
        in_specs=[pl.BlockSpec((tm, d), lambda i, f: (i, 0)), cols(d), cols(d), cols(CONV_WIDTH), cols(CONV_WIDTH),
                  cols(1), cols(1), pl.BlockSpec((tf, d), lambda i, f: (f, 0)),
                  pl.BlockSpec((tm, d), lambda i, f: (i, 0))],
        out_specs=pl.BlockSpec((tm, d), lambda i, f: (i, 0)),
        out_shape=jax.ShapeDtypeStruct((m, d), jnp.float32),
        scratch_shapes=[pltpu.VMEM((nf, 2, HALO, tf), jnp.float32), pltpu.VMEM((tm + HALO, tf), jnp.float32),
                        pltpu.VMEM((tm + HALO, tf), jnp.float32), pltpu.VMEM((tm, d), jnp.float32)],
        compiler_params=_params("arbitrary", "arbitrary"), name="conv_ffn",
    )(h, w_g, w_v, cw_g, cw_v, cb_g, cb_v, w_down, x)


def _pad_cols(a, width):
    return jnp.pad(a, ((0, 0), (0, width - a.shape[1])))


def kernel(x, mem, positions, ln_mix, w_qkv, qk_gain, w_br_a, w_br_b, w_br_c, w_gate, b_gate, w_o,
           ln_mem_q, ln_mem_kv, wm_q, wm_kv, wm_o, mem_qk_gain, ln_ffn, w_up, conv_w, conv_b, w_down):
    b, s, d = x.shape
    depth = ln_mix.shape[0]
    bf = jnp.bfloat16
    d_ff = w_down.shape[1]
    fp = -(-d_ff // FF_TILE) * FF_TILE

    tables = rope_tables(positions)
    xf = x.reshape(b * s, d)
    mem2d = mem.reshape(b * mem.shape[1], d)
    h = rmsnorm_bf16(xf, ln_mix[0])
    for l in range(depth):
        zeros_c = jnp.zeros((N_HEADS_C * HEAD_DIM,), jnp.float32)
        gain_cols = jnp.concatenate(
            [jnp.tile(qk_gain[l, 0], N_HEADS_A), jnp.tile(qk_gain[l, 2], N_HEADS_B), zeros_c,
             jnp.tile(qk_gain[l, 1], N_HEADS_A), jnp.tile(qk_gain[l, 3], N_HEADS_B), zeros_c,
             jnp.zeros((MIX_WIDTH,), jnp.float32)]).reshape(1, 3 * MIX_WIDTH)
        qkv = qkv_project(h, w_qkv[l].astype(bf), gain_cols, tables).reshape(b, s, 3 * MIX_WIDTH)
        gates = gate_project(h, w_gate[l].astype(bf), b_gate[l])
        win = [window_attention(qkv, g) for g in range(len(DILATIONS))]
        o_b = moba_attention(qkv)
        o_c = stick_attention(qkv)
        merged = merge_branches([o for o, _ in win], [e for _, e in win], o_b, o_c, gates,
                                w_br_a[l].astype(bf), w_br_b[l].astype(bf), w_br_c[l].astype(bf))
        xf, h = out_project(merged, w_o[l].astype(bf), xf, ln_mem_q[l])

        kv = mem_kv(mem2d, ln_mem_kv[l], wm_kv[l].astype(bf), mem_qk_gain[l, 1])
        kv = kv.reshape(b, mem.shape[1], kv.shape[1])
        xf, h = mem_attention(h, wm_q[l].astype(bf), mem_qk_gain[l, 0], kv, wm_o[l].astype(bf), xf,
                              ln_ffn[l], s)

        w_g = _pad_cols(w_up[l, :, :d_ff], fp).astype(bf)
        w_v = _pad_cols(w_up[l, :, d_ff:], fp).astype(bf)
        cw_g, cw_v = _pad_cols(conv_w[l, :, :d_ff], fp), _pad_cols(conv_w[l, :, d_ff:], fp)
        cb_g = _pad_cols(conv_b[l, :d_ff].reshape(1, d_ff), fp)
        cb_v = _pad_cols(conv_b[l, d_ff:].reshape(1, d_ff), fp)
        w_d = jnp.pad(w_down[l], ((0, fp - d_ff), (0, 0))).astype(bf)
        xf = conv_ffn(h, w_g, w_v, cw_g, cw_v, cb_g, cb_v, w_d, xf, s)
        if l + 1 < depth:
            h = rmsnorm_bf16(xf, ln_mix[l + 1])
    return xf.reshape(b, s, d)
```

```python
import functools

import jax
import jax.numpy as jnp
from jax import lax
from jax.experimental import pallas as pl
from jax.experimental.pallas import tpu as pltpu

HEAD_DIM = 128
DILATIONS = (1, 4, 16)
WINDOW_STEPS = 128
HEADS_PER_GROUP = 2
N_HEADS_A = 6
N_HEADS_B = 6
N_HEADS_C = 4
N_HEADS_MIX = 16
MIX_WIDTH = N_HEADS_MIX * HEAD_DIM
KEY_TILE = 128
MOBA_BLOCK = 256
MOBA_TOPK = 3
TILES_PER_BLOCK = MOBA_BLOCK // KEY_TILE
N_HEADS_MEM = 4
ROPE_THETA = 500000.0
ROT_DIM = HEAD_DIM // 4
ROT_HALF = ROT_DIM // 2
CONV_WIDTH = 3
EPS = 1e-6
SCALE = HEAD_DIM ** -0.5
FF_TILE = 512
HALO = 8
VMEM_LIMIT = 56 * 1024 * 1024

_NT = (((1,), (1,)), ((), ()))


def _params(*sem):
    return pltpu.CompilerParams(dimension_semantics=sem, vmem_limit_bytes=VMEM_LIMIT)


def _rms(y, gain):
    return y * lax.rsqrt(jnp.mean(y * y, axis=-1, keepdims=True) + EPS) * gain


def _sigmoid(y):
    return 1.0 / (1.0 + jnp.exp(-y))


def _rope_table_kernel(pos_ref, inv_ref, cos_ref, sin_lo_ref, sin_hi_ref):
    ang = pos_ref[...].astype(jnp.float32) * inv_ref[...]
    lane = lax.broadcasted_iota(jnp.int32, ang.shape, 1)
    s = jnp.sin(ang)
    cos_ref[...] = jnp.cos(ang)
    sin_lo_ref[...] = jnp.where(lane < ROT_HALF, -s, 0.0)
    sin_hi_ref[...] = jnp.where((lane >= ROT_HALF) & (lane < ROT_DIM), s, 0.0)


def rope_tables(positions):
    m = positions.size
    tm = 1024
    inv = ROPE_THETA ** (-jnp.arange(0, ROT_DIM, 2, dtype=jnp.float32) / ROT_DIM)
    inv_row = jnp.zeros((1, HEAD_DIM), jnp.float32).at[0, :ROT_DIM].set(jnp.concatenate([inv, inv]))
    tab = jax.ShapeDtypeStruct((m, HEAD_DIM), jnp.float32)
    spec = pl.BlockSpec((tm, HEAD_DIM), lambda i: (i, 0))
    return pl.pallas_call(
        _rope_table_kernel, grid=(m // tm,),
        in_specs=[pl.BlockSpec((tm, 1), lambda i: (i, 0)), pl.BlockSpec((1, HEAD_DIM), lambda i: (0, 0))],
        out_specs=[spec, spec, spec], out_shape=[tab, tab, tab],
        compiler_params=_params("parallel"), name="rope_tables",
    )(positions.reshape(m, 1), inv_row)


def _rmsnorm_kernel(x_ref, g_ref, o_ref):
    o_ref[...] = _rms(x_ref[...], g_ref[...]).astype(o_ref.dtype)


def rmsnorm_bf16(x, gain):
    m, d = x.shape
    tm = 512
    return pl.pallas_call(
        _rmsnorm_kernel, grid=(m // tm,),
        in_specs=[pl.BlockSpec((tm, d), lambda i: (i, 0)), pl.BlockSpec((1, d), lambda i: (0, 0))],
        out_specs=pl.BlockSpec((tm, d), lambda i: (i, 0)),
        out_shape=jax.ShapeDtypeStruct((m, d), jnp.bfloat16),
        compiler_params=_params("parallel"), name="rmsnorm",
    )(x, gain.reshape(1, d))


def _qkv_kernel(h_ref, w_ref, gain_ref, cos_ref, sin_lo_ref, sin_hi_ref, o_ref, *, heads_per_tile):
    j = pl.program_id(1)
    acc = jnp.dot(h_ref[...], w_ref[...], preferred_element_type=jnp.float32)
    tiles_per_part = MIX_WIDTH // (heads_per_tile * HEAD_DIM)
    normed = (j < 2 * tiles_per_part) & (j % tiles_per_part != tiles_per_part - 1)

    @pl.when(normed)
    def _():
        c, s_lo, s_hi = cos_ref[...], sin_lo_ref[...], sin_hi_ref[...]
        for hd in range(heads_per_tile):
            sl = slice(hd * HEAD_DIM, (hd + 1) * HEAD_DIM)
            y = _rms(acc[:, sl], gain_ref[:, sl])
            y = (y * c + pltpu.roll(y, HEAD_DIM - ROT_HALF, axis=1) * s_lo
                 + pltpu.roll(y, ROT_HALF, axis=1) * s_hi)
            o_ref[:, sl] = y.astype(o_ref.dtype)

    @pl.when(jnp.logical_not(normed))
    def _():
        o_ref[...] = acc.astype(o_ref.dtype)


def qkv_project(h, w_qkv, gain_cols, tables):
    m, d = h.shape
    n = w_qkv.shape[1]
    tm, tn = 1024, 512
    assert (N_HEADS_A + N_HEADS_B) * HEAD_DIM == MIX_WIDTH - tn
    tab_spec = pl.BlockSpec((tm, HEAD_DIM), lambda i, j: (i, 0))
    return pl.pallas_call(
        functools.partial(_qkv_kernel, heads_per_tile=tn // HEAD_DIM), grid=(m // tm, n // tn),
        in_specs=[pl.BlockSpec((tm, d), lambda i, j: (i, 0)), pl.BlockSpec((d, tn), lambda i, j: (0, j)),
                  pl.BlockSpec((1, tn), lambda i, j: (0, j)), tab_spec, tab_spec, tab_spec],
        out_specs=pl.BlockSpec((tm, tn), lambda i, j: (i, j)),
        out_shape=jax.ShapeDtypeStruct((m, n), jnp.bfloat16),
        compiler_params=_params("parallel", "arbitrary"), name="qkv_project",
    )(h, w_qkv, gain_cols, *tables)


def _gate_kernel(h_ref, w_ref, b_ref, o_ref):
    acc = jnp.dot(h_ref[...], w_ref[...], preferred_element_type=jnp.float32)
    o_ref[...] = _sigmoid(acc + b_ref[...]).astype(o_ref.dtype)


def gate_project(h, w_gate, b_gate):
    m, d = h.shape
    n = w_gate.shape[1]
    tm, tn = 1024, 512
    return pl.pallas_call(
        _gate_kernel, grid=(m // tm, n // tn),
        in_specs=[pl.BlockSpec((tm, d), lambda i, j: (i, 0)), pl.BlockSpec((d, tn), lambda i, j: (0, j)),
                  pl.BlockSpec((1, tn), lambda i, j: (0, j))],
        out_specs=pl.BlockSpec((tm, tn), lambda i, j: (i, j)),
        out_shape=jax.ShapeDtypeStruct((m, n), jnp.bfloat16),
        compiler_params=_params("parallel", "arbitrary"), name="gate_project",
    )(h, w_gate, b_gate.reshape(1, n))


def _transpose_values(v_ref, vt_ref):
    def one(c, carry):
        st = pl.multiple_of(c * KEY_TILE, KEY_TILE)
        vt_ref[c] = v_ref[0, pl.ds(st, KEY_TILE), :].astype(jnp.float32).T.astype(vt_ref.dtype)
        return carry
    lax.fori_loop(0, vt_ref.shape[0], one, 0)


def _softmax_tile(st, vt_tile, m_ref, l_ref, acc_ref):
    m_old = m_ref[...]
    m_new = jnp.maximum(m_old, jnp.max(st, axis=0, keepdims=True))
    alpha = jnp.exp(m_old - m_new)
    p = jnp.exp(st - m_new)
    l_ref[...] = alpha * l_ref[...] + jnp.sum(p, axis=0, keepdims=True)
    acc_ref[...] = alpha * acc_ref[...] + jnp.dot(vt_tile, p.astype(vt_tile.dtype),
                                                  preferred_element_type=jnp.float32)
    m_ref[...] = m_new


def _softmax_init(m_ref, l_ref, acc_ref):
    m_ref[...] = jnp.full_like(m_ref, -jnp.inf)
    l_ref[...] = jnp.zeros_like(l_ref)
    acc_ref[...] = jnp.zeros_like(acc_ref)


def _dilated_kernel(q0_ref, q1_ref, q2_ref, k0_ref, k1_ref, k2_ref, v0_ref, v1_ref, v2_ref, o_ref,
                    vt_ref, m_ref, l_ref, acc_ref):
    qi = pl.program_id(2)
    q_refs, k_refs, v_refs = (q0_ref, q1_ref, q2_ref), (k0_ref, k1_ref, k2_ref), (v0_ref, v1_ref, v2_ref)

    @pl.when(qi == 0)
    def _():
        for g in range(len(DILATIONS)):
            _transpose_values(v_refs[g], vt_ref.at[g])

    shape = (KEY_TILE, KEY_TILE)
    back = lax.broadcasted_iota(jnp.int32, shape, 1) - lax.broadcasted_iota(jnp.int32, shape, 0)
    _softmax_init(m_ref, l_ref, acc_ref)
    for g, d in enumerate(DILATIONS):
        q = q_refs[g][0]
        on_grid = (back & (d - 1)) == 0
        neg = jnp.float32(-jnp.inf)
        bias_mid = jnp.where(on_grid, 0.0, neg)
        bias_diag = jnp.where(back >= 0, bias_mid, neg)
        bias_last = jnp.where(back <= 0, bias_mid, neg)

        def tile(kt, bias, g=g, q=q):
            st = pl.multiple_of(kt * KEY_TILE, KEY_TILE)
            scores = lax.dot_general(k_refs[g][0, pl.ds(st, KEY_TILE), :], q, _NT,
                                     preferred_element_type=jnp.float32) * SCALE + bias
            _softmax_tile(scores, vt_ref[g, kt], m_ref, l_ref, acc_ref)

        tile(qi, bias_diag)
        def middle(a, carry, tile=tile, bias_mid=bias_mid):
            tile(qi - a, bias_mid)
            return carry
        lax.fori_loop(1, jnp.minimum(d - 1, qi) + 1, middle, 0)

        @pl.when(qi >= d)
        def _(tile=tile, bias_last=bias_last, d=d):
            tile(qi - d, bias_last)

    o_ref[0] = (acc_ref[...] / l_ref[...]).T.astype(o_ref.dtype)


def dilated_attention(qkv):
    b, s, _ = qkv.shape
    n_tiles = s // KEY_TILE
    assert WINDOW_STEPS == KEY_TILE and all(d & (d - 1) == 0 for d in DILATIONS)

    def q_spec(g):
        return pl.BlockSpec((1, KEY_TILE, HEAD_DIM), lambda bi, j, i: (bi, i, g * HEADS_PER_GROUP + j))

    def full(part, g):
        return pl.BlockSpec((1, s, HEAD_DIM),
                            lambda bi, j, i: (bi, 0, part * N_HEADS_MIX + g * HEADS_PER_GROUP + j))

    groups = range(len(DILATIONS))
    return pl.pallas_call(
        _dilated_kernel, grid=(b, HEADS_PER_GROUP, n_tiles),
        in_specs=[q_spec(g) for g in groups] + [full(1, g) for g in groups] + [full(2, g) for g in groups],
        out_specs=pl.BlockSpec((1, KEY_TILE, HEAD_DIM), lambda bi, j, i: (bi, i, j)),
        out_shape=jax.ShapeDtypeStruct((b, s, HEADS_PER_GROUP * HEAD_DIM), jnp.bfloat16),
        scratch_shapes=[pltpu.VMEM((len(DILATIONS), n_tiles, HEAD_DIM, KEY_TILE), jnp.bfloat16),
                        pltpu.VMEM((1, KEY_TILE), jnp.float32), pltpu.VMEM((1, KEY_TILE), jnp.float32),
                        pltpu.VMEM((HEAD_DIM, KEY_TILE), jnp.float32)],
        compiler_params=_params("parallel", "parallel", "arbitrary"), name="dilated_attention",
    )(*([qkv] * 9)).reshape(b * s, HEADS_PER_GROUP * HEAD_DIM)


def _moba_kernel(q_ref, k_ref, v_ref, o_ref, kmean_ref, vt_ref, bias_ref, m_ref, l_ref, acc_ref, *, n_blocks):
    own = pl.program_id(2)

    @pl.when(own == 0)
    def _():
        for n in range(n_blocks):
            kb = k_ref[0, n * MOBA_BLOCK:(n + 1) * MOBA_BLOCK, :].astype(jnp.float32)
            kmean_ref[n:n + 1, :] = jnp.sum(kb, axis=0, keepdims=True) / MOBA_BLOCK

        _transpose_values(v_ref, vt_ref)

    q = q_ref[0]
    neg = jnp.float32(-jnp.inf)
    gate = lax.dot_general(kmean_ref[...], q.astype(jnp.float32), _NT,
                           precision=lax.Precision.HIGHEST, preferred_element_type=jnp.float32)
    blk = lax.broadcasted_iota(jnp.int32, gate.shape, 0)
    gate = jnp.where(blk < own, gate, neg)
    rank = jnp.zeros(gate.shape, jnp.int32)
    for mth in range(n_blocks):
        gm = gate[mth:mth + 1, :]
        lower = jnp.where(mth < blk, 1, 0)
        rank = rank + jnp.where(gm > gate, 1, 0) + jnp.where(gm == gate, lower, 0)
    rank = jnp.where(blk < own, rank, MOBA_TOPK)
    bias_ref[...] = jnp.where(rank < MOBA_TOPK, 0.0, neg)

    sub = lax.broadcasted_iota(jnp.int32, (KEY_TILE, MOBA_BLOCK), 0)
    lane = lax.broadcasted_iota(jnp.int32, (KEY_TILE, MOBA_BLOCK), 1)
    _softmax_init(m_ref, l_ref, acc_ref)

    def tile(kt, bias):
        st = pl.multiple_of(kt * KEY_TILE, KEY_TILE)
        scores = lax.dot_general(k_ref[0, pl.ds(st, KEY_TILE), :], q, _NT,
                                 preferred_element_type=jnp.float32) * SCALE + bias
        _softmax_tile(scores, vt_ref[kt], m_ref, l_ref, acc_ref)

    for u in range(TILES_PER_BLOCK):
        tile(own * TILES_PER_BLOCK + u, jnp.where(u * KEY_TILE + sub <= lane, 0.0, neg))

    def past_tile(kt, carry):
        tile(kt, bias_ref[pl.ds(kt // TILES_PER_BLOCK, 1), :])
        return carry

    lax.fori_loop(0, own * TILES_PER_BLOCK, past_tile, 0)
    o_ref[0] = (acc_ref[...] / l_ref[...]).T.astype(o_ref.dtype)


def moba_attention(qkv):
    b, s, _ = qkv.shape
    n_blocks = s // MOBA_BLOCK
    assert s % MOBA_BLOCK == 0 and n_blocks >= MOBA_TOPK

    def full(part):
        return pl.BlockSpec((1, s, HEAD_DIM), lambda bi, h, i: (bi, 0, part * N_HEADS_MIX + N_HEADS_A + h))

    return pl.pallas_call(
        functools.partial(_moba_kernel, n_blocks=n_blocks), grid=(b, N_HEADS_B, n_blocks),
        in_specs=[pl.BlockSpec((1, MOBA_BLOCK, HEAD_DIM), lambda bi, h, i: (bi, i, N_HEADS_A + h)),
                  full(1), full(2)],
        out_specs=pl.BlockSpec((1, MOBA_BLOCK, HEAD_DIM), lambda bi, h, i: (bi, i, h)),
        out_shape=jax.ShapeDtypeStruct((b, s, N_HEADS_B * HEAD_DIM), jnp.bfloat16),
        scratch_shapes=[pltpu.VMEM((n_blocks, HEAD_DIM), jnp.float32),
                        pltpu.VMEM((s // KEY_TILE, HEAD_DIM, KEY_TILE), jnp.bfloat16),
                        pltpu.VMEM((n_blocks, MOBA_BLOCK), jnp.float32), pltpu.VMEM((1, MOBA_BLOCK), jnp.float32),
                        pltpu.VMEM((1, MOBA_BLOCK), jnp.float32), pltpu.VMEM((HEAD_DIM, MOBA_BLOCK), jnp.float32)],
        compiler_params=_params("parallel", "parallel", "arbitrary"), name="moba_attention",
    )(qkv, qkv, qkv).reshape(b * s, N_HEADS_B * HEAD_DIM)


STICK_TILE = 256
STICK_CUTOFF = -104.0


def _stick_kernel(q_ref, k_ref, v_ref, o_ref, vt_ref, run_ref, acc_ref):
    qi = pl.program_id(2)

    @pl.when(qi == 0)
    def _():
        _transpose_values(v_ref, vt_ref)

    q = q_ref[0]
    tiles_per_q = STICK_TILE // KEY_TILE
    sq = lax.broadcasted_iota(jnp.int32, (KEY_TILE, KEY_TILE), 0)
    sk = lax.broadcasted_iota(jnp.int32, (KEY_TILE, KEY_TILE), 1)
    later = jnp.where(sk > sq, 1.0, 0.0).astype(jnp.bfloat16)
    sub = lax.broadcasted_iota(jnp.int32, (KEY_TILE, STICK_TILE), 0)
    lane = lax.broadcasted_iota(jnp.int32, (KEY_TILE, STICK_TILE), 1)

    def tile(kt, valid):
        st = pl.multiple_of(kt * KEY_TILE, KEY_TILE)
        z = lax.dot_general(k_ref[0, pl.ds(st, KEY_TILE), :], q, _NT,
                            preferred_element_type=jnp.float32) * SCALE
        log_1m = -(jnp.maximum(z, 0.0) + jnp.log(1.0 + jnp.exp(-jnp.abs(z))))
        if valid is not None:
            log_1m = jnp.where(valid, log_1m, 0.0)
        hi = log_1m.astype(jnp.bfloat16)
        lo = (log_1m - hi.astype(jnp.float32)).astype(jnp.bfloat16)
        inside = (jnp.dot(later, hi, preferred_element_type=jnp.float32)
                  + jnp.dot(later, lo, preferred_element_type=jnp.float32))
        a = jnp.exp(z + log_1m + inside + run_ref[...])
        if valid is not None:
            a = jnp.where(valid, a, 0.0)
        acc_ref[...] += jnp.dot(vt_ref[kt], a.astype(jnp.bfloat16), preferred_element_type=jnp.float32)
        run_ref[...] += inside[0:1, :] + log_1m[0:1, :]

    def alive():
        return (jnp.max(run_ref[...]) > STICK_CUTOFF).astype(jnp.int32)

    run_ref[...] = jnp.zeros_like(run_ref)
    acc_ref[...] = jnp.zeros_like(acc_ref)
    for u in reversed(range(tiles_per_q)):
        tile(qi * tiles_per_q + u, u * KEY_TILE + sub < lane)

    def earlier(carry):
        kt, _ = carry
        tile(kt, None)
        return kt - 1, alive()

    lax.while_loop(lambda c: (c[0] >= 0) & (c[1] > 0), earlier, (qi * tiles_per_q - 1, alive()))
    o_ref[0] = acc_ref[...].T.astype(o_ref.dtype)


def stick_attention(qkv):
    b, s, _ = qkv.shape
    t = STICK_TILE
    head0 = N_HEADS_A + N_HEADS_B

    def full(part):
        return pl.BlockSpec((1, s, HEAD_DIM), lambda bi, h, i: (bi, 0, part * N_HEADS_MIX + head0 + h))

    return pl.pallas_call(
        _stick_kernel, grid=(b, N_HEADS_C, s // t),
        in_specs=[pl.BlockSpec((1, t, HEAD_DIM), lambda bi, h, i: (bi, i, head0 + h)), full(1), full(2)],
        out_specs=pl.BlockSpec((1, t, HEAD_DIM), lambda bi, h, i: (bi, i, h)),
        out_shape=jax.ShapeDtypeStruct((b, s, N_HEADS_C * HEAD_DIM), jnp.bfloat16),
        scratch_shapes=[pltpu.VMEM((s // KEY_TILE, HEAD_DIM, KEY_TILE), jnp.bfloat16),
                        pltpu.VMEM((1, t), jnp.float32), pltpu.VMEM((HEAD_DIM, t), jnp.float32)],
        compiler_params=_params("parallel", "parallel", "arbitrary"), name="stick_attention",
    )(qkv, qkv, qkv).reshape(b * s, N_HEADS_C * HEAD_DIM)


def _merge_kernel(oa_ref, ob_ref, oc_ref, g_ref, wa_ref, wb_ref, wc_ref, out_ref):
    d = out_ref.shape[1]
    ya = jnp.dot(oa_ref[...], wa_ref[...], preferred_element_type=jnp.float32)
    yb = jnp.dot(ob_ref[...], wb_ref[...], preferred_element_type=jnp.float32)
    yc = jnp.dot(oc_ref[...], wc_ref[...], preferred_element_type=jnp.float32)
    merged = (g_ref[:, 0:d].astype(jnp.float32) * ya + g_ref[:, d:2 * d].astype(jnp.float32) * yb
              + g_ref[:, 2 * d:3 * d].astype(jnp.float32) * yc)
    out_ref[...] = merged.astype(out_ref.dtype)


def merge_branches(o_a, o_b, o_c, gates, w_a, w_b, w_c):
    m = o_b.shape[0]
    d = w_a.shape[1]
    tm = 512

    def rows(a):
        return pl.BlockSpec((tm, a.shape[1]), lambda i: (i, 0))

    def whole(w):
        return pl.BlockSpec(w.shape, lambda i: (0, 0))

    return pl.pallas_call(
        _merge_kernel, grid=(m // tm,),
        in_specs=[rows(o_a), rows(o_b), rows(o_c), rows(gates), whole(w_a), whole(w_b), whole(w_c)],
        out_specs=pl.BlockSpec((tm, d), lambda i: (i, 0)), out_shape=jax.ShapeDtypeStruct((m, d), jnp.bfloat16),
        compiler_params=_params("parallel"), name="merge_branches",
    )(o_a, o_b, o_c, gates, w_a, w_b, w_c)


def _out_proj_kernel(a_ref, w_ref, x_ref, g_ref, xo_ref, ho_ref):
    xn = x_ref[...] + jnp.dot(a_ref[...], w_ref[...], preferred_element_type=jnp.float32)
    xo_ref[...] = xn
    ho_ref[...] = _rms(xn, g_ref[...]).astype(ho_ref.dtype)


def out_project(a, w, x, next_gain):
    m, k = a.shape
    d = w.shape[1]
    tm = 512
    return pl.pallas_call(
        _out_proj_kernel, grid=(m // tm,),
        in_specs=[pl.BlockSpec((tm, k), lambda i: (i, 0)), pl.BlockSpec((k, d), lambda i: (0, 0)),
                  pl.BlockSpec((tm, d), lambda i: (i, 0)), pl.BlockSpec((1, d), lambda i: (0, 0))],
        out_specs=[pl.BlockSpec((tm, d), lambda i: (i, 0)), pl.BlockSpec((tm, d), lambda i: (i, 0))],
        out_shape=[jax.ShapeDtypeStruct((m, d), jnp.float32), jax.ShapeDtypeStruct((m, d), jnp.bfloat16)],
        compiler_params=_params("parallel"), name="out_project",
    )(a, w, x, next_gain.reshape(1, d))


def _mem_kv_kernel(mem_ref, ln_ref, w_ref, gk_ref, kv_ref):
    hm = _rms(mem_ref[...], ln_ref[...]).astype(jnp.bfloat16)
    kv = jnp.dot(hm, w_ref[...], preferred_element_type=jnp.float32)
    half = kv.shape[1] // 2
    for hd in range(N_HEADS_MEM):
        sl = slice(hd * HEAD_DIM, (hd + 1) * HEAD_DIM)
        kv_ref[:, sl] = _rms(kv[:, sl], gk_ref[...]).astype(kv_ref.dtype)
    kv_ref[:, half:] = kv[:, half:].astype(kv_ref.dtype)


def mem_kv(mem2d, ln, wm_kv, gain_k):
    n, d = mem2d.shape
    w = wm_kv.shape[1]
    return pl.pallas_call(
        _mem_kv_kernel, grid=(1,),
        in_specs=[pl.BlockSpec((n, d), lambda i: (0, 0)), pl.BlockSpec((1, d), lambda i: (0, 0)),
                  pl.BlockSpec((d, w), lambda i: (0, 0)), pl.BlockSpec((1, HEAD_DIM), lambda i: (0, 0))],
        out_specs=pl.BlockSpec((n, w), lambda i: (0, 0)),
        out_shape=jax.ShapeDtypeStruct((n, w), jnp.bfloat16),
        compiler_params=_params("arbitrary"), name="mem_kv",
    )(mem2d, ln.reshape(1, d), wm_kv, gain_k.reshape(1, HEAD_DIM))


def _mem_attn_kernel(h_ref, wq_ref, gq_ref, kv_ref, wo_ref, x_ref, g_ref, xo_ref, ho_ref):
    qf = jnp.dot(h_ref[...], wq_ref[...], preferred_element_type=jnp.float32)
    half = kv_ref.shape[2] // 2
    outs = []
    for hd in range(N_HEADS_MEM):
        sl = slice(hd * HEAD_DIM, (hd + 1) * HEAD_DIM)
        qh = _rms(qf[:, sl], gq_ref[...]).astype(jnp.bfloat16)
        s = lax.dot_general(qh, kv_ref[0, :, sl], _NT, preferred_element_type=jnp.float32) * SCALE
        e = jnp.exp(s - jnp.max(s, axis=1, keepdims=True))
        vh = kv_ref[0, :, half + hd * HEAD_DIM:half + (hd + 1) * HEAD_DIM]
        o = jnp.dot(e.astype(jnp.bfloat16), vh, preferred_element_type=jnp.float32)
        outs.append((o / jnp.sum(e, axis=1, keepdims=True)).astype(jnp.bfloat16))
    o_all = jnp.concatenate(outs, axis=1)
    xn = x_ref[...] + jnp.dot(o_all, wo_ref[...], preferred_element_type=jnp.float32)
    xo_ref[...] = xn
    ho_ref[...] = _rms(xn, g_ref[...]).astype(ho_ref.dtype)


def mem_attention(h, wm_q, gain_q, kv, wm_o, x, next_gain, seq):
    m, d = h.shape
    wq = wm_q.shape[1]
    tm = 512
    per_batch = seq // tm
    return pl.pallas_call(
        _mem_attn_kernel, grid=(m // tm,),
        in_specs=[pl.BlockSpec((tm, d), lambda i: (i, 0)), pl.BlockSpec((d, wq), lambda i: (0, 0)),
                  pl.BlockSpec((1, HEAD_DIM), lambda i: (0, 0)),
                  pl.BlockSpec((1,) + kv.shape[1:], lambda i: (i // per_batch, 0, 0)),
                  pl.BlockSpec((wq, d), lambda i: (0, 0)), pl.BlockSpec((tm, d), lambda i: (i, 0)),
                  pl.BlockSpec((1, d), lambda i: (0, 0))],
        out_specs=[pl.BlockSpec((tm, d), lambda i: (i, 0)), pl.BlockSpec((tm, d), lambda i: (i, 0))],
        out_shape=[jax.ShapeDtypeStruct((m, d), jnp.float32), jax.ShapeDtypeStruct((m, d), jnp.bfloat16)],
        compiler_params=_params("parallel"), name="mem_attention",
    )(h, wm_q, gain_q.reshape(1, HEAD_DIM), kv, wm_o, x, next_gain.reshape(1, d))


def _ffn_kernel(h_ref, wg_ref, wv_ref, cwg_ref, cwv_ref, cbg_ref, cbv_ref, wd_ref, x_ref, o_ref,
                halo_ref, ug_ref, uv_ref, acc_ref, *, tiles_per_seq):
    i, f = pl.program_id(0), pl.program_id(1)
    tm = h_ref.shape[0]

    def conv(w_ref, cw_ref, cb_ref, u_ref, part):
        u = jnp.dot(h_ref[...], w_ref[...], preferred_element_type=jnp.float32)
        prev = halo_ref[f, part]
        u_ref[0:HALO, :] = jnp.where(i % tiles_per_seq == 0, jnp.zeros_like(prev), prev)
        u_ref[HALO:, :] = u
        halo_ref[f, part] = u[tm - HALO:, :]
        return (cw_ref[0:1, :] * u_ref[HALO - 2:HALO - 2 + tm, :] + cw_ref[1:2, :] * u_ref[HALO - 1:HALO - 1 + tm, :]
                + cw_ref[2:3, :] * u + cb_ref[...])

    yg = conv(wg_ref, cwg_ref, cbg_ref, ug_ref, 0)
    yv = conv(wv_ref, cwv_ref, cbv_ref, uv_ref, 1)
    act = (yg * _sigmoid(yg) * yv).astype(jnp.bfloat16)
    part = jnp.dot(act, wd_ref[...], preferred_element_type=jnp.float32)

    @pl.when(f == 0)
    def _():
        acc_ref[...] = part

    @pl.when(f > 0)
    def _():
        acc_ref[...] += part

    @pl.when(f == pl.num_programs(1) - 1)
    def _():
        o_ref[...] = x_ref[...] + acc_ref[...]


def conv_ffn(h, w_g, w_v, cw_g, cw_v, cb_g, cb_v, w_down, x, seq):
    m, d = h.shape
    fp = w_g.shape[1]
    tm, tf = 512, FF_TILE
    nf = fp // tf
    assert CONV_WIDTH - 1 <= HALO

    def cols(rows):
        return pl.BlockSpec((rows, tf), lambda i, f: (0, f))

    return pl.pallas_call(
        functools.partial(_ffn_kernel, tiles_per_seq=seq // tm), grid=(m // tm, nf),
        in_specs=[pl.BlockSpec((tm, d), lambda i, f: (i, 0)), cols(d), cols(d), cols(CONV_WIDTH), cols(CONV_WIDTH),
                  cols(1), cols(1), pl.BlockSpec((tf, d), lambda i, f: (f, 0)),
                  pl.BlockSpec((tm, d), lambda i, f: (i, 0))],
        out_specs=pl.BlockSpec((tm, d), lambda i, f: (i, 0)),
        out_shape=jax.ShapeDtypeStruct((m, d), jnp.float32),
        scratch_shapes=[pltpu.VMEM((nf, 2, HALO, tf), jnp.float32), pltpu.VMEM((tm + HALO, tf), jnp.float32),
                        pltpu.VMEM((tm + HALO, tf), jnp.float32), pltpu.VMEM((tm, d), jnp.float32)],
        compiler_params=_params("arbitrary", "arbitrary"), name="conv_ffn",
    )(h, w_g, w_v, cw_g, cw_v, cb_g, cb_v, w_down, x)


def _pad_cols(a, width):
    return jnp.pad(a, ((0, 0), (0, width - a.shape[1])))


def kernel(x, mem, positions, ln_mix, w_qkv, qk_gain, w_br_a, w_br_b, w_br_c, w_gate, b_gate, w_o,
           ln_mem_q, ln_mem_kv, wm_q, wm_kv, wm_o, mem_qk_gain, ln_ffn, w_up, conv_w, conv_b, w_down):
    b, s, d = x.shape
    depth = ln_mix.shape[0]
    bf = jnp.bfloat16
    d_ff = w_down.shape[1]
    fp = -(-d_ff // FF_TILE) * FF_TILE

    tables = rope_tables(positions)
    xf = x.reshape(b * s, d)
    mem2d = mem.reshape(b * mem.shape[1], d)
    h = rmsnorm_bf16(xf, ln_mix[0])
    for l in range(depth):
        zeros_c = jnp.zeros((N_HEADS_C * HEAD_DIM,), jnp.float32)
        gain_cols = jnp.concatenate(
            [jnp.tile(qk_gain[l, 0], N_HEADS_A), jnp.tile(qk_gain[l, 2], N_HEADS_B), zeros_c,
             jnp.tile(qk_gain[l, 1], N_HEADS_A), jnp.tile(qk_gain[l, 3], N_HEADS_B), zeros_c,
             jnp.zeros((MIX_WIDTH,), jnp.float32)]).reshape(1, 3 * MIX_WIDTH)
        qkv = qkv_project(h, w_qkv[l].astype(bf), gain_cols, tables).reshape(b, s, 3 * MIX_WIDTH)
        gates = gate_project(h, w_gate[l].astype(bf), b_gate[l])
        o_a = dilated_attention(qkv)
        o_b = moba_attention(qkv)
        o_c = stick_attention(qkv)
        merged = merge_branches(o_a, o_b, o_c, gates,
                                w_br_a[l].astype(bf), w_br_b[l].astype(bf), w_br_c[l].astype(bf))
        xf, h = out_project(merged, w_o[l].astype(bf), xf, ln_mem_q[l])

        kv = mem_kv(mem2d, ln_mem_kv[l], wm_kv[l].astype(bf), mem_qk_gain[l, 1])
        kv = kv.reshape(b, mem.shape[1], kv.shape[1])
        xf, h = mem_attention(h, wm_q[l].astype(bf), mem_qk_gain[l, 0], kv, wm_o[l].astype(bf), xf,
                              ln_ffn[l], s)

        w_g = _pad_cols(w_up[l, :, :d_ff], fp).astype(bf)
        w_v = _pad_cols(w_up[l, :, d_ff:], fp).astype(bf)
        cw_g, cw_v = _pad_cols(conv_w[l, :, :d_ff], fp), _pad_cols(conv_w[l, :, d_ff:], fp)
        cb_g = _pad_cols(conv_b[l, :d_ff].reshape(1, d_ff), fp)
        cb_v = _pad_cols(conv_b[l, d_ff:].reshape(1, d_ff), fp)
        w_d = jnp.pad(w_down[l], ((0, fp - d_ff), (0, 0))).astype(bf)
        xf = conv_ffn(h, w_g, w_v, cw_g, cw_v, cb_g, cb_v, w_d, xf, s)
        if l + 1 < depth:
            h = rmsnorm_bf16(xf, ln_mix[l + 1])
    return xf.reshape(b, s, d)
```

```python
import functools

import jax
import jax.numpy as jnp
from jax import lax
from jax.experimental import pallas as pl
from jax.experimental.pallas import tpu as pltpu

HEAD_DIM = 128
DILATIONS = (1, 4, 16)
WINDOW_STEPS = 128
HEADS_PER_GROUP = 2
N_HEADS_A = 6
N_HEADS_B = 6
N_HEADS_C = 4
N_HEADS_MIX = 16
MIX_WIDTH = N_HEADS_MIX * HEAD_DIM
KEY_TILE = 128
MOBA_BLOCK = 256
MOBA_TOPK = 3
MOBA_CHUNK_BLOCKS = 2
N_HEADS_MEM = 4
ROPE_THETA = 500000.0
ROT_DIM = HEAD_DIM // 4
ROT_HALF = ROT_DIM // 2
CONV_WIDTH = 3
EPS = 1e-6
SCALE = HEAD_DIM ** -0.5
FF_TILE = 512
FF_ROWS = 256
HALO = 8
VMEM_LIMIT = 56 * 1024 * 1024

_NT = (((1,), (1,)), ((), ()))


def _params(*sem):
    return pltpu.CompilerParams(dimension_semantics=sem, vmem_limit_bytes=VMEM_LIMIT)


def _rms(y, gain):
    return y * lax.rsqrt(jnp.mean(y * y, axis=-1, keepdims=True) + EPS) * gain


def _sigmoid(y):
    return 1.0 / (1.0 + jnp.exp(-y))


def _rope_table_kernel(pos_ref, inv_ref, cos_ref, sin_lo_ref, sin_hi_ref):
    ang = pos_ref[...].astype(jnp.float32) * inv_ref[...]
    lane = lax.broadcasted_iota(jnp.int32, ang.shape, 1)
    s = jnp.sin(ang)
    cos_ref[...] = jnp.cos(ang)
    sin_lo_ref[...] = jnp.where(lane < ROT_HALF, -s, 0.0)
    sin_hi_ref[...] = jnp.where((lane >= ROT_HALF) & (lane < ROT_DIM), s, 0.0)


def rope_tables(positions):
    m = positions.size
    tm = 1024
    inv = ROPE_THETA ** (-jnp.arange(0, ROT_DIM, 2, dtype=jnp.float32) / ROT_DIM)
    inv_row = jnp.zeros((1, HEAD_DIM), jnp.float32).at[0, :ROT_DIM].set(jnp.concatenate([inv, inv]))
    tab = jax.ShapeDtypeStruct((m, HEAD_DIM), jnp.float32)
    spec = pl.BlockSpec((tm, HEAD_DIM), lambda i: (i, 0))
    return pl.pallas_call(
        _rope_table_kernel, grid=(m // tm,),
        in_specs=[pl.BlockSpec((tm, 1), lambda i: (i, 0)), pl.BlockSpec((1, HEAD_DIM), lambda i: (0, 0))],
        out_specs=[spec, spec, spec], out_shape=[tab, tab, tab],
        compiler_params=_params("parallel"), name="rope_tables",
    )(positions.reshape(m, 1), inv_row)


def _rmsnorm_kernel(x_ref, g_ref, o_ref):
    o_ref[...] = _rms(x_ref[...], g_ref[...]).astype(o_ref.dtype)


def rmsnorm_bf16(x, gain):
    m, d = x.shape
    tm = 512
    return pl.pallas_call(
        _rmsnorm_kernel, grid=(m // tm,),
        in_specs=[pl.BlockSpec((tm, d), lambda i: (i, 0)), pl.BlockSpec((1, d), lambda i: (0, 0))],
        out_specs=pl.BlockSpec((tm, d), lambda i: (i, 0)),
        out_shape=jax.ShapeDtypeStruct((m, d), jnp.bfloat16),
        compiler_params=_params("parallel"), name="rmsnorm",
    )(x, gain.reshape(1, d))


PROJ_TILE = 512
PROJ_ROWS = 256
NORMED_HEADS = N_HEADS_A + N_HEADS_B
QK_K_BLOCK = NORMED_HEADS
REST_KC_BLOCK = N_HEADS_C
REST_V_BLOCK = 2 * N_HEADS_C


def _qk_norm_kernel(h_ref, w_ref, gain_ref, ones_ref, cos_ref, sin_lo_ref, sin_hi_ref, o_ref):
    for r in range(h_ref.shape[0] // PROJ_ROWS):
        rows = slice(r * PROJ_ROWS, (r + 1) * PROJ_ROWS)
        acc = jnp.dot(h_ref[rows, :], w_ref[...], preferred_element_type=jnp.float32)
        ss = jnp.dot((acc * acc).astype(jnp.bfloat16), ones_ref[...], preferred_element_type=jnp.float32)
        y = acc * lax.rsqrt(ss * (1.0 / HEAD_DIM) + EPS) * gain_ref[...]
        c, s_lo, s_hi = cos_ref[rows, :], sin_lo_ref[rows, :], sin_hi_ref[rows, :]
        for hd in range(PROJ_TILE // HEAD_DIM):
            sl = slice(hd * HEAD_DIM, (hd + 1) * HEAD_DIM)
            yh = y[:, sl]
            yh = (yh * c + pltpu.roll(yh, HEAD_DIM - ROT_HALF, axis=1) * s_lo
                  + pltpu.roll(yh, ROT_HALF, axis=1) * s_hi)
            o_ref[rows, sl] = yh.astype(o_ref.dtype)


def _plain_proj_kernel(h_ref, w_ref, o_ref):
    o_ref[...] = jnp.dot(h_ref[...], w_ref[...], preferred_element_type=jnp.float32).astype(o_ref.dtype)


def qkv_project(h, w_qkv, gain_cols, tables):
    m, d = h.shape
    tm, tn = 1024, PROJ_TILE
    per_part = MIX_WIDTH // tn
    normed_tiles = NORMED_HEADS * HEAD_DIM // tn
    assert normed_tiles * tn == NORMED_HEADS * HEAD_DIM and normed_tiles + 1 == per_part
    n_out = 2 * normed_tiles * tn
    ones = jnp.kron(jnp.eye(tn // HEAD_DIM, dtype=jnp.float32),
                    jnp.ones((HEAD_DIM, HEAD_DIM), jnp.float32)).astype(jnp.bfloat16)
    tab_spec = pl.BlockSpec((tm, HEAD_DIM), lambda i, j: (i, 0))
    h_spec = pl.BlockSpec((tm, d), lambda i, j: (i, 0))
    out_spec = pl.BlockSpec((tm, tn), lambda i, j: (i, j))
    out_sds = jax.ShapeDtypeStruct((m, n_out), jnp.bfloat16)
    qk = pl.pallas_call(
        _qk_norm_kernel, grid=(m // tm, 2 * normed_tiles),
        in_specs=[h_spec, pl.BlockSpec((d, tn), lambda i, j: (0, j + j // normed_tiles)),
                  pl.BlockSpec((1, tn), lambda i, j: (0, j)), pl.BlockSpec((tn, tn), lambda i, j: (0, 0)),
                  tab_spec, tab_spec, tab_spec],
        out_specs=out_spec, out_shape=out_sds,
        compiler_params=_params("parallel", "arbitrary"), name="qk_norm_project",
    )(h, w_qkv, gain_cols, ones, *tables)
    rest = pl.pallas_call(
        _plain_proj_kernel, grid=(m // tm, 2 + per_part),
        in_specs=[h_spec, pl.BlockSpec(
            (d, tn), lambda i, j: (0, jnp.where(j < 2, normed_tiles + j * per_part, j + 2 * per_part - 2)))],
        out_specs=out_spec, out_shape=out_sds,
        compiler_params=_params("parallel", "arbitrary"), name="plain_project",
    )(h, w_qkv)
    return qk, rest


def _gate_kernel(h_ref, w_ref, b_ref, o_ref):
    acc = jnp.dot(h_ref[...], w_ref[...], preferred_element_type=jnp.float32)
    o_ref[...] = _sigmoid(acc + b_ref[...]).astype(o_ref.dtype)


def gate_project(h, w_gate, b_gate):
    m, d = h.shape
    n = w_gate.shape[1]
    tm, tn = 1024, 512
    return pl.pallas_call(
        _gate_kernel, grid=(m // tm, n // tn),
        in_specs=[pl.BlockSpec((tm, d), lambda i, j: (i, 0)), pl.BlockSpec((d, tn), lambda i, j: (0, j)),
                  pl.BlockSpec((1, tn), lambda i, j: (0, j))],
        out_specs=pl.BlockSpec((tm, tn), lambda i, j: (i, j)),
        out_shape=jax.ShapeDtypeStruct((m, n), jnp.bfloat16),
        compiler_params=_params("parallel", "arbitrary"), name="gate_project",
    )(h, w_gate, b_gate.reshape(1, n))


def _transpose_values(v_ref, vt_ref):
    def one(c, carry):
        st = pl.multiple_of(c * KEY_TILE, KEY_TILE)
        vt_ref[c] = v_ref[0, pl.ds(st, KEY_TILE), :].astype(jnp.float32).T.astype(vt_ref.dtype)
        return carry
    lax.fori_loop(0, vt_ref.shape[0], one, 0)


def _softmax_tile(st, vt_tile, m_ref, l_ref, acc_ref):
    m_old = m_ref[...]
    m_new = jnp.maximum(m_old, jnp.max(st, axis=0, keepdims=True))
    alpha = jnp.exp(m_old - m_new)
    p = jnp.exp(st - m_new)
    l_ref[...] = alpha * l_ref[...] + jnp.sum(p, axis=0, keepdims=True)
    acc_ref[...] = alpha * acc_ref[...] + jnp.dot(vt_tile, p.astype(vt_tile.dtype),
                                                  preferred_element_type=jnp.float32)
    m_ref[...] = m_new


def _softmax_init(m_ref, l_ref, acc_ref):
    m_ref[...] = jnp.full_like(m_ref, -jnp.inf)
    l_ref[...] = jnp.zeros_like(l_ref)
    acc_ref[...] = jnp.zeros_like(acc_ref)


DILATED_ROWS = tuple((d + 1) * KEY_TILE for d in DILATIONS)
DILATED_OFFSETS = tuple(sum(DILATED_ROWS[:g]) for g in range(len(DILATIONS)))


def _dilated_kernel(q0_ref, q1_ref, q2_ref, k0_ref, k1_ref, k2_ref, v0_ref, v1_ref, v2_ref, o_ref,
                    vt_ref, bias_ref, ahead_ref, sc_ref):
    qi = pl.program_id(2)
    q_refs, k_refs, v_refs = (q0_ref, q1_ref, q2_ref), (k0_ref, k1_ref, k2_ref), (v0_ref, v1_ref, v2_ref)
    neg = jnp.float32(-jnp.inf)

    @pl.when(qi == 0)
    def _():
        rows = max(DILATED_ROWS)
        ahead = (lax.broadcasted_iota(jnp.int32, (rows, KEY_TILE), 0)
                 - lax.broadcasted_iota(jnp.int32, (rows, KEY_TILE), 1))
        ahead_ref[...] = ahead
        for g, d in enumerate(DILATIONS):
            _transpose_values(v_refs[g], vt_ref.at[g])
            sl = slice(DILATED_OFFSETS[g], DILATED_OFFSETS[g] + DILATED_ROWS[g])
            on_grid = jnp.where((ahead[:DILATED_ROWS[g]] & (d - 1)) == 0, 0.0, neg)
            bias_ref[0, sl, :] = on_grid
            bias_ref[1, sl, :] = jnp.where(ahead[:DILATED_ROWS[g]] >= 0, on_grid, neg)

    m = jnp.full((1, KEY_TILE), neg, jnp.float32)
    starts = []
    for g, d in enumerate(DILATIONS):
        rows, off = DILATED_ROWS[g], DILATED_OFFSETS[g]
        start = jnp.maximum(qi - d, 0)
        starts.append(start)
        st = pl.multiple_of(start * KEY_TILE, KEY_TILE)
        scores = lax.dot_general(k_refs[g][0, pl.ds(st, rows), :], q_refs[g][0], _NT,
                                 preferred_element_type=jnp.float32) * SCALE
        scores = scores + bias_ref[jnp.where(qi >= d, 1, 0), off:off + rows, :]
        scores = jnp.where(ahead_ref[0:rows, :] <= (qi - start) * KEY_TILE, scores, neg)
        sc_ref[off:off + rows, :] = scores
        m = jnp.maximum(m, jnp.max(scores, axis=0, keepdims=True))

    l = jnp.zeros((1, KEY_TILE), jnp.float32)
    acc = jnp.zeros((HEAD_DIM, KEY_TILE), jnp.float32)
    for g, d in enumerate(DILATIONS):
        rows, off = DILATED_ROWS[g], DILATED_OFFSETS[g]
        p = jnp.exp(sc_ref[off:off + rows, :] - m)
        l = l + jnp.sum(p, axis=0, keepdims=True)
        p = p.astype(vt_ref.dtype)
        for a in range(d + 1):
            acc = acc + jnp.dot(vt_ref[g, starts[g] + a], p[a * KEY_TILE:(a + 1) * KEY_TILE, :],
                                preferred_element_type=jnp.float32)
    o_ref[0] = (acc / l).T.astype(o_ref.dtype)


def dilated_attention(qk, rest):
    b, s, _ = qk.shape
    n_tiles = s // KEY_TILE
    assert WINDOW_STEPS == KEY_TILE and all(d & (d - 1) == 0 for d in DILATIONS)
    assert s >= max(DILATED_ROWS)

    def q_spec(g):
        return pl.BlockSpec((1, KEY_TILE, HEAD_DIM), lambda bi, j, i: (bi, i, g * HEADS_PER_GROUP + j))

    def full(first_block, g):
        return pl.BlockSpec((1, s, HEAD_DIM), lambda bi, j, i: (bi, 0, first_block + g * HEADS_PER_GROUP + j))

    groups = range(len(DILATIONS))
    return pl.pallas_call(
        _dilated_kernel, grid=(b, HEADS_PER_GROUP, n_tiles),
        in_specs=([q_spec(g) for g in groups] + [full(QK_K_BLOCK, g) for g in groups]
                  + [full(REST_V_BLOCK, g) for g in groups]),
        out_specs=pl.BlockSpec((1, KEY_TILE, HEAD_DIM), lambda bi, j, i: (bi, i, j)),
        out_shape=jax.ShapeDtypeStruct((b, s, HEADS_PER_GROUP * HEAD_DIM), jnp.bfloat16),
        scratch_shapes=[pltpu.VMEM((len(DILATIONS), n_tiles, HEAD_DIM, KEY_TILE), jnp.bfloat16),
                        pltpu.VMEM((2, sum(DILATED_ROWS), KEY_TILE), jnp.float32),
                        pltpu.VMEM((max(DILATED_ROWS), KEY_TILE), jnp.int32),
                        pltpu.VMEM((sum(DILATED_ROWS), KEY_TILE), jnp.float32)],
        compiler_params=_params("parallel", "parallel", "arbitrary"), name="dilated_attention",
    )(*([qk] * 6 + [rest] * 3)).reshape(b * s, HEADS_PER_GROUP * HEAD_DIM)


def _moba_kernel(q_ref, k_ref, v_ref, o_ref, kmean_ref, vt_ref, bias_ref, sc_ref, m_ref, l_ref, acc_ref, *,
                 n_blocks):
    own = pl.program_id(2)

    @pl.when(own == 0)
    def _():
        for n in range(n_blocks):
            kb = k_ref[0, n * MOBA_BLOCK:(n + 1) * MOBA_BLOCK, :].astype(jnp.float32)
            kmean_ref[n:n + 1, :] = jnp.sum(kb, axis=0, keepdims=True) / MOBA_BLOCK

        _transpose_values(v_ref, vt_ref)

    q = q_ref[0]
    neg = jnp.float32(-jnp.inf)
    gate = lax.dot_general(kmean_ref[...], q.astype(jnp.float32), _NT,
                           precision=lax.Precision.HIGHEST, preferred_element_type=jnp.float32)
    blk = lax.broadcasted_iota(jnp.int32, gate.shape, 0)
    gate = jnp.where(blk < own, gate, neg)
    rank = jnp.zeros(gate.shape, jnp.int32)
    for mth in range(n_blocks):
        gm = gate[mth:mth + 1, :]
        lower = jnp.where(mth < blk, 1, 0)
        rank = rank + jnp.where(gm > gate, 1, 0) + jnp.where(gm == gate, lower, 0)
    rank = jnp.where(blk < own, rank, MOBA_TOPK)
    bias_ref[...] = jnp.where((rank < MOBA_TOPK) | (blk == own), 0.0, neg)

    rows = MOBA_CHUNK_BLOCKS * MOBA_BLOCK
    top = own // MOBA_CHUNK_BLOCKS
    _softmax_init(m_ref, l_ref, acc_ref)

    def chunk_scores(c, causal):
        st = pl.multiple_of(c * rows, rows)
        s = lax.dot_general(k_ref[0, pl.ds(st, rows), :], q, _NT, preferred_element_type=jnp.float32) * SCALE
        s = jnp.concatenate([s[u * MOBA_BLOCK:(u + 1) * MOBA_BLOCK] + bias_ref[pl.ds(c * MOBA_CHUNK_BLOCKS + u, 1), :]
                             for u in range(MOBA_CHUNK_BLOCKS)], axis=0)
        if causal:
            ahead = (lax.broadcasted_iota(jnp.int32, s.shape, 0) - lax.broadcasted_iota(jnp.int32, s.shape, 1))
            s = jnp.where(ahead <= (own - c * MOBA_CHUNK_BLOCKS) * MOBA_BLOCK, s, neg)
        return s

    first = chunk_scores(top, True)
    sc_ref[...] = first

    def step(i, col_max):
        cur = top - i
        m_old = m_ref[...]
        m_new = jnp.maximum(m_old, col_max)
        alpha = jnp.exp(m_old - m_new)
        p = jnp.exp(sc_ref[...] - m_new)
        nxt = chunk_scores(jnp.maximum(cur - 1, 0), False)
        sc_ref[...] = nxt
        l_ref[...] = alpha * l_ref[...] + jnp.sum(p, axis=0, keepdims=True)
        p = p.astype(vt_ref.dtype)
        pv = jnp.zeros(acc_ref.shape, jnp.float32)
        for u in range(rows // KEY_TILE):
            pv = pv + jnp.dot(vt_ref[cur * (rows // KEY_TILE) + u], p[u * KEY_TILE:(u + 1) * KEY_TILE, :],
                              preferred_element_type=jnp.float32)
        acc_ref[...] = alpha * acc_ref[...] + pv
        m_ref[...] = m_new
        return jnp.max(nxt, axis=0, keepdims=True)

    lax.fori_loop(0, top + 1, step, jnp.max(first, axis=0, keepdims=True))
    o_ref[0] = (acc_ref[...] / l_ref[...]).T.astype(o_ref.dtype)


def moba_attention(qk, rest):
    b, s, _ = qk.shape
    n_blocks = s // MOBA_BLOCK
    assert s % MOBA_BLOCK == 0 and n_blocks >= MOBA_TOPK and n_blocks % MOBA_CHUNK_BLOCKS == 0

    def full(first_block):
        return pl.BlockSpec((1, s, HEAD_DIM), lambda bi, h, i: (bi, 0, first_block + N_HEADS_A + h))

    return pl.pallas_call(
        functools.partial(_moba_kernel, n_blocks=n_blocks), grid=(b, N_HEADS_B, n_blocks),
        in_specs=[pl.BlockSpec((1, MOBA_BLOCK, HEAD_DIM), lambda bi, h, i: (bi, i, N_HEADS_A + h)),
                  full(QK_K_BLOCK), full(REST_V_BLOCK)],
        out_specs=pl.BlockSpec((1, MOBA_BLOCK, HEAD_DIM), lambda bi, h, i: (bi, i, h)),
        out_shape=jax.ShapeDtypeStruct((b, s, N_HEADS_B * HEAD_DIM), jnp.bfloat16),
        scratch_shapes=[pltpu.VMEM((n_blocks, HEAD_DIM), jnp.float32),
                        pltpu.VMEM((s // KEY_TILE, HEAD_DIM, KEY_TILE), jnp.bfloat16),
                        pltpu.VMEM((n_blocks, MOBA_BLOCK), jnp.float32),
                        pltpu.VMEM((MOBA_CHUNK_BLOCKS * MOBA_BLOCK, MOBA_BLOCK), jnp.float32),
                        pltpu.VMEM((1, MOBA_BLOCK), jnp.float32),
                        pltpu.VMEM((1, MOBA_BLOCK), jnp.float32), pltpu.VMEM((HEAD_DIM, MOBA_BLOCK), jnp.float32)],
        compiler_params=_params("parallel", "parallel", "arbitrary"), name="moba_attention",
    )(qk, qk, rest).reshape(b * s, N_HEADS_B * HEAD_DIM)


STICK_TILE = 256
STICK_CHUNK = 256
STICK_CUTOFF = -104.0


def _stick_kernel(q_ref, k_ref, v_ref, o_ref, vt_ref, run_ref, acc_ref):
    qi = pl.program_id(2)

    @pl.when(qi == 0)
    def _():
        _transpose_values(v_ref, vt_ref)

    q = q_ref[0]
    rows = STICK_CHUNK
    chunks_per_q = STICK_TILE // rows
    tiles = rows // KEY_TILE
    sq = lax.broadcasted_iota(jnp.int32, (rows, rows), 0)
    sk = lax.broadcasted_iota(jnp.int32, (rows, rows), 1)
    later = jnp.where(sk > sq, 1.0, 0.0).astype(jnp.bfloat16)
    sub = lax.broadcasted_iota(jnp.int32, (rows, STICK_TILE), 0)
    lane = lax.broadcasted_iota(jnp.int32, (rows, STICK_TILE), 1)

    def tile(kt, valid):
        st = pl.multiple_of(kt * rows, rows)
        z = lax.dot_general(k_ref[0, pl.ds(st, rows), :], q, _NT,
                            preferred_element_type=jnp.float32) * SCALE
        log_1m = -(jnp.maximum(z, 0.0) + jnp.log(1.0 + jnp.exp(-jnp.abs(z))))
        if valid is not None:
            log_1m = jnp.where(valid, log_1m, 0.0)
        hi = log_1m.astype(jnp.bfloat16)
        lo = (log_1m - hi.astype(jnp.float32)).astype(jnp.bfloat16)
        inside = (jnp.dot(later, hi, preferred_element_type=jnp.float32)
                  + jnp.dot(later, lo, preferred_element_type=jnp.float32))
        a = jnp.exp(z + log_1m + inside + run_ref[...])
        if valid is not None:
            a = jnp.where(valid, a, 0.0)
        a = a.astype(jnp.bfloat16)
        pv = acc_ref[...]
        for u in range(tiles):
            pv = pv + jnp.dot(vt_ref[kt * tiles + u], a[u * KEY_TILE:(u + 1) * KEY_TILE, :],
                              preferred_element_type=jnp.float32)
        acc_ref[...] = pv
        run_ref[...] += inside[0:1, :] + log_1m[0:1, :]

    def alive():
        return (jnp.max(run_ref[...]) > STICK_CUTOFF).astype(jnp.int32)

    run_ref[...] = jnp.zeros_like(run_ref)
    acc_ref[...] = jnp.zeros_like(acc_ref)
    for u in reversed(range(chunks_per_q)):
        tile(qi * chunks_per_q + u, u * rows + sub < lane)

    def earlier(carry):
        kt, _ = carry
        tile(kt, None)
        return kt - 1, alive()

    lax.while_loop(lambda c: (c[0] >= 0) & (c[1] > 0), earlier, (qi * chunks_per_q - 1, alive()))
    o_ref[0] = acc_ref[...].T.astype(o_ref.dtype)


def stick_attention(rest):
    b, s, _ = rest.shape
    t = STICK_TILE

    def full(first_block):
        return pl.BlockSpec((1, s, HEAD_DIM), lambda bi, h, i: (bi, 0, first_block + h))

    return pl.pallas_call(
        _stick_kernel, grid=(b, N_HEADS_C, s // t),
        in_specs=[pl.BlockSpec((1, t, HEAD_DIM), lambda bi, h, i: (bi, i, h)), full(REST_KC_BLOCK),
                  full(REST_V_BLOCK + NORMED_HEADS)],
        out_specs=pl.BlockSpec((1, t, HEAD_DIM), lambda bi, h, i: (bi, i, h)),
        out_shape=jax.ShapeDtypeStruct((b, s, N_HEADS_C * HEAD_DIM), jnp.bfloat16),
        scratch_shapes=[pltpu.VMEM((s // KEY_TILE, HEAD_DIM, KEY_TILE), jnp.bfloat16),
                        pltpu.VMEM((1, t), jnp.float32), pltpu.VMEM((HEAD_DIM, t), jnp.float32)],
        compiler_params=_params("parallel", "parallel", "arbitrary"), name="stick_attention",
    )(rest, rest, rest).reshape(b * s, N_HEADS_C * HEAD_DIM)


def _merge_kernel(oa_ref, ob_ref, oc_ref, g_ref, wa_ref, wb_ref, wc_ref, out_ref):
    d = out_ref.shape[1]
    ya = jnp.dot(oa_ref[...], wa_ref[...], preferred_element_type=jnp.float32)
    yb = jnp.dot(ob_ref[...], wb_ref[...], preferred_element_type=jnp.float32)
    yc = jnp.dot(oc_ref[...], wc_ref[...], preferred_element_type=jnp.float32)
    merged = (g_ref[:, 0:d].astype(jnp.float32) * ya + g_ref[:, d:2 * d].astype(jnp.float32) * yb
              + g_ref[:, 2 * d:3 * d].astype(jnp.float32) * yc)
    out_ref[...] = merged.astype(out_ref.dtype)


def merge_branches(o_a, o_b, o_c, gates, w_a, w_b, w_c):
    m = o_b.shape[0]
    d = w_a.shape[1]
    tm = 512

    def rows(a):
        return pl.BlockSpec((tm, a.shape[1]), lambda i: (i, 0))

    def whole(w):
        return pl.BlockSpec(w.shape, lambda i: (0, 0))

    return pl.pallas_call(
        _merge_kernel, grid=(m // tm,),
        in_specs=[rows(o_a), rows(o_b), rows(o_c), rows(gates), whole(w_a), whole(w_b), whole(w_c)],
        out_specs=pl.BlockSpec((tm, d), lambda i: (i, 0)), out_shape=jax.ShapeDtypeStruct((m, d), jnp.bfloat16),
        compiler_params=_params("parallel"), name="merge_branches",
    )(o_a, o_b, o_c, gates, w_a, w_b, w_c)


def _out_proj_kernel(a_ref, w_ref, x_ref, g_ref, xo_ref, ho_ref):
    xn = x_ref[...] + jnp.dot(a_ref[...], w_ref[...], preferred_element_type=jnp.float32)
    xo_ref[...] = xn
    ho_ref[...] = _rms(xn, g_ref[...]).astype(ho_ref.dtype)


def out_project(a, w, x, next_gain):
    m, k = a.shape
    d = w.shape[1]
    tm = 512
    return pl.pallas_call(
        _out_proj_kernel, grid=(m // tm,),
        in_specs=[pl.BlockSpec((tm, k), lambda i: (i, 0)), pl.BlockSpec((k, d), lambda i: (0, 0)),
                  pl.BlockSpec((tm, d), lambda i: (i, 0)), pl.BlockSpec((1, d), lambda i: (0, 0))],
        out_specs=[pl.BlockSpec((tm, d), lambda i: (i, 0)), pl.BlockSpec((tm, d), lambda i: (i, 0))],
        out_shape=[jax.ShapeDtypeStruct((m, d), jnp.float32), jax.ShapeDtypeStruct((m, d), jnp.bfloat16)],
        compiler_params=_params("parallel"), name="out_project",
    )(a, w, x, next_gain.reshape(1, d))


def _mem_kv_kernel(mem_ref, ln_ref, w_ref, gk_ref, kv_ref):
    hm = _rms(mem_ref[...], ln_ref[...]).astype(jnp.bfloat16)
    kv = jnp.dot(hm, w_ref[...], preferred_element_type=jnp.float32)
    half = kv.shape[1] // 2
    for hd in range(N_HEADS_MEM):
        sl = slice(hd * HEAD_DIM, (hd + 1) * HEAD_DIM)
        kv_ref[:, sl] = _rms(kv[:, sl], gk_ref[...]).astype(kv_ref.dtype)
    kv_ref[:, half:] = kv[:, half:].astype(kv_ref.dtype)


def mem_kv(mem2d, ln, wm_kv, gain_k):
    n, d = mem2d.shape
    w = wm_kv.shape[1]
    return pl.pallas_call(
        _mem_kv_kernel, grid=(1,),
        in_specs=[pl.BlockSpec((n, d), lambda i: (0, 0)), pl.BlockSpec((1, d), lambda i: (0, 0)),
                  pl.BlockSpec((d, w), lambda i: (0, 0)), pl.BlockSpec((1, HEAD_DIM), lambda i: (0, 0))],
        out_specs=pl.BlockSpec((n, w), lambda i: (0, 0)),
        out_shape=jax.ShapeDtypeStruct((n, w), jnp.bfloat16),
        compiler_params=_params("arbitrary"), name="mem_kv",
    )(mem2d, ln.reshape(1, d), wm_kv, gain_k.reshape(1, HEAD_DIM))


def _mem_attn_kernel(h_ref, wq_ref, gq_ref, kv_ref, wo_ref, x_ref, g_ref, xo_ref, ho_ref):
    qf = jnp.dot(h_ref[...], wq_ref[...], preferred_element_type=jnp.float32)
    half = kv_ref.shape[2] // 2
    outs = []
    for hd in range(N_HEADS_MEM):
        sl = slice(hd * HEAD_DIM, (hd + 1) * HEAD_DIM)
        qh = _rms(qf[:, sl], gq_ref[...]).astype(jnp.bfloat16)
        s = lax.dot_general(qh, kv_ref[0, :, sl], _NT, preferred_element_type=jnp.float32) * SCALE
        e = jnp.exp(s - jnp.max(s, axis=1, keepdims=True))
        vh = kv_ref[0, :, half + hd * HEAD_DIM:half + (hd + 1) * HEAD_DIM]
        o = jnp.dot(e.astype(jnp.bfloat16), vh, preferred_element_type=jnp.float32)
        outs.append((o / jnp.sum(e, axis=1, keepdims=True)).astype(jnp.bfloat16))
    o_all = jnp.concatenate(outs, axis=1)
    xn = x_ref[...] + jnp.dot(o_all, wo_ref[...], preferred_element_type=jnp.float32)
    xo_ref[...] = xn
    ho_ref[...] = _rms(xn, g_ref[...]).astype(ho_ref.dtype)


def mem_attention(h, wm_q, gain_q, kv, wm_o, x, next_gain, seq):
    m, d = h.shape
    wq = wm_q.shape[1]
    tm = 512
    per_batch = seq // tm
    return pl.pallas_call(
        _mem_attn_kernel, grid=(m // tm,),
        in_specs=[pl.BlockSpec((tm, d), lambda i: (i, 0)), pl.BlockSpec((d, wq), lambda i: (0, 0)),
                  pl.BlockSpec((1, HEAD_DIM), lambda i: (0, 0)),
                  pl.BlockSpec((1,) + kv.shape[1:], lambda i: (i // per_batch, 0, 0)),
                  pl.BlockSpec((wq, d), lambda i: (0, 0)), pl.BlockSpec((tm, d), lambda i: (i, 0)),
                  pl.BlockSpec((1, d), lambda i: (0, 0))],
        out_specs=[pl.BlockSpec((tm, d), lambda i: (i, 0)), pl.BlockSpec((tm, d), lambda i: (i, 0))],
        out_shape=[jax.ShapeDtypeStruct((m, d), jnp.float32), jax.ShapeDtypeStruct((m, d), jnp.bfloat16)],
        compiler_params=_params("parallel"), name="mem_attention",
    )(h, wm_q, gain_q.reshape(1, HEAD_DIM), kv, wm_o, x, next_gain.reshape(1, d))


def _shift_rows(u, prev, k):
    rolled = pltpu.roll(u, k, axis=0)
    row = lax.broadcasted_iota(jnp.int32, prev.shape, 0)
    head = jnp.where(row < k, pltpu.roll(prev, k, axis=0), rolled[:HALO])
    return jnp.concatenate([head, rolled[HALO:]], axis=0)


def _ffn_kernel(h_ref, wg_ref, wv_ref, cwg_ref, cwv_ref, cbg_ref, cbv_ref, wd_ref, x_ref, o_ref,
                halo_ref, *, tiles_per_seq):
    i, f = pl.program_id(0), pl.program_id(1)
    tm = h_ref.shape[0]
    keep = jnp.where(i % tiles_per_seq == 0, 0.0, 1.0)
    prevs = [jnp.where(keep > 0.0, halo_ref[f, part], 0.0) for part in range(2)]

    @pl.when(f == 0)
    def _():
        o_ref[...] = x_ref[...]

    for r in range(tm // FF_ROWS):
        rows = slice(r * FF_ROWS, (r + 1) * FF_ROWS)
        ys = []
        for part, (w_ref, cw_ref, cb_ref) in enumerate(((wg_ref, cwg_ref, cbg_ref), (wv_ref, cwv_ref, cbv_ref))):
            u = jnp.dot(h_ref[rows, :], w_ref[...], preferred_element_type=jnp.float32)
            ys.append(cw_ref[0:1, :] * _shift_rows(u, prevs[part], 2) + cw_ref[1:2, :] * _shift_rows(u, prevs[part], 1)
                      + cw_ref[2:3, :] * u + cb_ref[...])
            prevs[part] = u[FF_ROWS - HALO:, :]
        act = (ys[0] * _sigmoid(ys[0]) * ys[1]).astype(jnp.bfloat16)
        o_ref[rows, :] += jnp.dot(act, wd_ref[...], preferred_element_type=jnp.float32)
    for part in range(2):
        halo_ref[f, part] = prevs[part]


def conv_ffn(h, w_g, w_v, cw_g, cw_v, cb_g, cb_v, w_down, x, seq):
    m, d = h.shape
    fp = w_g.shape[1]
    tm, tf = 1024, FF_TILE
    nf = fp // tf
    assert CONV_WIDTH - 1 <= HALO and seq % tm == 0

    def cols(rows):
        return pl.BlockSpec((rows, tf), lambda i, f: (0, f))

    return pl.pallas_call(
        functools.partial(_ffn_kernel, tiles_per_seq=seq // tm), grid=(m // tm, nf),
        in_specs=[pl.BlockSpec((tm, d), lambda i, f: (i, 0)), cols(d), cols(d), cols(CONV_WIDTH), cols(CONV_WIDTH),
                  cols(1), cols(1), pl.BlockSpec((tf, d), lambda i, f: (f, 0)),
                  pl.BlockSpec((tm, d), lambda i, f: (i, 0), pipeline_mode=pl.Buffered(1))],
        out_specs=pl.BlockSpec((tm, d), lambda i, f: (i, 0)),
        out_shape=jax.ShapeDtypeStruct((m, d), jnp.float32),
        scratch_shapes=[pltpu.VMEM((nf, 2, HALO, tf), jnp.float32)],
        compiler_params=_params("arbitrary", "arbitrary"), name="conv_ffn",
    )(h, w_g, w_v, cw_g, cw_v, cb_g, cb_v, w_down, x)


def _pad_cols(a, width):
    return jnp.pad(a, ((0, 0), (0, width - a.shape[1])))


def kernel(x, mem, positions, ln_mix, w_qkv, qk_gain, w_br_a, w_br_b, w_br_c, w_gate, b_gate, w_o,
           ln_mem_q, ln_mem_kv, wm_q, wm_kv, wm_o, mem_qk_gain, ln_ffn, w_up, conv_w, conv_b, w_down):
    b, s, d = x.shape
    depth = ln_mix.shape[0]
    bf = jnp.bfloat16
    d_ff = w_down.shape[1]
    fp = -(-d_ff // FF_TILE) * FF_TILE

    tables = rope_tables(positions)
    xf = x.reshape(b * s, d)
    mem2d = mem.reshape(b * mem.shape[1], d)
    h = rmsnorm_bf16(xf, ln_mix[0])
    for l in range(depth):
        gain_cols = jnp.concatenate(
            [jnp.tile(qk_gain[l, 0], N_HEADS_A), jnp.tile(qk_gain[l, 2], N_HEADS_B),
             jnp.tile(qk_gain[l, 1], N_HEADS_A), jnp.tile(qk_gain[l, 3], N_HEADS_B)]).reshape(1, -1)
        qk, rest = qkv_project(h, w_qkv[l].astype(bf), gain_cols, tables)
        qk, rest = qk.reshape(b, s, -1), rest.reshape(b, s, -1)
        gates = gate_project(h, w_gate[l].astype(bf), b_gate[l])
        o_a = dilated_attention(qk, rest)
        o_b = moba_attention(qk, rest)
        o_c = stick_attention(rest)
        merged = merge_branches(o_a, o_b, o_c, gates,
                                w_br_a[l].astype(bf), w_br_b[l].astype(bf), w_br_c[l].astype(bf))
        xf, h = out_project(merged, w_o[l].astype(bf), xf, ln_mem_q[l])

        kv = mem_kv(mem2d, ln_mem_kv[l], wm_kv[l].astype(bf), mem_qk_gain[l, 1])
        kv = kv.reshape(b, mem.shape[1], kv.shape[1])
        xf, h = mem_attention(h, wm_q[l].astype(bf), mem_qk_gain[l, 0], kv, wm_o[l].astype(bf), xf,
                              ln_ffn[l], s)

        w_g = _pad_cols(w_up[l, :, :d_ff], fp).astype(bf)
        w_v = _pad_cols(w_up[l, :, d_ff:], fp).astype(bf)
        cw_g, cw_v = _pad_cols(conv_w[l, :, :d_ff], fp), _pad_cols(conv_w[l, :, d_ff:], fp)
        cb_g = _pad_cols(conv_b[l, :d_ff].reshape(1, d_ff), fp)
        cb_v = _pad_cols(conv_b[l, d_ff:].reshape(1, d_ff), fp)
        w_d = jnp.pad(w_down[l], ((0, fp - d_ff), (0, 0))).astype(bf)
        xf = conv_ffn(h, w_g, w_v, cw_g, cw_v, cb_g, cb_v, w_d, xf, s)
        if l + 1 < depth:
            h = rmsnorm_bf16(xf, ln_mix[l + 1])
    return xf.reshape(b, s, d)
```

```python
import functools

import jax
import jax.numpy as jnp
from jax import lax
from jax.experimental import pallas as pl
from jax.experimental.pallas import tpu as pltpu

HEAD_DIM = 128
LANES = 128
DILATIONS = (1, 4, 16)
WINDOW_STEPS = 128
HEADS_PER_GROUP = 2
N_HEADS_A = 6
N_HEADS_B = 6
N_HEADS_C = 4
N_HEADS_MIX = 16
MIX_WIDTH = N_HEADS_MIX * HEAD_DIM
KEY_TILE = 128
MOBA_BLOCK = 256
MOBA_TOPK = 3
MOBA_CHUNK_BLOCKS = 2
N_HEADS_MEM = 4
ROPE_THETA = 500000.0
ROT_DIM = HEAD_DIM // 4
ROT_HALF = ROT_DIM // 2
CONV_WIDTH = 3
EPS = 1e-6
SCALE = HEAD_DIM ** -0.5
FF_TILE = 512
FF_ROWS = 256
HALO = 8
VMEM_LIMIT = 56 * 1024 * 1024

_NT = (((1,), (1,)), ((), ()))


def _params(*sem):
    return pltpu.CompilerParams(dimension_semantics=sem, vmem_limit_bytes=VMEM_LIMIT)


def _rms(y, gain):
    return y * lax.rsqrt(jnp.mean(y * y, axis=-1, keepdims=True) + EPS) * gain


def _sigmoid(y):
    return 1.0 / (1.0 + jnp.exp(-y))


def _rope_table_kernel(pos_ref, inv_ref, cos_ref, sin_lo_ref, sin_hi_ref):
    ang = pos_ref[...].astype(jnp.float32) * inv_ref[...]
    lane = lax.broadcasted_iota(jnp.int32, ang.shape, 1)
    s = jnp.sin(ang)
    cos_ref[...] = jnp.cos(ang)
    sin_lo_ref[...] = jnp.where(lane < ROT_HALF, -s, 0.0)
    sin_hi_ref[...] = jnp.where((lane >= ROT_HALF) & (lane < ROT_DIM), s, 0.0)


def rope_tables(positions):
    m = positions.size
    tm = 1024
    inv = ROPE_THETA ** (-jnp.arange(0, ROT_DIM, 2, dtype=jnp.float32) / ROT_DIM)
    inv_row = jnp.zeros((1, HEAD_DIM), jnp.float32).at[0, :ROT_DIM].set(jnp.concatenate([inv, inv]))
    tab = jax.ShapeDtypeStruct((m, HEAD_DIM), jnp.float32)
    spec = pl.BlockSpec((tm, HEAD_DIM), lambda i: (i, 0))
    return pl.pallas_call(
        _rope_table_kernel, grid=(m // tm,),
        in_specs=[pl.BlockSpec((tm, 1), lambda i: (i, 0)), pl.BlockSpec((1, HEAD_DIM), lambda i: (0, 0))],
        out_specs=[spec, spec, spec], out_shape=[tab, tab, tab],
        compiler_params=_params("parallel"), name="rope_tables",
    )(positions.reshape(m, 1), inv_row)


def _rmsnorm_kernel(x_ref, g_ref, o_ref):
    o_ref[...] = _rms(x_ref[...], g_ref[...]).astype(o_ref.dtype)


def rmsnorm_bf16(x, gain):
    m, d = x.shape
    tm = 512
    return pl.pallas_call(
        _rmsnorm_kernel, grid=(m // tm,),
        in_specs=[pl.BlockSpec((tm, d), lambda i: (i, 0)), pl.BlockSpec((1, d), lambda i: (0, 0))],
        out_specs=pl.BlockSpec((tm, d), lambda i: (i, 0)),
        out_shape=jax.ShapeDtypeStruct((m, d), jnp.bfloat16),
        compiler_params=_params("parallel"), name="rmsnorm",
    )(x, gain.reshape(1, d))


PROJ_TILE = 512
PROJ_ROWS = 256
NORMED_HEADS = N_HEADS_A + N_HEADS_B
QK_K_BLOCK = NORMED_HEADS
REST_KC_BLOCK = N_HEADS_C
REST_V_BLOCK = 2 * N_HEADS_C


def _qk_norm_kernel(h_ref, w_ref, gain_ref, ones_ref, cos_ref, sin_lo_ref, sin_hi_ref, o_ref):
    for r in range(h_ref.shape[0] // PROJ_ROWS):
        rows = slice(r * PROJ_ROWS, (r + 1) * PROJ_ROWS)
        acc = jnp.dot(h_ref[rows, :], w_ref[...], preferred_element_type=jnp.float32)
        ss = jnp.dot((acc * acc).astype(jnp.bfloat16), ones_ref[...], preferred_element_type=jnp.float32)
        y = acc * lax.rsqrt(ss * (1.0 / HEAD_DIM) + EPS) * gain_ref[...]
        c, s_lo, s_hi = cos_ref[rows, :], sin_lo_ref[rows, :], sin_hi_ref[rows, :]
        for hd in range(PROJ_TILE // HEAD_DIM):
            sl = slice(hd * HEAD_DIM, (hd + 1) * HEAD_DIM)
            yh = y[:, sl]
            yh = (yh * c + pltpu.roll(yh, HEAD_DIM - ROT_HALF, axis=1) * s_lo
                  + pltpu.roll(yh, ROT_HALF, axis=1) * s_hi)
            o_ref[rows, sl] = yh.astype(o_ref.dtype)


def _plain_proj_kernel(h_ref, w_ref, o_ref):
    o_ref[...] = jnp.dot(h_ref[...], w_ref[...], preferred_element_type=jnp.float32).astype(o_ref.dtype)


def qkv_project(h, w_qkv, gain_cols, tables):
    m, d = h.shape
    tm, tn = 1024, PROJ_TILE
    per_part = MIX_WIDTH // tn
    normed_tiles = NORMED_HEADS * HEAD_DIM // tn
    assert normed_tiles * tn == NORMED_HEADS * HEAD_DIM and normed_tiles + 1 == per_part
    n_out = 2 * normed_tiles * tn
    ones = jnp.kron(jnp.eye(tn // HEAD_DIM, dtype=jnp.float32),
                    jnp.ones((HEAD_DIM, HEAD_DIM), jnp.float32)).astype(jnp.bfloat16)
    tab_spec = pl.BlockSpec((tm, HEAD_DIM), lambda i, j: (i, 0))
    h_spec = pl.BlockSpec((tm, d), lambda i, j: (i, 0))
    out_spec = pl.BlockSpec((tm, tn), lambda i, j: (i, j))
    out_sds = jax.ShapeDtypeStruct((m, n_out), jnp.bfloat16)
    qk = pl.pallas_call(
        _qk_norm_kernel, grid=(m // tm, 2 * normed_tiles),
        in_specs=[h_spec, pl.BlockSpec((d, tn), lambda i, j: (0, j + j // normed_tiles)),
                  pl.BlockSpec((1, tn), lambda i, j: (0, j)), pl.BlockSpec((tn, tn), lambda i, j: (0, 0)),
                  tab_spec, tab_spec, tab_spec],
        out_specs=out_spec, out_shape=out_sds,
        compiler_params=_params("parallel", "arbitrary"), name="qk_norm_project",
    )(h, w_qkv, gain_cols, ones, *tables)
    rest = pl.pallas_call(
        _plain_proj_kernel, grid=(m // tm, 2 + per_part),
        in_specs=[h_spec, pl.BlockSpec(
            (d, tn), lambda i, j: (0, jnp.where(j < 2, normed_tiles + j * per_part, j + 2 * per_part - 2)))],
        out_specs=out_spec, out_shape=out_sds,
        compiler_params=_params("parallel", "arbitrary"), name="plain_project",
    )(h, w_qkv)
    return qk, rest


def _gate_kernel(h_ref, w_ref, b_ref, o_ref):
    acc = jnp.dot(h_ref[...], w_ref[...], preferred_element_type=jnp.float32)
    o_ref[...] = _sigmoid(acc + b_ref[...]).astype(o_ref.dtype)


def gate_project(h, w_gate, b_gate):
    m, d = h.shape
    n = w_gate.shape[1]
    tm, tn = 1024, 2048
    return pl.pallas_call(
        _gate_kernel, grid=(m // tm, n // tn),
        in_specs=[pl.BlockSpec((tm, d), lambda i, j: (i, 0)), pl.BlockSpec((d, tn), lambda i, j: (0, j)),
                  pl.BlockSpec((1, tn), lambda i, j: (0, j))],
        out_specs=pl.BlockSpec((tm, tn), lambda i, j: (i, j)),
        out_shape=jax.ShapeDtypeStruct((m, n), jnp.bfloat16),
        compiler_params=_params("parallel", "arbitrary"), name="gate_project",
    )(h, w_gate, b_gate.reshape(1, n))


def _transpose_values(v_ref, vt_ref):
    def one(c, carry):
        st = pl.multiple_of(c * KEY_TILE, KEY_TILE)
        vt_ref[c] = v_ref[0, pl.ds(st, KEY_TILE), :].astype(jnp.float32).T.astype(vt_ref.dtype)
        return carry
    lax.fori_loop(0, vt_ref.shape[0], one, 0)


def _softmax_tile(st, vt_tile, m_ref, l_ref, acc_ref):
    m_old = m_ref[...]
    m_new = jnp.maximum(m_old, jnp.max(st, axis=0, keepdims=True))
    alpha = jnp.exp(m_old - m_new)
    p = jnp.exp(st - m_new)
    l_ref[...] = alpha * l_ref[...] + jnp.sum(p, axis=0, keepdims=True)
    acc_ref[...] = alpha * acc_ref[...] + jnp.dot(vt_tile, p.astype(vt_tile.dtype),
                                                  preferred_element_type=jnp.float32)
    m_ref[...] = m_new


def _softmax_init(m_ref, l_ref, acc_ref):
    m_ref[...] = jnp.full_like(m_ref, -jnp.inf)
    l_ref[...] = jnp.zeros_like(l_ref)
    acc_ref[...] = jnp.zeros_like(acc_ref)


DILATED_ROWS = tuple((d + 1) * KEY_TILE for d in DILATIONS)
DILATED_OFFSETS = tuple(sum(DILATED_ROWS[:g]) for g in range(len(DILATIONS)))


def _dilated_kernel(q0_ref, q1_ref, q2_ref, k0_ref, k1_ref, k2_ref, v0_ref, v1_ref, v2_ref, o_ref,
                    vt_ref, bias_ref, ahead_ref, sc_ref):
    q_refs, k_refs, v_refs = (q0_ref, q1_ref, q2_ref), (k0_ref, k1_ref, k2_ref), (v0_ref, v1_ref, v2_ref)
    neg = jnp.float32(-jnp.inf)

    ahead_rows = max(DILATED_ROWS)
    ahead = (lax.broadcasted_iota(jnp.int32, (ahead_rows, KEY_TILE), 0)
             - lax.broadcasted_iota(jnp.int32, (ahead_rows, KEY_TILE), 1))
    ahead_ref[...] = ahead
    for g, d in enumerate(DILATIONS):
        _transpose_values(v_refs[g], vt_ref.at[g])
        sl = slice(DILATED_OFFSETS[g], DILATED_OFFSETS[g] + DILATED_ROWS[g])
        on_grid = jnp.where((ahead[:DILATED_ROWS[g]] & (d - 1)) == 0, 0.0, neg)
        bias_ref[0, sl, :] = on_grid
        bias_ref[1, sl, :] = jnp.where(ahead[:DILATED_ROWS[g]] >= 0, on_grid, neg)

    def query_tile(qi, carry):
        qs = pl.multiple_of(qi * KEY_TILE, KEY_TILE)
        m = jnp.full((1, KEY_TILE), neg, jnp.float32)
        starts = []
        for g, d in enumerate(DILATIONS):
            start = jnp.maximum(qi - d, 0)
            starts.append(start)
            q = q_refs[g][0, pl.ds(qs, KEY_TILE), :]
            which = jnp.where(qi >= d, 1, 0)
            newest = (qi - start) * KEY_TILE
            for a in range(d + 1):
                st = pl.multiple_of((start + a) * KEY_TILE, KEY_TILE)
                r0 = DILATED_OFFSETS[g] + a * KEY_TILE
                scores = lax.dot_general(k_refs[g][0, pl.ds(st, KEY_TILE), :], q, _NT,
                                         preferred_element_type=jnp.float32) * SCALE
                scores = scores + bias_ref[which, r0:r0 + KEY_TILE, :]
                scores = jnp.where(ahead_ref[a * KEY_TILE:(a + 1) * KEY_TILE, :] <= newest, scores, neg)
                sc_ref[r0:r0 + KEY_TILE, :] = scores
                m = jnp.maximum(m, jnp.max(scores, axis=0, keepdims=True))

        l = jnp.zeros((1, KEY_TILE), jnp.float32)
        acc = jnp.zeros((HEAD_DIM, KEY_TILE), jnp.float32)
        for g, d in enumerate(DILATIONS):
            for a in range(d + 1):
                r0 = DILATED_OFFSETS[g] + a * KEY_TILE
                p = jnp.exp(sc_ref[r0:r0 + KEY_TILE, :] - m)
                l = l + jnp.sum(p, axis=0, keepdims=True)
                acc = acc + jnp.dot(vt_ref[g, starts[g] + a], p.astype(vt_ref.dtype),
                                    preferred_element_type=jnp.float32)
        o_ref[0, pl.ds(qs, KEY_TILE), :] = (acc / l).T.astype(o_ref.dtype)
        return carry

    lax.fori_loop(0, o_ref.shape[1] // KEY_TILE, query_tile, 0)


def dilated_attention(qk, rest):
    b, s, _ = qk.shape
    n_tiles = s // KEY_TILE
    assert WINDOW_STEPS == KEY_TILE and all(d & (d - 1) == 0 for d in DILATIONS)
    assert s >= max(DILATED_ROWS)

    def full(first_block, g):
        return pl.BlockSpec((1, s, HEAD_DIM), lambda bi, j: (bi, 0, first_block + g * HEADS_PER_GROUP + j))

    groups = range(len(DILATIONS))
    return pl.pallas_call(
        _dilated_kernel, grid=(b, HEADS_PER_GROUP),
        in_specs=([full(0, g) for g in groups] + [full(QK_K_BLOCK, g) for g in groups]
                  + [full(REST_V_BLOCK, g) for g in groups]),
        out_specs=pl.BlockSpec((1, s, HEAD_DIM), lambda bi, j: (bi, 0, j)),
        out_shape=jax.ShapeDtypeStruct((b, s, HEADS_PER_GROUP * HEAD_DIM), jnp.bfloat16),
        scratch_shapes=[pltpu.VMEM((len(DILATIONS), n_tiles, HEAD_DIM, KEY_TILE), jnp.bfloat16),
                        pltpu.VMEM((2, sum(DILATED_ROWS), KEY_TILE), jnp.float32),
                        pltpu.VMEM((max(DILATED_ROWS), KEY_TILE), jnp.int32),
                        pltpu.VMEM((sum(DILATED_ROWS), KEY_TILE), jnp.float32)],
        compiler_params=_params("parallel", "parallel"), name="dilated_attention",
    )(*([qk] * 6 + [rest] * 3)).reshape(b * s, HEADS_PER_GROUP * HEAD_DIM)


def _moba_kernel(q_ref, k_ref, v_ref, o_ref, kmean_ref, vt_ref, bias_ref, sc_ref, m_ref, l_ref, acc_ref, *,
                 n_blocks):
    for n in range(n_blocks):
        kb = k_ref[0, n * MOBA_BLOCK:(n + 1) * MOBA_BLOCK, :].astype(jnp.float32)
        kmean_ref[n:n + 1, :] = jnp.sum(kb, axis=0, keepdims=True) / MOBA_BLOCK
    _transpose_values(v_ref, vt_ref)
    neg = jnp.float32(-jnp.inf)
    rows = MOBA_CHUNK_BLOCKS * MOBA_BLOCK

    def query_block(own, carry):
        qs = pl.multiple_of(own * MOBA_BLOCK, MOBA_BLOCK)
        q = q_ref[0, pl.ds(qs, MOBA_BLOCK), :]
        gate = lax.dot_general(kmean_ref[...], q.astype(jnp.float32), _NT,
                               precision=lax.Precision.HIGHEST, preferred_element_type=jnp.float32)
        blk = lax.broadcasted_iota(jnp.int32, gate.shape, 0)
        gate = jnp.where(blk < own, gate, neg)
        rank = jnp.zeros(gate.shape, jnp.int32)
        for mth in range(n_blocks):
            gm = gate[mth:mth + 1, :]
            lower = jnp.where(mth < blk, 1, 0)
            rank = rank + jnp.where(gm > gate, 1, 0) + jnp.where(gm == gate, lower, 0)
        rank = jnp.where(blk < own, rank, MOBA_TOPK)
        bias_ref[...] = jnp.where((rank < MOBA_TOPK) | (blk == own), 0.0, neg)

        top = own // MOBA_CHUNK_BLOCKS
        _softmax_init(m_ref, l_ref, acc_ref)

        def chunk_scores(c, causal):
            st = pl.multiple_of(c * rows, rows)
            s = lax.dot_general(k_ref[0, pl.ds(st, rows), :], q, _NT, preferred_element_type=jnp.float32) * SCALE
            s = jnp.concatenate([s[u * MOBA_BLOCK:(u + 1) * MOBA_BLOCK]
                                 + bias_ref[pl.ds(c * MOBA_CHUNK_BLOCKS + u, 1), :]
                                 for u in range(MOBA_CHUNK_BLOCKS)], axis=0)
            if causal:
                ahead = lax.broadcasted_iota(jnp.int32, s.shape, 0) - lax.broadcasted_iota(jnp.int32, s.shape, 1)
                s = jnp.where(ahead <= (own - c * MOBA_CHUNK_BLOCKS) * MOBA_BLOCK, s, neg)
            return s

        first = chunk_scores(top, True)
        sc_ref[...] = first

        def step(i, col_max):
            cur = top - i
            m_old = m_ref[...]
            m_new = jnp.maximum(m_old, col_max)
            alpha = jnp.exp(m_old - m_new)
            p = jnp.exp(sc_ref[...] - m_new)
            nxt = chunk_scores(jnp.maximum(cur - 1, 0), False)
            sc_ref[...] = nxt
            l_ref[...] = alpha * l_ref[...] + jnp.sum(p, axis=0, keepdims=True)
            p = p.astype(vt_ref.dtype)
            pv = jnp.zeros(acc_ref.shape, jnp.float32)
            for u in range(rows // KEY_TILE):
                pv = pv + jnp.dot(vt_ref[cur * (rows // KEY_TILE) + u], p[u * KEY_TILE:(u + 1) * KEY_TILE, :],
                                  preferred_element_type=jnp.float32)
            acc_ref[...] = alpha * acc_ref[...] + pv
            m_ref[...] = m_new
            return jnp.max(nxt, axis=0, keepdims=True)

        lax.fori_loop(0, top + 1, step, jnp.max(first, axis=0, keepdims=True))
        o_ref[0, pl.ds(qs, MOBA_BLOCK), :] = (acc_ref[...] / l_ref[...]).T.astype(o_ref.dtype)
        return carry

    lax.fori_loop(0, n_blocks, query_block, 0)


def moba_attention(qk, rest):
    b, s, _ = qk.shape
    n_blocks = s // MOBA_BLOCK
    assert s % MOBA_BLOCK == 0 and n_blocks >= MOBA_TOPK and n_blocks % MOBA_CHUNK_BLOCKS == 0

    def full(first_block):
        return pl.BlockSpec((1, s, HEAD_DIM), lambda bi, h: (bi, 0, first_block + N_HEADS_A + h))

    return pl.pallas_call(
        functools.partial(_moba_kernel, n_blocks=n_blocks), grid=(b, N_HEADS_B),
        in_specs=[full(0), full(QK_K_BLOCK), full(REST_V_BLOCK)],
        out_specs=pl.BlockSpec((1, s, HEAD_DIM), lambda bi, h: (bi, 0, h)),
        out_shape=jax.ShapeDtypeStruct((b, s, N_HEADS_B * HEAD_DIM), jnp.bfloat16),
        scratch_shapes=[pltpu.VMEM((n_blocks, HEAD_DIM), jnp.float32),
                        pltpu.VMEM((s // KEY_TILE, HEAD_DIM, KEY_TILE), jnp.bfloat16),
                        pltpu.VMEM((n_blocks, MOBA_BLOCK), jnp.float32),
                        pltpu.VMEM((MOBA_CHUNK_BLOCKS * MOBA_BLOCK, MOBA_BLOCK), jnp.float32),
                        pltpu.VMEM((1, MOBA_BLOCK), jnp.float32),
                        pltpu.VMEM((1, MOBA_BLOCK), jnp.float32), pltpu.VMEM((HEAD_DIM, MOBA_BLOCK), jnp.float32)],
        compiler_params=_params("parallel", "parallel"), name="moba_attention",
    )(qk, qk, rest).reshape(b * s, N_HEADS_B * HEAD_DIM)


STICK_TILE = 256
STICK_CHUNK = 256
STICK_CUTOFF = -104.0


def _stick_kernel(q_ref, k_ref, v_ref, o_ref, vt_ref, run_ref, acc_ref):
    _transpose_values(v_ref, vt_ref)
    rows = STICK_CHUNK
    chunks_per_q = STICK_TILE // rows
    tiles = rows // KEY_TILE
    sq = lax.broadcasted_iota(jnp.int32, (rows, rows), 0)
    sk = lax.broadcasted_iota(jnp.int32, (rows, rows), 1)
    later = jnp.where(sk > sq, 1.0, 0.0).astype(jnp.bfloat16)
    sub = lax.broadcasted_iota(jnp.int32, (rows, STICK_TILE), 0)
    lane = lax.broadcasted_iota(jnp.int32, (rows, STICK_TILE), 1)

    def tile(q, kt, valid):
        st = pl.multiple_of(kt * rows, rows)
        z = lax.dot_general(k_ref[0, pl.ds(st, rows), :], q, _NT,
                            preferred_element_type=jnp.float32) * SCALE
        log_1m = -(jnp.maximum(z, 0.0) + jnp.log(1.0 + jnp.exp(-jnp.abs(z))))
        if valid is not None:
            log_1m = jnp.where(valid, log_1m, 0.0)
        hi = log_1m.astype(jnp.bfloat16)
        lo = (log_1m - hi.astype(jnp.float32)).astype(jnp.bfloat16)
        inside = (jnp.dot(later, hi, preferred_element_type=jnp.float32)
                  + jnp.dot(later, lo, preferred_element_type=jnp.float32))
        a = jnp.exp(z + log_1m + inside + run_ref[...])
        if valid is not None:
            a = jnp.where(valid, a, 0.0)
        a = a.astype(jnp.bfloat16)
        pv = acc_ref[...]
        for u in range(tiles):
            pv = pv + jnp.dot(vt_ref[kt * tiles + u], a[u * KEY_TILE:(u + 1) * KEY_TILE, :],
                              preferred_element_type=jnp.float32)
        acc_ref[...] = pv
        run_ref[...] += inside[0:1, :] + log_1m[0:1, :]

    def alive():
        return (jnp.max(run_ref[...]) > STICK_CUTOFF).astype(jnp.int32)

    def query_tile(qi, carry):
        qs = pl.multiple_of(qi * STICK_TILE, STICK_TILE)
        q = q_ref[0, pl.ds(qs, STICK_TILE), :]
        run_ref[...] = jnp.zeros_like(run_ref)
        acc_ref[...] = jnp.zeros_like(acc_ref)
        for u in reversed(range(chunks_per_q)):
            tile(q, qi * chunks_per_q + u, u * rows + sub < lane)

        def earlier(c):
            tile(q, c[0], None)
            return c[0] - 1, alive()

        lax.while_loop(lambda c: (c[0] >= 0) & (c[1] > 0), earlier, (qi * chunks_per_q - 1, alive()))
        o_ref[0, pl.ds(qs, STICK_TILE), :] = acc_ref[...].T.astype(o_ref.dtype)
        return carry

    lax.fori_loop(0, o_ref.shape[1] // STICK_TILE, query_tile, 0)


def stick_attention(rest):
    b, s, _ = rest.shape
    t = STICK_TILE

    def full(first_block):
        return pl.BlockSpec((1, s, HEAD_DIM), lambda bi, h: (bi, 0, first_block + h))

    return pl.pallas_call(
        _stick_kernel, grid=(b, N_HEADS_C),
        in_specs=[full(0), full(REST_KC_BLOCK), full(REST_V_BLOCK + NORMED_HEADS)],
        out_specs=pl.BlockSpec((1, s, HEAD_DIM), lambda bi, h: (bi, 0, h)),
        out_shape=jax.ShapeDtypeStruct((b, s, N_HEADS_C * HEAD_DIM), jnp.bfloat16),
        scratch_shapes=[pltpu.VMEM((s // KEY_TILE, HEAD_DIM, KEY_TILE), jnp.bfloat16),
                        pltpu.VMEM((1, t), jnp.float32), pltpu.VMEM((HEAD_DIM, t), jnp.float32)],
        compiler_params=_params("parallel", "parallel"), name="stick_attention",
    )(rest, rest, rest).reshape(b * s, N_HEADS_C * HEAD_DIM)


def _merge_kernel(oa_ref, ob_ref, oc_ref, g_ref, wa_ref, wb_ref, wc_ref, out_ref):
    d = out_ref.shape[1]
    ya = jnp.dot(oa_ref[...], wa_ref[...], preferred_element_type=jnp.float32)
    yb = jnp.dot(ob_ref[...], wb_ref[...], preferred_element_type=jnp.float32)
    yc = jnp.dot(oc_ref[...], wc_ref[...], preferred_element_type=jnp.float32)
    merged = (g_ref[:, 0:d].astype(jnp.float32) * ya + g_ref[:, d:2 * d].astype(jnp.float32) * yb
              + g_ref[:, 2 * d:3 * d].astype(jnp.float32) * yc)
    out_ref[...] = merged.astype(out_ref.dtype)


def merge_branches(o_a, o_b, o_c, gates, w_a, w_b, w_c):
    m = o_b.shape[0]
    d = w_a.shape[1]
    tm = 512

    def rows(a):
        return pl.BlockSpec((tm, a.shape[1]), lambda i: (i, 0))

    def whole(w):
        return pl.BlockSpec(w.shape, lambda i: (0, 0))

    return pl.pallas_call(
        _merge_kernel, grid=(m // tm,),
        in_specs=[rows(o_a), rows(o_b), rows(o_c), rows(gates), whole(w_a), whole(w_b), whole(w_c)],
        out_specs=pl.BlockSpec((tm, d), lambda i: (i, 0)), out_shape=jax.ShapeDtypeStruct((m, d), jnp.bfloat16),
        compiler_params=_params("parallel"), name="merge_branches",
    )(o_a, o_b, o_c, gates, w_a, w_b, w_c)


def _out_proj_kernel(a_ref, w_ref, x_ref, g_ref, xo_ref, ho_ref):
    xn = x_ref[...] + jnp.dot(a_ref[...], w_ref[...], preferred_element_type=jnp.float32)
    xo_ref[...] = xn
    ho_ref[...] = _rms(xn, g_ref[...]).astype(ho_ref.dtype)


def out_project(a, w, x, next_gain):
    m, k = a.shape
    d = w.shape[1]
    tm = 512
    return pl.pallas_call(
        _out_proj_kernel, grid=(m // tm,),
        in_specs=[pl.BlockSpec((tm, k), lambda i: (i, 0)), pl.BlockSpec((k, d), lambda i: (0, 0)),
                  pl.BlockSpec((tm, d), lambda i: (i, 0)), pl.BlockSpec((1, d), lambda i: (0, 0))],
        out_specs=[pl.BlockSpec((tm, d), lambda i: (i, 0)), pl.BlockSpec((tm, d), lambda i: (i, 0))],
        out_shape=[jax.ShapeDtypeStruct((m, d), jnp.float32), jax.ShapeDtypeStruct((m, d), jnp.bfloat16)],
        compiler_params=_params("parallel"), name="out_project",
    )(a, w, x, next_gain.reshape(1, d))


def _mem_kv_kernel(mem_ref, ln_ref, w_ref, gk_ref, kv_ref):
    hm = _rms(mem_ref[...], ln_ref[...]).astype(jnp.bfloat16)
    kv = jnp.dot(hm, w_ref[...], preferred_element_type=jnp.float32)
    half = kv.shape[1] // 2
    for hd in range(N_HEADS_MEM):
        sl = slice(hd * HEAD_DIM, (hd + 1) * HEAD_DIM)
        kv_ref[:, sl] = _rms(kv[:, sl], gk_ref[...]).astype(kv_ref.dtype)
    kv_ref[:, half:] = kv[:, half:].astype(kv_ref.dtype)


def mem_kv(mem2d, ln, wm_kv, gain_k):
    n, d = mem2d.shape
    w = wm_kv.shape[1]
    return pl.pallas_call(
        _mem_kv_kernel, grid=(1,),
        in_specs=[pl.BlockSpec((n, d), lambda i: (0, 0)), pl.BlockSpec((1, d), lambda i: (0, 0)),
                  pl.BlockSpec((d, w), lambda i: (0, 0)), pl.BlockSpec((1, HEAD_DIM), lambda i: (0, 0))],
        out_specs=pl.BlockSpec((n, w), lambda i: (0, 0)),
        out_shape=jax.ShapeDtypeStruct((n, w), jnp.bfloat16),
        compiler_params=_params("arbitrary"), name="mem_kv",
    )(mem2d, ln.reshape(1, d), wm_kv, gain_k.reshape(1, HEAD_DIM))


def _mem_attn_kernel(h_ref, wq_ref, gq_ref, kv_ref, wo_ref, x_ref, g_ref, xo_ref, ho_ref):
    qf = jnp.dot(h_ref[...], wq_ref[...], preferred_element_type=jnp.float32)
    half = kv_ref.shape[2] // 2
    outs = []
    for hd in range(N_HEADS_MEM):
        sl = slice(hd * HEAD_DIM, (hd + 1) * HEAD_DIM)
        qh = _rms(qf[:, sl], gq_ref[...]).astype(jnp.bfloat16)
        s = lax.dot_general(qh, kv_ref[0, :, sl], _NT, preferred_element_type=jnp.float32) * SCALE
        e = jnp.exp(s - jnp.max(s, axis=1, keepdims=True))
        vh = kv_ref[0, :, half + hd * HEAD_DIM:half + (hd + 1) * HEAD_DIM]
        o = jnp.dot(e.astype(jnp.bfloat16), vh, preferred_element_type=jnp.float32)
        outs.append((o / jnp.sum(e, axis=1, keepdims=True)).astype(jnp.bfloat16))
    o_all = jnp.concatenate(outs, axis=1)
    xn = x_ref[...] + jnp.dot(o_all, wo_ref[...], preferred_element_type=jnp.float32)
    xo_ref[...] = xn
    ho_ref[...] = _rms(xn, g_ref[...]).astype(ho_ref.dtype)


def mem_attention(h, wm_q, gain_q, kv, wm_o, x, next_gain, seq):
    m, d = h.shape
    wq = wm_q.shape[1]
    tm = 512
    per_batch = seq // tm
    return pl.pallas_call(
        _mem_attn_kernel, grid=(m // tm,),
        in_specs=[pl.BlockSpec((tm, d), lambda i: (i, 0)), pl.BlockSpec((d, wq), lambda i: (0, 0)),
                  pl.BlockSpec((1, HEAD_DIM), lambda i: (0, 0)),
                  pl.BlockSpec((1,) + kv.shape[1:], lambda i: (i // per_batch, 0, 0)),
                  pl.BlockSpec((wq, d), lambda i: (0, 0)), pl.BlockSpec((tm, d), lambda i: (i, 0)),
                  pl.BlockSpec((1, d), lambda i: (0, 0))],
        out_specs=[pl.BlockSpec((tm, d), lambda i: (i, 0)), pl.BlockSpec((tm, d), lambda i: (i, 0))],
        out_shape=[jax.ShapeDtypeStruct((m, d), jnp.float32), jax.ShapeDtypeStruct((m, d), jnp.bfloat16)],
        compiler_params=_params("parallel"), name="mem_attention",
    )(h, wm_q, gain_q.reshape(1, HEAD_DIM), kv, wm_o, x, next_gain.reshape(1, d))


def _shift_rows(u, prev, k):
    rolled = pltpu.roll(u, k, axis=0)
    row = lax.broadcasted_iota(jnp.int32, prev.shape, 0)
    head = jnp.where(row < k, pltpu.roll(prev, k, axis=0), rolled[:HALO])
    return jnp.concatenate([head, rolled[HALO:]], axis=0)


def _ffn_kernel(h_ref, wg_ref, wv_ref, cwg_ref, cwv_ref, cbg_ref, cbv_ref, wd_ref, x_ref, o_ref,
                halo_ref, *, tiles_per_seq):
    i, f = pl.program_id(0), pl.program_id(1)
    tm = h_ref.shape[0]
    keep = jnp.where(i % tiles_per_seq == 0, 0.0, 1.0)
    prevs = [jnp.where(keep > 0.0, halo_ref[f, part], 0.0) for part in range(2)]

    @pl.when(f == 0)
    def _():
        o_ref[...] = x_ref[...]

    for r in range(tm // FF_ROWS):
        rows = slice(r * FF_ROWS, (r + 1) * FF_ROWS)
        ys = []
        for part, (w_ref, cw_ref, cb_ref) in enumerate(((wg_ref, cwg_ref, cbg_ref), (wv_ref, cwv_ref, cbv_ref))):
            u = jnp.dot(h_ref[rows, :], w_ref[...], preferred_element_type=jnp.float32)
            ys.append(cw_ref[0:1, :] * _shift_rows(u, prevs[part], 2) + cw_ref[1:2, :] * _shift_rows(u, prevs[part], 1)
                      + cw_ref[2:3, :] * u + cb_ref[...])
            prevs[part] = u[FF_ROWS - HALO:, :]
        act = (ys[0] * _sigmoid(ys[0]) * ys[1]).astype(jnp.bfloat16)
        o_ref[rows, :] += jnp.dot(act, wd_ref[...], preferred_element_type=jnp.float32)
    for part in range(2):
        halo_ref[f, part] = prevs[part]


def _cast_up_kernel(*refs, valid_blocks):
    *w_refs, o_ref = refs
    f = pl.program_id(1)
    for j, w_ref in enumerate(w_refs):
        inside = f * len(w_refs) + j < valid_blocks
        o_ref[:, j * LANES:(j + 1) * LANES] = jnp.where(inside, w_ref[...], 0.0).astype(o_ref.dtype)


def cast_up_weights(w_up, layer, fp):
    _, d, two_ff = w_up.shape
    d_ff = two_ff // 2
    assert d_ff % LANES == 0 and fp % FF_TILE == 0
    half_blocks = d_ff // LANES
    per_tile = FF_TILE // LANES

    def in_spec(j):
        return pl.BlockSpec(
            (None, d, LANES),
            lambda p, f: (layer, 0, jnp.minimum(p * half_blocks + f * per_tile + j, 2 * half_blocks - 1)))

    return pl.pallas_call(
        functools.partial(_cast_up_kernel, valid_blocks=half_blocks), grid=(2, fp // FF_TILE),
        in_specs=[in_spec(j) for j in range(per_tile)],
        out_specs=pl.BlockSpec((None, d, FF_TILE), lambda p, f: (p, 0, f)),
        out_shape=jax.ShapeDtypeStruct((2, d, fp), jnp.bfloat16),
        compiler_params=_params("parallel", "parallel"), name="cast_up_weights",
    )(*([w_up] * per_tile))


def _cast_down_kernel(w_ref, o_ref, *, valid_blocks):
    o_ref[...] = jnp.where(pl.program_id(0) < valid_blocks, w_ref[...], 0.0).astype(o_ref.dtype)


def cast_down_weights(w_down, layer, fp):
    _, d_ff, d = w_down.shape
    assert d_ff % LANES == 0
    blocks = d_ff // LANES
    return pl.pallas_call(
        functools.partial(_cast_down_kernel, valid_blocks=blocks), grid=(fp // LANES,),
        in_specs=[pl.BlockSpec((None, LANES, d), lambda r: (layer, jnp.minimum(r, blocks - 1), 0))],
        out_specs=pl.BlockSpec((LANES, d), lambda r: (r, 0)),
        out_shape=jax.ShapeDtypeStruct((fp, d), jnp.bfloat16),
        compiler_params=_params("parallel"), name="cast_down_weights",
    )(w_down)


def conv_ffn(h, w_gv, cw_g, cw_v, cb_g, cb_v, w_down, x, seq):
    m, d = h.shape
    fp = w_gv.shape[2]
    tm, tf = 512, FF_TILE
    nf = fp // tf
    assert CONV_WIDTH - 1 <= HALO and seq % tm == 0 and tm % FF_ROWS == 0

    def cols(rows):
        return pl.BlockSpec((rows, tf), lambda i, f: (0, f))

    def up_half(part):
        return pl.BlockSpec((None, d, tf), lambda i, f: (part, 0, f))

    return pl.pallas_call(
        functools.partial(_ffn_kernel, tiles_per_seq=seq // tm), grid=(m // tm, nf),
        in_specs=[pl.BlockSpec((tm, d), lambda i, f: (i, 0)), up_half(0), up_half(1), cols(CONV_WIDTH),
                  cols(CONV_WIDTH), cols(1), cols(1), pl.BlockSpec((tf, d), lambda i, f: (f, 0)),
                  pl.BlockSpec((tm, d), lambda i, f: (i, 0))],
        out_specs=pl.BlockSpec((tm, d), lambda i, f: (i, 0)),
        out_shape=jax.ShapeDtypeStruct((m, d), jnp.float32),
        scratch_shapes=[pltpu.VMEM((nf, 2, HALO, tf), jnp.float32)],
        compiler_params=_params("arbitrary", "arbitrary"), name="conv_ffn",
    )(h, w_gv, w_gv, cw_g, cw_v, cb_g, cb_v, w_down, x)


def _pad_cols(a, width):
    return jnp.pad(a, ((0, 0), (0, width - a.shape[1])))


def kernel(x, mem, positions, ln_mix, w_qkv, qk_gain, w_br_a, w_br_b, w_br_c, w_gate, b_gate, w_o,
           ln_mem_q, ln_mem_kv, wm_q, wm_kv, wm_o, mem_qk_gain, ln_ffn, w_up, conv_w, conv_b, w_down):
    b, s, d = x.shape
    depth = ln_mix.shape[0]
    bf = jnp.bfloat16
    d_ff = w_down.shape[1]
    fp = -(-d_ff // FF_TILE) * FF_TILE

    tables = rope_tables(positions)
    xf = x.reshape(b * s, d)
    mem2d = mem.reshape(b * mem.shape[1], d)
    h = rmsnorm_bf16(xf, ln_mix[0])
    for l in range(depth):
        gain_cols = jnp.concatenate(
            [jnp.tile(qk_gain[l, 0], N_HEADS_A), jnp.tile(qk_gain[l, 2], N_HEADS_B),
             jnp.tile(qk_gain[l, 1], N_HEADS_A), jnp.tile(qk_gain[l, 3], N_HEADS_B)]).reshape(1, -1)
        qk, rest = qkv_project(h, w_qkv[l].astype(bf), gain_cols, tables)
        qk, rest = qk.reshape(b, s, -1), rest.reshape(b, s, -1)
        gates = gate_project(h, w_gate[l].astype(bf), b_gate[l])
        o_a = dilated_attention(qk, rest)
        o_b = moba_attention(qk, rest)
        o_c = stick_attention(rest)
        merged = merge_branches(o_a, o_b, o_c, gates,
                                w_br_a[l].astype(bf), w_br_b[l].astype(bf), w_br_c[l].astype(bf))
        xf, h = out_project(merged, w_o[l].astype(bf), xf, ln_mem_q[l])

        kv = mem_kv(mem2d, ln_mem_kv[l], wm_kv[l].astype(bf), mem_qk_gain[l, 1])
        kv = kv.reshape(b, mem.shape[1], kv.shape[1])
        xf, h = mem_attention(h, wm_q[l].astype(bf), mem_qk_gain[l, 0], kv, wm_o[l].astype(bf), xf,
                              ln_ffn[l], s)

        cw_g, cw_v = _pad_cols(conv_w[l, :, :d_ff], fp), _pad_cols(conv_w[l, :, d_ff:], fp)
        cb_g = _pad_cols(conv_b[l, :d_ff].reshape(1, d_ff), fp)
        cb_v = _pad_cols(conv_b[l, d_ff:].reshape(1, d_ff), fp)
        xf = conv_ffn(h, cast_up_weights(w_up, l, fp), cw_g, cw_v, cb_g, cb_v,
                      cast_down_weights(w_down, l, fp), xf, s)
        if l + 1 < depth:
            h = rmsnorm_bf16(xf, ln_mix[l + 1])
    return xf.reshape(b, s, d)
```

```python
import functools

import jax
import jax.numpy as jnp
from jax import lax
from jax.experimental import pallas as pl
from jax.experimental.pallas import tpu as pltpu

HEAD_DIM = 128
LANES = 128
DILATIONS = (1, 4, 16)
WINDOW_STEPS = 128
HEADS_PER_GROUP = 2
N_HEADS_A = 6
N_HEADS_B = 6
N_HEADS_C = 4
N_HEADS_MIX = 16
MIX_WIDTH = N_HEADS_MIX * HEAD_DIM
KEY_TILE = 128
MOBA_BLOCK = 256
MOBA_TOPK = 3
MOBA_CHUNK_BLOCKS = 4
N_HEADS_MEM = 4
ROPE_THETA = 500000.0
ROT_DIM = HEAD_DIM // 4
ROT_HALF = ROT_DIM // 2
CONV_WIDTH = 3
EPS = 1e-6
SCALE = HEAD_DIM ** -0.5
FF_TILE = 512
FF_ROWS = 256
HALO = 8
VMEM_LIMIT = 56 * 1024 * 1024

_NT = (((1,), (1,)), ((), ()))


def _params(*sem):
    return pltpu.CompilerParams(dimension_semantics=sem, vmem_limit_bytes=VMEM_LIMIT)


def _rms(y, gain):
    return y * lax.rsqrt(jnp.mean(y * y, axis=-1, keepdims=True) + EPS) * gain


def _sigmoid(y):
    return 1.0 / (1.0 + jnp.exp(-y))


def _rope_table_kernel(pos_ref, inv_ref, cos_ref, sin_lo_ref, sin_hi_ref):
    ang = pos_ref[...].astype(jnp.float32) * inv_ref[...]
    lane = lax.broadcasted_iota(jnp.int32, ang.shape, 1)
    s = jnp.sin(ang)
    cos_ref[...] = jnp.cos(ang)
    sin_lo_ref[...] = jnp.where(lane < ROT_HALF, -s, 0.0)
    sin_hi_ref[...] = jnp.where((lane >= ROT_HALF) & (lane < ROT_DIM), s, 0.0)


def rope_tables(positions):
    m = positions.size
    tm = 1024
    inv = ROPE_THETA ** (-jnp.arange(0, ROT_DIM, 2, dtype=jnp.float32) / ROT_DIM)
    inv_row = jnp.zeros((1, HEAD_DIM), jnp.float32).at[0, :ROT_DIM].set(jnp.concatenate([inv, inv]))
    tab = jax.ShapeDtypeStruct((m, HEAD_DIM), jnp.float32)
    spec = pl.BlockSpec((tm, HEAD_DIM), lambda i: (i, 0))
    return pl.pallas_call(
        _rope_table_kernel, grid=(m // tm,),
        in_specs=[pl.BlockSpec((tm, 1), lambda i: (i, 0)), pl.BlockSpec((1, HEAD_DIM), lambda i: (0, 0))],
        out_specs=[spec, spec, spec], out_shape=[tab, tab, tab],
        compiler_params=_params("parallel"), name="rope_tables",
    )(positions.reshape(m, 1), inv_row)


def _rmsnorm_kernel(x_ref, g_ref, o_ref):
    o_ref[...] = _rms(x_ref[...], g_ref[...]).astype(o_ref.dtype)


def rmsnorm_bf16(x, gain):
    m, d = x.shape
    tm = 512
    return pl.pallas_call(
        _rmsnorm_kernel, grid=(m // tm,),
        in_specs=[pl.BlockSpec((tm, d), lambda i: (i, 0)), pl.BlockSpec((1, d), lambda i: (0, 0))],
        out_specs=pl.BlockSpec((tm, d), lambda i: (i, 0)),
        out_shape=jax.ShapeDtypeStruct((m, d), jnp.bfloat16),
        compiler_params=_params("parallel"), name="rmsnorm",
    )(x, gain.reshape(1, d))


PROJ_TILE = 512
PROJ_ROWS = 256
NORMED_HEADS = N_HEADS_A + N_HEADS_B
QK_K_BLOCK = NORMED_HEADS
REST_KC_BLOCK = N_HEADS_C
REST_V_BLOCK = 2 * N_HEADS_C


def _qk_norm_kernel(h_ref, w_ref, gain_ref, ones_ref, cos_ref, sin_lo_ref, sin_hi_ref, o_ref):
    def project(r):
        return jnp.dot(h_ref[r * PROJ_ROWS:(r + 1) * PROJ_ROWS, :], w_ref[...], preferred_element_type=jnp.float32)

    n_chunks = h_ref.shape[0] // PROJ_ROWS
    ahead = project(0)
    for r in range(n_chunks):
        rows = slice(r * PROJ_ROWS, (r + 1) * PROJ_ROWS)
        acc = ahead
        ahead = project(r + 1) if r + 1 < n_chunks else None
        ss = jnp.dot((acc * acc).astype(jnp.bfloat16), ones_ref[...], preferred_element_type=jnp.float32)
        y = acc * lax.rsqrt(ss * (1.0 / HEAD_DIM) + EPS) * gain_ref[...]
        c, s_lo, s_hi = cos_ref[rows, :], sin_lo_ref[rows, :], sin_hi_ref[rows, :]
        for hd in range(PROJ_TILE // HEAD_DIM):
            sl = slice(hd * HEAD_DIM, (hd + 1) * HEAD_DIM)
            yh = y[:, sl]
            yh = (yh * c + pltpu.roll(yh, HEAD_DIM - ROT_HALF, axis=1) * s_lo
                  + pltpu.roll(yh, ROT_HALF, axis=1) * s_hi)
            o_ref[rows, sl] = yh.astype(o_ref.dtype)


def _plain_proj_kernel(h_ref, w_ref, o_ref):
    o_ref[...] = jnp.dot(h_ref[...], w_ref[...], preferred_element_type=jnp.float32).astype(o_ref.dtype)


def qkv_project(h, w_qkv, gain_cols, tables):
    m, d = h.shape
    tm, tn = 1024, PROJ_TILE
    per_part = MIX_WIDTH // tn
    normed_tiles = NORMED_HEADS * HEAD_DIM // tn
    assert normed_tiles * tn == NORMED_HEADS * HEAD_DIM and normed_tiles + 1 == per_part
    n_out = 2 * normed_tiles * tn
    ones = jnp.kron(jnp.eye(tn // HEAD_DIM, dtype=jnp.float32),
                    jnp.ones((HEAD_DIM, HEAD_DIM), jnp.float32)).astype(jnp.bfloat16)
    tab_spec = pl.BlockSpec((tm, HEAD_DIM), lambda i, j: (i, 0))
    h_spec = pl.BlockSpec((tm, d), lambda i, j: (i, 0))
    out_spec = pl.BlockSpec((tm, tn), lambda i, j: (i, j))
    out_sds = jax.ShapeDtypeStruct((m, n_out), jnp.bfloat16)
    qk = pl.pallas_call(
        _qk_norm_kernel, grid=(m // tm, 2 * normed_tiles),
        in_specs=[h_spec, pl.BlockSpec((d, tn), lambda i, j: (0, j + j // normed_tiles)),
                  pl.BlockSpec((1, tn), lambda i, j: (0, j)), pl.BlockSpec((tn, tn), lambda i, j: (0, 0)),
                  tab_spec, tab_spec, tab_spec],
        out_specs=out_spec, out_shape=out_sds,
        compiler_params=_params("parallel", "arbitrary"), name="qk_norm_project",
    )(h, w_qkv, gain_cols, ones, *tables)
    rest = pl.pallas_call(
        _plain_proj_kernel, grid=(m // tm, 2 + per_part),
        in_specs=[h_spec, pl.BlockSpec(
            (d, tn), lambda i, j: (0, jnp.where(j < 2, normed_tiles + j * per_part, j + 2 * per_part - 2)))],
        out_specs=out_spec, out_shape=out_sds,
        compiler_params=_params("parallel", "arbitrary"), name="plain_project",
    )(h, w_qkv)
    return qk, rest


def _gate_kernel(h_ref, w_ref, b_ref, o_ref):
    acc = jnp.dot(h_ref[...], w_ref[...], preferred_element_type=jnp.float32)
    o_ref[...] = _sigmoid(acc + b_ref[...]).astype(o_ref.dtype)


def gate_project(h, w_gate, b_gate):
    m, d = h.shape
    n = w_gate.shape[1]
    tm, tn = 1024, 2048
    return pl.pallas_call(
        _gate_kernel, grid=(m // tm, n // tn),
        in_specs=[pl.BlockSpec((tm, d), lambda i, j: (i, 0)), pl.BlockSpec((d, tn), lambda i, j: (0, j)),
                  pl.BlockSpec((1, tn), lambda i, j: (0, j))],
        out_specs=pl.BlockSpec((tm, tn), lambda i, j: (i, j)),
        out_shape=jax.ShapeDtypeStruct((m, n), jnp.bfloat16),
        compiler_params=_params("parallel", "arbitrary"), name="gate_project",
    )(h, w_gate, b_gate.reshape(1, n))


def _transpose_values(v_ref, vt_ref):
    def one(c, carry):
        st = pl.multiple_of(c * KEY_TILE, KEY_TILE)
        vt_ref[c] = v_ref[0, pl.ds(st, KEY_TILE), :].astype(jnp.float32).T.astype(vt_ref.dtype)
        return carry
    lax.fori_loop(0, vt_ref.shape[0], one, 0)


def _softmax_tile(st, vt_tile, m_ref, l_ref, acc_ref):
    m_old = m_ref[...]
    m_new = jnp.maximum(m_old, jnp.max(st, axis=0, keepdims=True))
    alpha = jnp.exp(m_old - m_new)
    p = jnp.exp(st - m_new)
    l_ref[...] = alpha * l_ref[...] + jnp.sum(p, axis=0, keepdims=True)
    acc_ref[...] = alpha * acc_ref[...] + jnp.dot(vt_tile, p.astype(vt_tile.dtype),
                                                  preferred_element_type=jnp.float32)
    m_ref[...] = m_new


def _softmax_init(m_ref, l_ref, acc_ref):
    m_ref[...] = jnp.full_like(m_ref, -jnp.inf)
    l_ref[...] = jnp.zeros_like(l_ref)
    acc_ref[...] = jnp.zeros_like(acc_ref)


DILATED_ROWS = tuple((d + 1) * KEY_TILE for d in DILATIONS)
DILATED_OFFSETS = tuple(sum(DILATED_ROWS[:g]) for g in range(len(DILATIONS)))


def _dilated_kernel(q0_ref, q1_ref, q2_ref, k0_ref, k1_ref, k2_ref, v0_ref, v1_ref, v2_ref, o_ref,
                    vt_ref, bias_ref, ahead_ref, sc_ref):
    q_refs, k_refs, v_refs = (q0_ref, q1_ref, q2_ref), (k0_ref, k1_ref, k2_ref), (v0_ref, v1_ref, v2_ref)
    neg = jnp.float32(-jnp.inf)

    ahead_rows = max(DILATED_ROWS)
    ahead = (lax.broadcasted_iota(jnp.int32, (ahead_rows, KEY_TILE), 0)
             - lax.broadcasted_iota(jnp.int32, (ahead_rows, KEY_TILE), 1))
    ahead_ref[...] = ahead
    for g, d in enumerate(DILATIONS):
        _transpose_values(v_refs[g], vt_ref.at[g])
        sl = slice(DILATED_OFFSETS[g], DILATED_OFFSETS[g] + DILATED_ROWS[g])
        on_grid = jnp.where((ahead[:DILATED_ROWS[g]] & (d - 1)) == 0, 0.0, neg)
        bias_ref[0, sl, :] = on_grid
        bias_ref[1, sl, :] = jnp.where(ahead[:DILATED_ROWS[g]] >= 0, on_grid, neg)

    def query_tile(qi, carry):
        qs = pl.multiple_of(qi * KEY_TILE, KEY_TILE)
        m = jnp.full((1, KEY_TILE), neg, jnp.float32)
        starts = []
        for g, d in enumerate(DILATIONS):
            start = jnp.maximum(qi - d, 0)
            starts.append(start)
            q = q_refs[g][0, pl.ds(qs, KEY_TILE), :]
            which = jnp.where(qi >= d, 1, 0)
            newest = (qi - start) * KEY_TILE
            for a in range(d + 1):
                st = pl.multiple_of((start + a) * KEY_TILE, KEY_TILE)
                r0 = DILATED_OFFSETS[g] + a * KEY_TILE
                scores = lax.dot_general(k_refs[g][0, pl.ds(st, KEY_TILE), :], q, _NT,
                                         preferred_element_type=jnp.float32) * SCALE
                scores = scores + bias_ref[which, r0:r0 + KEY_TILE, :]
                scores = jnp.where(ahead_ref[a * KEY_TILE:(a + 1) * KEY_TILE, :] <= newest, scores, neg)
                sc_ref[r0:r0 + KEY_TILE, :] = scores
                m = jnp.maximum(m, jnp.max(scores, axis=0, keepdims=True))

        l = jnp.zeros((1, KEY_TILE), jnp.float32)
        acc = jnp.zeros((HEAD_DIM, KEY_TILE), jnp.float32)
        for g, d in enumerate(DILATIONS):
            for a in range(d + 1):
                r0 = DILATED_OFFSETS[g] + a * KEY_TILE
                p = jnp.exp(sc_ref[r0:r0 + KEY_TILE, :] - m)
                l = l + jnp.sum(p, axis=0, keepdims=True)
                acc = acc + jnp.dot(vt_ref[g, starts[g] + a], p.astype(vt_ref.dtype),
                                    preferred_element_type=jnp.float32)
        o_ref[0, pl.ds(qs, KEY_TILE), :] = (acc / l).T.astype(o_ref.dtype)
        return carry

    lax.fori_loop(0, o_ref.shape[1] // KEY_TILE, query_tile, 0)


def dilated_attention(qk, rest):
    b, s, _ = qk.shape
    n_tiles = s // KEY_TILE
    assert WINDOW_STEPS == KEY_TILE and all(d & (d - 1) == 0 for d in DILATIONS)
    assert s >= max(DILATED_ROWS)

    def full(first_block, g):
        return pl.BlockSpec((1, s, HEAD_DIM), lambda bi, j: (bi, 0, first_block + g * HEADS_PER_GROUP + j))

    groups = range(len(DILATIONS))
    return pl.pallas_call(
        _dilated_kernel, grid=(b, HEADS_PER_GROUP),
        in_specs=([full(0, g) for g in groups] + [full(QK_K_BLOCK, g) for g in groups]
                  + [full(REST_V_BLOCK, g) for g in groups]),
        out_specs=pl.BlockSpec((1, s, HEAD_DIM), lambda bi, j: (bi, 0, j)),
        out_shape=jax.ShapeDtypeStruct((b, s, HEADS_PER_GROUP * HEAD_DIM), jnp.bfloat16),
        scratch_shapes=[pltpu.VMEM((len(DILATIONS), n_tiles, HEAD_DIM, KEY_TILE), jnp.bfloat16),
                        pltpu.VMEM((2, sum(DILATED_ROWS), KEY_TILE), jnp.float32),
                        pltpu.VMEM((max(DILATED_ROWS), KEY_TILE), jnp.int32),
                        pltpu.VMEM((sum(DILATED_ROWS), KEY_TILE), jnp.float32)],
        compiler_params=_params("parallel", "parallel"), name="dilated_attention",
    )(*([qk] * 6 + [rest] * 3)).reshape(b * s, HEADS_PER_GROUP * HEAD_DIM)


def _moba_kernel(q_ref, k_ref, v_ref, o_ref, kmean_ref, vt_ref, bias_ref, sc_ref, m_ref, l_ref, acc_ref, *,
                 n_blocks):
    for n in range(n_blocks):
        kb = k_ref[0, n * MOBA_BLOCK:(n + 1) * MOBA_BLOCK, :].astype(jnp.float32)
        kmean_ref[n:n + 1, :] = jnp.sum(kb, axis=0, keepdims=True) / MOBA_BLOCK
    _transpose_values(v_ref, vt_ref)
    neg = jnp.float32(-jnp.inf)
    rows = MOBA_CHUNK_BLOCKS * MOBA_BLOCK

    def query_block(own, carry):
        qs = pl.multiple_of(own * MOBA_BLOCK, MOBA_BLOCK)
        q = q_ref[0, pl.ds(qs, MOBA_BLOCK), :]
        gate = lax.dot_general(kmean_ref[...], q.astype(jnp.float32), _NT,
                               precision=lax.Precision.HIGHEST, preferred_element_type=jnp.float32)
        blk = lax.broadcasted_iota(jnp.int32, gate.shape, 0)
        gate = jnp.where(blk < own, gate, neg)
        rank = jnp.zeros(gate.shape, jnp.int32)
        for mth in range(n_blocks):
            gm = gate[mth:mth + 1, :]
            lower = jnp.where(mth < blk, 1, 0)
            rank = rank + jnp.where(gm > gate, 1, 0) + jnp.where(gm == gate, lower, 0)
        rank = jnp.where(blk < own, rank, MOBA_TOPK)
        bias_ref[...] = jnp.where((rank < MOBA_TOPK) | (blk == own), 0.0, neg)

        top = own // MOBA_CHUNK_BLOCKS
        _softmax_init(m_ref, l_ref, acc_ref)

        def chunk_scores(c, causal):
            st = pl.multiple_of(c * rows, rows)
            s = lax.dot_general(k_ref[0, pl.ds(st, rows), :], q, _NT, preferred_element_type=jnp.float32) * SCALE
            s = jnp.concatenate([s[u * MOBA_BLOCK:(u + 1) * MOBA_BLOCK]
                                 + bias_ref[pl.ds(c * MOBA_CHUNK_BLOCKS + u, 1), :]
                                 for u in range(MOBA_CHUNK_BLOCKS)], axis=0)
            if causal:
                ahead = lax.broadcasted_iota(jnp.int32, s.shape, 0) - lax.broadcasted_iota(jnp.int32, s.shape, 1)
                s = jnp.where(ahead <= (own - c * MOBA_CHUNK_BLOCKS) * MOBA_BLOCK, s, neg)
            return s

        first = chunk_scores(top, True)
        sc_ref[...] = first

        def consume(cur, col_max, produce_next):
            m_old = m_ref[...]
            m_new = jnp.maximum(m_old, col_max)
            alpha = jnp.exp(m_old - m_new)
            p = jnp.exp(sc_ref[...] - m_new)
            nxt_max = None
            if produce_next:
                nxt = chunk_scores(cur - 1, False)
                sc_ref[...] = nxt
                nxt_max = jnp.max(nxt, axis=0, keepdims=True)
            l_ref[...] = alpha * l_ref[...] + jnp.sum(p, axis=0, keepdims=True)
            p = p.astype(vt_ref.dtype)
            pv = jnp.zeros(acc_ref.shape, jnp.float32)
            for u in range(rows // KEY_TILE):
                pv = pv + jnp.dot(vt_ref[cur * (rows // KEY_TILE) + u], p[u * KEY_TILE:(u + 1) * KEY_TILE, :],
                                  preferred_element_type=jnp.float32)
            acc_ref[...] = alpha * acc_ref[...] + pv
            m_ref[...] = m_new
            return nxt_max

        last_max = lax.fori_loop(0, top, lambda i, col_max: consume(top - i, col_max, True),
                                 jnp.max(first, axis=0, keepdims=True))
        consume(0, last_max, False)
        o_ref[0, pl.ds(qs, MOBA_BLOCK), :] = (acc_ref[...] / l_ref[...]).T.astype(o_ref.dtype)
        return carry

    lax.fori_loop(0, n_blocks, query_block, 0)


def moba_attention(qk, rest):
    b, s, _ = qk.shape
    n_blocks = s // MOBA_BLOCK
    assert s % MOBA_BLOCK == 0 and n_blocks >= MOBA_TOPK and n_blocks % MOBA_CHUNK_BLOCKS == 0

    def full(first_block):
        return pl.BlockSpec((1, s, HEAD_DIM), lambda bi, h: (bi, 0, first_block + N_HEADS_A + h))

    return pl.pallas_call(
        functools.partial(_moba_kernel, n_blocks=n_blocks), grid=(b, N_HEADS_B),
        in_specs=[full(0), full(QK_K_BLOCK), full(REST_V_BLOCK)],
        out_specs=pl.BlockSpec((1, s, HEAD_DIM), lambda bi, h: (bi, 0, h)),
        out_shape=jax.ShapeDtypeStruct((b, s, N_HEADS_B * HEAD_DIM), jnp.bfloat16),
        scratch_shapes=[pltpu.VMEM((n_blocks, HEAD_DIM), jnp.float32),
                        pltpu.VMEM((s // KEY_TILE, HEAD_DIM, KEY_TILE), jnp.bfloat16),
                        pltpu.VMEM((n_blocks, MOBA_BLOCK), jnp.float32),
                        pltpu.VMEM((MOBA_CHUNK_BLOCKS * MOBA_BLOCK, MOBA_BLOCK), jnp.float32),
                        pltpu.VMEM((1, MOBA_BLOCK), jnp.float32),
                        pltpu.VMEM((1, MOBA_BLOCK), jnp.float32), pltpu.VMEM((HEAD_DIM, MOBA_BLOCK), jnp.float32)],
        compiler_params=_params("parallel", "parallel"), name="moba_attention",
    )(qk, qk, rest).reshape(b * s, N_HEADS_B * HEAD_DIM)


STICK_TILE = 256
STICK_CHUNK = 256
STICK_CUTOFF = -104.0


def _stick_kernel(q_ref, k_ref, v_ref, o_ref, vt_ref, run_ref, acc_ref):
    _transpose_values(v_ref, vt_ref)
    rows = STICK_CHUNK
    chunks_per_q = STICK_TILE // rows
    tiles = rows // KEY_TILE
    sq = lax.broadcasted_iota(jnp.int32, (rows, rows), 0)
    sk = lax.broadcasted_iota(jnp.int32, (rows, rows), 1)
    later = jnp.where(sk > sq, 1.0, 0.0).astype(jnp.bfloat16)
    sub = lax.broadcasted_iota(jnp.int32, (rows, STICK_TILE), 0)
    lane = lax.broadcasted_iota(jnp.int32, (rows, STICK_TILE), 1)

    def tile(q, kt, valid):
        st = pl.multiple_of(kt * rows, rows)
        z = lax.dot_general(k_ref[0, pl.ds(st, rows), :], q, _NT,
                            preferred_element_type=jnp.float32) * SCALE
        log_1m = -(jnp.maximum(z, 0.0) + jnp.log(1.0 + jnp.exp(-jnp.abs(z))))
        if valid is not None:
            log_1m = jnp.where(valid, log_1m, 0.0)
        hi = log_1m.astype(jnp.bfloat16)
        lo = (log_1m - hi.astype(jnp.float32)).astype(jnp.bfloat16)
        inside = (jnp.dot(later, hi, preferred_element_type=jnp.float32)
                  + jnp.dot(later, lo, preferred_element_type=jnp.float32))
        a = jnp.exp(z + log_1m + inside + run_ref[...])
        if valid is not None:
            a = jnp.where(valid, a, 0.0)
        a = a.astype(jnp.bfloat16)
        pv = acc_ref[...]
        for u in range(tiles):
            pv = pv + jnp.dot(vt_ref[kt * tiles + u], a[u * KEY_TILE:(u + 1) * KEY_TILE, :],
                              preferred_element_type=jnp.float32)
        acc_ref[...] = pv
        run_ref[...] += inside[0:1, :] + log_1m[0:1, :]

    def alive():
        return (jnp.max(run_ref[...]) > STICK_CUTOFF).astype(jnp.int32)

    def query_tile(qi, carry):
        qs = pl.multiple_of(qi * STICK_TILE, STICK_TILE)
        q = q_ref[0, pl.ds(qs, STICK_TILE), :]
        run_ref[...] = jnp.zeros_like(run_ref)
        acc_ref[...] = jnp.zeros_like(acc_ref)
        for u in reversed(range(chunks_per_q)):
            tile(q, qi * chunks_per_q + u, u * rows + sub < lane)

        def earlier(c):
            tile(q, c[0], None)
            return c[0] - 1, alive()

        lax.while_loop(lambda c: (c[0] >= 0) & (c[1] > 0), earlier, (qi * chunks_per_q - 1, alive()))
        o_ref[0, pl.ds(qs, STICK_TILE), :] = acc_ref[...].T.astype(o_ref.dtype)
        return carry

    lax.fori_loop(0, o_ref.shape[1] // STICK_TILE, query_tile, 0)


def stick_attention(rest):
    b, s, _ = rest.shape
    t = STICK_TILE

    def full(first_block):
        return pl.BlockSpec((1, s, HEAD_DIM), lambda bi, h: (bi, 0, first_block + h))

    return pl.pallas_call(
        _stick_kernel, grid=(b, N_HEADS_C),
        in_specs=[full(0), full(REST_KC_BLOCK), full(REST_V_BLOCK + NORMED_HEADS)],
        out_specs=pl.BlockSpec((1, s, HEAD_DIM), lambda bi, h: (bi, 0, h)),
        out_shape=jax.ShapeDtypeStruct((b, s, N_HEADS_C * HEAD_DIM), jnp.bfloat16),
        scratch_shapes=[pltpu.VMEM((s // KEY_TILE, HEAD_DIM, KEY_TILE), jnp.bfloat16),
                        pltpu.VMEM((1, t), jnp.float32), pltpu.VMEM((HEAD_DIM, t), jnp.float32)],
        compiler_params=_params("parallel", "parallel"), name="stick_attention",
    )(rest, rest, rest).reshape(b * s, N_HEADS_C * HEAD_DIM)


def _merge_kernel(oa_ref, ob_ref, oc_ref, g_ref, wa_ref, wb_ref, wc_ref, out_ref):
    d = out_ref.shape[1]
    ya = jnp.dot(oa_ref[...], wa_ref[...], preferred_element_type=jnp.float32)
    yb = jnp.dot(ob_ref[...], wb_ref[...], preferred_element_type=jnp.float32)
    yc = jnp.dot(oc_ref[...], wc_ref[...], preferred_element_type=jnp.float32)
    merged = (g_ref[:, 0:d].astype(jnp.float32) * ya + g_ref[:, d:2 * d].astype(jnp.float32) * yb
              + g_ref[:, 2 * d:3 * d].astype(jnp.float32) * yc)
    out_ref[...] = merged.astype(out_ref.dtype)


def merge_branches(o_a, o_b, o_c, gates, w_a, w_b, w_c):
    m = o_b.shape[0]
    d = w_a.shape[1]
    tm = 512

    def rows(a):
        return pl.BlockSpec((tm, a.shape[1]), lambda i: (i, 0))

    def whole(w):
        return pl.BlockSpec(w.shape, lambda i: (0, 0))

    return pl.pallas_call(
        _merge_kernel, grid=(m // tm,),
        in_specs=[rows(o_a), rows(o_b), rows(o_c), rows(gates), whole(w_a), whole(w_b), whole(w_c)],
        out_specs=pl.BlockSpec((tm, d), lambda i: (i, 0)), out_shape=jax.ShapeDtypeStruct((m, d), jnp.bfloat16),
        compiler_params=_params("parallel"), name="merge_branches",
    )(o_a, o_b, o_c, gates, w_a, w_b, w_c)


def _out_proj_kernel(a_ref, w_ref, x_ref, g_ref, xo_ref, ho_ref):
    xn = x_ref[...] + jnp.dot(a_ref[...], w_ref[...], preferred_element_type=jnp.float32)
    xo_ref[...] = xn
    ho_ref[...] = _rms(xn, g_ref[...]).astype(ho_ref.dtype)


def out_project(a, w, x, next_gain):
    m, k = a.shape
    d = w.shape[1]
    tm = 512
    return pl.pallas_call(
        _out_proj_kernel, grid=(m // tm,),
        in_specs=[pl.BlockSpec((tm, k), lambda i: (i, 0)), pl.BlockSpec((k, d), lambda i: (0, 0)),
                  pl.BlockSpec((tm, d), lambda i: (i, 0)), pl.BlockSpec((1, d), lambda i: (0, 0))],
        out_specs=[pl.BlockSpec((tm, d), lambda i: (i, 0)), pl.BlockSpec((tm, d), lambda i: (i, 0))],
        out_shape=[jax.ShapeDtypeStruct((m, d), jnp.float32), jax.ShapeDtypeStruct((m, d), jnp.bfloat16)],
        compiler_params=_params("parallel"), name="out_project",
    )(a, w, x, next_gain.reshape(1, d))


def _mem_kv_kernel(mem_ref, ln_ref, w_ref, gk_ref, kv_ref):
    hm = _rms(mem_ref[...], ln_ref[...]).astype(jnp.bfloat16)
    kv = jnp.dot(hm, w_ref[...], preferred_element_type=jnp.float32)
    half = kv.shape[1] // 2
    for hd in range(N_HEADS_MEM):
        sl = slice(hd * HEAD_DIM, (hd + 1) * HEAD_DIM)
        kv_ref[:, sl] = _rms(kv[:, sl], gk_ref[...]).astype(kv_ref.dtype)
    kv_ref[:, half:] = kv[:, half:].astype(kv_ref.dtype)


def mem_kv(mem2d, ln, wm_kv, gain_k):
    n, d = mem2d.shape
    w = wm_kv.shape[1]
    return pl.pallas_call(
        _mem_kv_kernel, grid=(1,),
        in_specs=[pl.BlockSpec((n, d), lambda i: (0, 0)), pl.BlockSpec((1, d), lambda i: (0, 0)),
                  pl.BlockSpec((d, w), lambda i: (0, 0)), pl.BlockSpec((1, HEAD_DIM), lambda i: (0, 0))],
        out_specs=pl.BlockSpec((n, w), lambda i: (0, 0)),
        out_shape=jax.ShapeDtypeStruct((n, w), jnp.bfloat16),
        compiler_params=_params("arbitrary"), name="mem_kv",
    )(mem2d, ln.reshape(1, d), wm_kv, gain_k.reshape(1, HEAD_DIM))


def _mem_attn_kernel(h_ref, wq_ref, gq_ref, kv_ref, wo_ref, x_ref, g_ref, xo_ref, ho_ref):
    qf = jnp.dot(h_ref[...], wq_ref[...], preferred_element_type=jnp.float32)
    half = kv_ref.shape[2] // 2
    outs = []
    for hd in range(N_HEADS_MEM):
        sl = slice(hd * HEAD_DIM, (hd + 1) * HEAD_DIM)
        qh = _rms(qf[:, sl], gq_ref[...]).astype(jnp.bfloat16)
        s = lax.dot_general(qh, kv_ref[0, :, sl], _NT, preferred_element_type=jnp.float32) * SCALE
        e = jnp.exp(s - jnp.max(s, axis=1, keepdims=True))
        vh = kv_ref[0, :, half + hd * HEAD_DIM:half + (hd + 1) * HEAD_DIM]
        o = jnp.dot(e.astype(jnp.bfloat16), vh, preferred_element_type=jnp.float32)
        outs.append((o / jnp.sum(e, axis=1, keepdims=True)).astype(jnp.bfloat16))
    o_all = jnp.concatenate(outs, axis=1)
    xn = x_ref[...] + jnp.dot(o_all, wo_ref[...], preferred_element_type=jnp.float32)
    xo_ref[...] = xn
    ho_ref[...] = _rms(xn, g_ref[...]).astype(ho_ref.dtype)


def mem_attention(h, wm_q, gain_q, kv, wm_o, x, next_gain, seq):
    m, d = h.shape
    wq = wm_q.shape[1]
    tm = 512
    per_batch = seq // tm
    return pl.pallas_call(
        _mem_attn_kernel, grid=(m // tm,),
        in_specs=[pl.BlockSpec((tm, d), lambda i: (i, 0)), pl.BlockSpec((d, wq), lambda i: (0, 0)),
                  pl.BlockSpec((1, HEAD_DIM), lambda i: (0, 0)),
                  pl.BlockSpec((1,) + kv.shape[1:], lambda i: (i // per_batch, 0, 0)),
                  pl.BlockSpec((wq, d), lambda i: (0, 0)), pl.BlockSpec((tm, d), lambda i: (i, 0)),
                  pl.BlockSpec((1, d), lambda i: (0, 0))],
        out_specs=[pl.BlockSpec((tm, d), lambda i: (i, 0)), pl.BlockSpec((tm, d), lambda i: (i, 0))],
        out_shape=[jax.ShapeDtypeStruct((m, d), jnp.float32), jax.ShapeDtypeStruct((m, d), jnp.bfloat16)],
        compiler_params=_params("parallel"), name="mem_attention",
    )(h, wm_q, gain_q.reshape(1, HEAD_DIM), kv, wm_o, x, next_gain.reshape(1, d))


def _shift_rows(u, prev, k):
    rolled = pltpu.roll(u, k, axis=0)
    row = lax.broadcasted_iota(jnp.int32, prev.shape, 0)
    head = jnp.where(row < k, pltpu.roll(prev, k, axis=0), rolled[:HALO])
    return jnp.concatenate([head, rolled[HALO:]], axis=0)


def _ffn_kernel(h_ref, wg_ref, wv_ref, cwg_ref, cwv_ref, cbg_ref, cbv_ref, wd_ref, x_ref, o_ref,
                halo_ref, *, tiles_per_seq):
    i, f = pl.program_id(0), pl.program_id(1)
    tm = h_ref.shape[0]
    keep = jnp.where(i % tiles_per_seq == 0, 0.0, 1.0)
    prevs = [jnp.where(keep > 0.0, halo_ref[f, part], 0.0) for part in range(2)]

    @pl.when(f == 0)
    def _():
        o_ref[...] = x_ref[...]

    def up(r):
        rows = slice(r * FF_ROWS, (r + 1) * FF_ROWS)
        return [jnp.dot(h_ref[rows, :], w_ref[...], preferred_element_type=jnp.float32) for w_ref in (wg_ref, wv_ref)]

    n_chunks = tm // FF_ROWS
    us = up(0)
    for r in range(n_chunks):
        ahead = up(r + 1) if r + 1 < n_chunks else None
        ys = []
        for part, (cw_ref, cb_ref) in enumerate(((cwg_ref, cbg_ref), (cwv_ref, cbv_ref))):
            u = us[part]
            ys.append(cw_ref[0:1, :] * _shift_rows(u, prevs[part], 2) + cw_ref[1:2, :] * _shift_rows(u, prevs[part], 1)
                      + cw_ref[2:3, :] * u + cb_ref[...])
            prevs[part] = u[FF_ROWS - HALO:, :]
        act = (ys[0] * _sigmoid(ys[0]) * ys[1]).astype(jnp.bfloat16)
        rows = slice(r * FF_ROWS, (r + 1) * FF_ROWS)
        o_ref[rows, :] += jnp.dot(act, wd_ref[...], preferred_element_type=jnp.float32)
        us = ahead
    for part in range(2):
        halo_ref[f, part] = prevs[part]


def _cast_up_kernel(*refs, valid_blocks):
    *w_refs, o_ref = refs
    f = pl.program_id(1)
    for j, w_ref in enumerate(w_refs):
        inside = f * len(w_refs) + j < valid_blocks
        o_ref[:, j * LANES:(j + 1) * LANES] = jnp.where(inside, w_ref[...], 0.0).astype(o_ref.dtype)


def cast_up_weights(w_up, layer, fp):
    _, d, two_ff = w_up.shape
    d_ff = two_ff // 2
    assert d_ff % LANES == 0 and fp % FF_TILE == 0
    half_blocks = d_ff // LANES
    per_tile = FF_TILE // LANES

    def in_spec(j):
        return pl.BlockSpec(
            (None, d, LANES),
            lambda p, f: (layer, 0, jnp.minimum(p * half_blocks + f * per_tile + j, 2 * half_blocks - 1)))

    return pl.pallas_call(
        functools.partial(_cast_up_kernel, valid_blocks=half_blocks), grid=(2, fp // FF_TILE),
        in_specs=[in_spec(j) for j in range(per_tile)],
        out_specs=pl.BlockSpec((None, None, d, FF_TILE), lambda p, f: (p, f, 0, 0)),
        out_shape=jax.ShapeDtypeStruct((2, fp // FF_TILE, d, FF_TILE), jnp.bfloat16),
        compiler_params=_params("parallel", "parallel"), name="cast_up_weights",
    )(*([w_up] * per_tile))


def _cast_down_kernel(w_ref, o_ref, *, valid_blocks):
    o_ref[...] = jnp.where(pl.program_id(0) < valid_blocks, w_ref[...], 0.0).astype(o_ref.dtype)


def cast_down_weights(w_down, layer, fp):
    _, d_ff, d = w_down.shape
    assert d_ff % LANES == 0
    blocks = d_ff // LANES
    return pl.pallas_call(
        functools.partial(_cast_down_kernel, valid_blocks=blocks), grid=(fp // LANES,),
        in_specs=[pl.BlockSpec((None, LANES, d), lambda r: (layer, jnp.minimum(r, blocks - 1), 0))],
        out_specs=pl.BlockSpec((LANES, d), lambda r: (r, 0)),
        out_shape=jax.ShapeDtypeStruct((fp, d), jnp.bfloat16),
        compiler_params=_params("parallel"), name="cast_down_weights",
    )(w_down)


def conv_ffn(h, w_gv, cw_g, cw_v, cb_g, cb_v, w_down, x, seq):
    m, d = h.shape
    fp = w_gv.shape[1] * w_gv.shape[3]
    assert w_gv.shape[3] == FF_TILE
    tm, tf = 512, FF_TILE
    nf = fp // tf
    assert CONV_WIDTH - 1 <= HALO and seq % tm == 0 and tm % FF_ROWS == 0

    def cols(rows):
        return pl.BlockSpec((rows, tf), lambda i, f: (0, f))

    def up_half(part):
        return pl.BlockSpec((None, None, d, tf), lambda i, f: (part, f, 0, 0))

    return pl.pallas_call(
        functools.partial(_ffn_kernel, tiles_per_seq=seq // tm), grid=(m // tm, nf),
        in_specs=[pl.BlockSpec((tm, d), lambda i, f: (i, 0)), up_half(0), up_half(1), cols(CONV_WIDTH),
                  cols(CONV_WIDTH), cols(1), cols(1), pl.BlockSpec((tf, d), lambda i, f: (f, 0)),
                  pl.BlockSpec((tm, d), lambda i, f: (i, 0))],
        out_specs=pl.BlockSpec((tm, d), lambda i, f: (i, 0)),
        out_shape=jax.ShapeDtypeStruct((m, d), jnp.float32),
        scratch_shapes=[pltpu.VMEM((nf, 2, HALO, tf), jnp.float32)],
        compiler_params=_params("arbitrary", "arbitrary"), name="conv_ffn",
    )(h, w_gv, w_gv, cw_g, cw_v, cb_g, cb_v, w_down, x)


def _pad_cols(a, width):
    return jnp.pad(a, ((0, 0), (0, width - a.shape[1])))


def kernel(x, mem, positions, ln_mix, w_qkv, qk_gain, w_br_a, w_br_b, w_br_c, w_gate, b_gate, w_o,
           ln_mem_q, ln_mem_kv, wm_q, wm_kv, wm_o, mem_qk_gain, ln_ffn, w_up, conv_w, conv_b, w_down):
    b, s, d = x.shape
    depth = ln_mix.shape[0]
    bf = jnp.bfloat16
    d_ff = w_down.shape[1]
    fp = -(-d_ff // FF_TILE) * FF_TILE

    tables = rope_tables(positions)
    xf = x.reshape(b * s, d)
    mem2d = mem.reshape(b * mem.shape[1], d)
    h = rmsnorm_bf16(xf, ln_mix[0])
    for l in range(depth):
        gain_cols = jnp.concatenate(
            [jnp.tile(qk_gain[l, 0], N_HEADS_A), jnp.tile(qk_gain[l, 2], N_HEADS_B),
             jnp.tile(qk_gain[l, 1], N_HEADS_A), jnp.tile(qk_gain[l, 3], N_HEADS_B)]).reshape(1, -1)
        qk, rest = qkv_project(h, w_qkv[l].astype(bf), gain_cols, tables)
        qk, rest = qk.reshape(b, s, -1), rest.reshape(b, s, -1)
        gates = gate_project(h, w_gate[l].astype(bf), b_gate[l])
        o_a = dilated_attention(qk, rest)
        o_b = moba_attention(qk, rest)
        o_c = stick_attention(rest)
        merged = merge_branches(o_a, o_b, o_c, gates,
                                w_br_a[l].astype(bf), w_br_b[l].astype(bf), w_br_c[l].astype(bf))
        xf, h = out_project(merged, w_o[l].astype(bf), xf, ln_mem_q[l])

        kv = mem_kv(mem2d, ln_mem_kv[l], wm_kv[l].astype(bf), mem_qk_gain[l, 1])
        kv = kv.reshape(b, mem.shape[1], kv.shape[1])
        xf, h = mem_attention(h, wm_q[l].astype(bf), mem_qk_gain[l, 0], kv, wm_o[l].astype(bf), xf,
                              ln_ffn[l], s)

        cw_g, cw_v = _pad_cols(conv_w[l, :, :d_ff], fp), _pad_cols(conv_w[l, :, d_ff:], fp)
        cb_g = _pad_cols(conv_b[l, :d_ff].reshape(1, d_ff), fp)
        cb_v = _pad_cols(conv_b[l, d_ff:].reshape(1, d_ff), fp)
        xf = conv_ffn(h, cast_up_weights(w_up, l, fp), cw_g, cw_v, cb_g, cb_v,
                      cast_down_weights(w_down, l, fp), xf, s)
        if l + 1 < depth:
            h = rmsnorm_bf16(xf, ln_mix[l + 1])
    return xf.reshape(b, s, d)
```

```python
import functools

import jax
import jax.numpy as jnp
from jax import lax
from jax.experimental import pallas as pl
from jax.experimental.pallas import tpu as pltpu

HEAD_DIM = 128
LANES = 128
DILATIONS = (1, 4, 16)
WINDOW_STEPS = 128
HEADS_PER_GROUP = 2
N_HEADS_A = 6
N_HEADS_B = 6
N_HEADS_C = 4
N_HEADS_MIX = 16
MIX_WIDTH = N_HEADS_MIX * HEAD_DIM
KEY_TILE = 128
MOBA_BLOCK = 256
MOBA_TOPK = 3
MOBA_CHUNK_BLOCKS = 4
MOBA_HEADS = 3
N_HEADS_MEM = 4
ROPE_THETA = 500000.0
ROT_DIM = HEAD_DIM // 4
ROT_HALF = ROT_DIM // 2
CONV_WIDTH = 3
EPS = 1e-6
SCALE = HEAD_DIM ** -0.5
SCALE_LOG2E = SCALE * 1.4426950408889634
FF_TILE = 512
FF_ROWS = 256
HALO = 8
VMEM_LIMIT = 56 * 1024 * 1024

_NT = (((1,), (1,)), ((), ()))


def _params(*sem):
    return pltpu.CompilerParams(dimension_semantics=sem, vmem_limit_bytes=VMEM_LIMIT)


def _rms(y, gain):
    return y * lax.rsqrt(jnp.mean(y * y, axis=-1, keepdims=True) + EPS) * gain


def _sigmoid(y):
    return 1.0 / (1.0 + jnp.exp(-y))


def _rope_table_kernel(pos_ref, inv_ref, cos_ref, sin_lo_ref, sin_hi_ref):
    ang = pos_ref[...].astype(jnp.float32) * inv_ref[...]
    lane = lax.broadcasted_iota(jnp.int32, ang.shape, 1)
    s = jnp.sin(ang)
    cos_ref[...] = jnp.cos(ang)
    sin_lo_ref[...] = jnp.where(lane < ROT_HALF, -s, 0.0)
    sin_hi_ref[...] = jnp.where((lane >= ROT_HALF) & (lane < ROT_DIM), s, 0.0)


def rope_tables(positions):
    m = positions.size
    tm = 1024
    inv = ROPE_THETA ** (-jnp.arange(0, ROT_DIM, 2, dtype=jnp.float32) / ROT_DIM)
    inv_row = jnp.zeros((1, HEAD_DIM), jnp.float32).at[0, :ROT_DIM].set(jnp.concatenate([inv, inv]))
    tab = jax.ShapeDtypeStruct((m, HEAD_DIM), jnp.float32)
    spec = pl.BlockSpec((tm, HEAD_DIM), lambda i: (i, 0))
    return pl.pallas_call(
        _rope_table_kernel, grid=(m // tm,),
        in_specs=[pl.BlockSpec((tm, 1), lambda i: (i, 0)), pl.BlockSpec((1, HEAD_DIM), lambda i: (0, 0))],
        out_specs=[spec, spec, spec], out_shape=[tab, tab, tab],
        compiler_params=_params("parallel"), name="rope_tables",
    )(positions.reshape(m, 1), inv_row)


def _rmsnorm_kernel(x_ref, g_ref, o_ref):
    o_ref[...] = _rms(x_ref[...], g_ref[...]).astype(o_ref.dtype)


def rmsnorm_bf16(x, gain):
    m, d = x.shape
    tm = 512
    return pl.pallas_call(
        _rmsnorm_kernel, grid=(m // tm,),
        in_specs=[pl.BlockSpec((tm, d), lambda i: (i, 0)), pl.BlockSpec((1, d), lambda i: (0, 0))],
        out_specs=pl.BlockSpec((tm, d), lambda i: (i, 0)),
        out_shape=jax.ShapeDtypeStruct((m, d), jnp.bfloat16),
        compiler_params=_params("parallel"), name="rmsnorm",
    )(x, gain.reshape(1, d))


PROJ_TILE = 512
PROJ_ROWS = 256
NORMED_HEADS = N_HEADS_A + N_HEADS_B
QK_K_BLOCK = NORMED_HEADS
REST_KC_BLOCK = N_HEADS_C
REST_V_BLOCK = 2 * N_HEADS_C


def _qk_norm_kernel(h_ref, w_ref, gain_ref, ones_ref, cos_ref, sin_lo_ref, sin_hi_ref, o_ref):
    def project(r):
        return jnp.dot(h_ref[r * PROJ_ROWS:(r + 1) * PROJ_ROWS, :], w_ref[...], preferred_element_type=jnp.float32)

    n_chunks = h_ref.shape[0] // PROJ_ROWS
    ahead = project(0)
    for r in range(n_chunks):
        rows = slice(r * PROJ_ROWS, (r + 1) * PROJ_ROWS)
        acc = ahead
        ahead = project(r + 1) if r + 1 < n_chunks else None
        ss = jnp.dot((acc * acc).astype(jnp.bfloat16), ones_ref[...], preferred_element_type=jnp.float32)
        y = acc * lax.rsqrt(ss * (1.0 / HEAD_DIM) + EPS) * gain_ref[...]
        c, s_lo, s_hi = cos_ref[rows, :], sin_lo_ref[rows, :], sin_hi_ref[rows, :]
        for hd in range(PROJ_TILE // HEAD_DIM):
            sl = slice(hd * HEAD_DIM, (hd + 1) * HEAD_DIM)
            yh = y[:, sl]
            yh = (yh * c + pltpu.roll(yh, HEAD_DIM - ROT_HALF, axis=1) * s_lo
                  + pltpu.roll(yh, ROT_HALF, axis=1) * s_hi)
            o_ref[rows, sl] = yh.astype(o_ref.dtype)


def _plain_proj_kernel(h_ref, w_ref, o_ref):
    o_ref[...] = jnp.dot(h_ref[...], w_ref[...], preferred_element_type=jnp.float32).astype(o_ref.dtype)


def qkv_project(h, w_qkv, gain_cols, tables):
    m, d = h.shape
    tm, tn = 1024, PROJ_TILE
    per_part = MIX_WIDTH // tn
    normed_tiles = NORMED_HEADS * HEAD_DIM // tn
    assert normed_tiles * tn == NORMED_HEADS * HEAD_DIM and normed_tiles + 1 == per_part
    n_out = 2 * normed_tiles * tn
    ones = jnp.kron(jnp.eye(tn // HEAD_DIM, dtype=jnp.float32),
                    jnp.ones((HEAD_DIM, HEAD_DIM), jnp.float32)).astype(jnp.bfloat16)
    tab_spec = pl.BlockSpec((tm, HEAD_DIM), lambda i, j: (i, 0))
    h_spec = pl.BlockSpec((tm, d), lambda i, j: (i, 0))
    out_spec = pl.BlockSpec((tm, tn), lambda i, j: (i, j))
    out_sds = jax.ShapeDtypeStruct((m, n_out), jnp.bfloat16)
    qk = pl.pallas_call(
        _qk_norm_kernel, grid=(m // tm, 2 * normed_tiles),
        in_specs=[h_spec, pl.BlockSpec((d, tn), lambda i, j: (0, j + j // normed_tiles)),
                  pl.BlockSpec((1, tn), lambda i, j: (0, j)), pl.BlockSpec((tn, tn), lambda i, j: (0, 0)),
                  tab_spec, tab_spec, tab_spec],
        out_specs=out_spec, out_shape=out_sds,
        compiler_params=_params("parallel", "arbitrary"), name="qk_norm_project",
    )(h, w_qkv, gain_cols, ones, *tables)
    rest = pl.pallas_call(
        _plain_proj_kernel, grid=(m // tm, 2 + per_part),
        in_specs=[h_spec, pl.BlockSpec(
            (d, tn), lambda i, j: (0, jnp.where(j < 2, normed_tiles + j * per_part, j + 2 * per_part - 2)))],
        out_specs=out_spec, out_shape=out_sds,
        compiler_params=_params("parallel", "arbitrary"), name="plain_project",
    )(h, w_qkv)
    return qk, rest


def _gate_kernel(h_ref, w_ref, b_ref, o_ref):
    acc = jnp.dot(h_ref[...], w_ref[...], preferred_element_type=jnp.float32)
    o_ref[...] = _sigmoid(acc + b_ref[...]).astype(o_ref.dtype)


def gate_project(h, w_gate, b_gate):
    m, d = h.shape
    n = w_gate.shape[1]
    tm, tn = 1024, 2048
    return pl.pallas_call(
        _gate_kernel, grid=(m // tm, n // tn),
        in_specs=[pl.BlockSpec((tm, d), lambda i, j: (i, 0)), pl.BlockSpec((d, tn), lambda i, j: (0, j)),
                  pl.BlockSpec((1, tn), lambda i, j: (0, j))],
        out_specs=pl.BlockSpec((tm, tn), lambda i, j: (i, j)),
        out_shape=jax.ShapeDtypeStruct((m, n), jnp.bfloat16),
        compiler_params=_params("parallel", "arbitrary"), name="gate_project",
    )(h, w_gate, b_gate.reshape(1, n))


def _transpose_values(v_ref, vt_ref):
    tile = vt_ref.shape[-1]

    def one(c, carry):
        st = pl.multiple_of(c * tile, tile)
        vt_ref[c] = v_ref[0, pl.ds(st, tile), :].astype(jnp.float32).T.astype(vt_ref.dtype)
        return carry
    lax.fori_loop(0, vt_ref.shape[0], one, 0)


def _softmax_tile(st, vt_tile, m_ref, l_ref, acc_ref):
    m_old = m_ref[...]
    m_new = jnp.maximum(m_old, jnp.max(st, axis=0, keepdims=True))
    alpha = jnp.exp(m_old - m_new)
    p = jnp.exp(st - m_new)
    l_ref[...] = alpha * l_ref[...] + jnp.sum(p, axis=0, keepdims=True)
    acc_ref[...] = alpha * acc_ref[...] + jnp.dot(vt_tile, p.astype(vt_tile.dtype),
                                                  preferred_element_type=jnp.float32)
    m_ref[...] = m_new


def _softmax_init(m_ref, l_ref, acc_ref):
    m_ref[...] = jnp.full_like(m_ref, -jnp.inf)
    l_ref[...] = jnp.zeros_like(l_ref)
    acc_ref[...] = jnp.zeros_like(acc_ref)


DILATED_ROWS = tuple((d + 1) * KEY_TILE for d in DILATIONS)
DILATED_OFFSETS = tuple(sum(DILATED_ROWS[:g]) for g in range(len(DILATIONS)))


def _dilated_kernel(*refs):
    n_groups, slots = len(DILATIONS), HEADS_PER_GROUP
    per = n_groups * slots
    q_refs, k_refs, v_refs = (refs[i * per:(i + 1) * per] for i in range(3))
    o_ref, vt_ref, bias_ref, ahead_ref, sc_ref = refs[3 * per:]
    neg = jnp.float32(-jnp.inf)

    ahead_rows = max(DILATED_ROWS)
    ahead = (lax.broadcasted_iota(jnp.int32, (ahead_rows, KEY_TILE), 0)
             - lax.broadcasted_iota(jnp.int32, (ahead_rows, KEY_TILE), 1))
    ahead_ref[...] = ahead
    for g, d in enumerate(DILATIONS):
        for j in range(slots):
            _transpose_values(v_refs[g * slots + j], vt_ref.at[g * slots + j])
        sl = slice(DILATED_OFFSETS[g], DILATED_OFFSETS[g] + DILATED_ROWS[g])
        on_grid = jnp.where((ahead[:DILATED_ROWS[g]] & (d - 1)) == 0, 0.0, neg)
        bias_ref[0, sl, :] = on_grid
        bias_ref[1, sl, :] = jnp.where(ahead[:DILATED_ROWS[g]] >= 0, on_grid, neg)

    def query_tile(qi, carry):
        qs = pl.multiple_of(qi * KEY_TILE, KEY_TILE)
        starts = [jnp.maximum(qi - d, 0) for d in DILATIONS]
        ms = []
        for j in range(slots):
            m = jnp.full((1, KEY_TILE), neg, jnp.float32)
            for g, d in enumerate(DILATIONS):
                q = q_refs[g * slots + j][0, pl.ds(qs, KEY_TILE), :]
                which = jnp.where(qi >= d, 1, 0)
                newest = (qi - starts[g]) * KEY_TILE
                for a in range(d + 1):
                    st = pl.multiple_of((starts[g] + a) * KEY_TILE, KEY_TILE)
                    r0 = DILATED_OFFSETS[g] + a * KEY_TILE
                    scores = lax.dot_general(k_refs[g * slots + j][0, pl.ds(st, KEY_TILE), :], q, _NT,
                                             preferred_element_type=jnp.float32) * SCALE_LOG2E
                    scores = scores + bias_ref[which, r0:r0 + KEY_TILE, :]
                    scores = jnp.where(ahead_ref[a * KEY_TILE:(a + 1) * KEY_TILE, :] <= newest, scores, neg)
                    sc_ref[j, r0:r0 + KEY_TILE, :] = scores
                    m = jnp.maximum(m, jnp.max(scores, axis=0, keepdims=True))
            ms.append(m)

        for j in range(slots):
            l = jnp.zeros((1, KEY_TILE), jnp.float32)
            acc = jnp.zeros((HEAD_DIM, KEY_TILE), jnp.float32)
            for g, d in enumerate(DILATIONS):
                for a in range(d + 1):
                    r0 = DILATED_OFFSETS[g] + a * KEY_TILE
                    p = jnp.exp2(sc_ref[j, r0:r0 + KEY_TILE, :] - ms[j])
                    l = l + jnp.sum(p, axis=0, keepdims=True)
                    acc = acc + jnp.dot(vt_ref[g * slots + j, starts[g] + a], p.astype(vt_ref.dtype),
                                        preferred_element_type=jnp.float32)
            o_ref[0, pl.ds(qs, KEY_TILE), j * HEAD_DIM:(j + 1) * HEAD_DIM] = (acc / l).T.astype(o_ref.dtype)
        return carry

    lax.fori_loop(0, o_ref.shape[1] // KEY_TILE, query_tile, 0)


def dilated_attention(qk, rest):
    b, s, _ = qk.shape
    n_tiles = s // KEY_TILE
    assert WINDOW_STEPS == KEY_TILE and all(d & (d - 1) == 0 for d in DILATIONS)
    assert s >= max(DILATED_ROWS)

    heads = len(DILATIONS) * HEADS_PER_GROUP

    def full(first_block):
        return [pl.BlockSpec((1, s, HEAD_DIM), functools.partial(lambda bi, blk: (bi, 0, blk), blk=first_block + h),
                             pipeline_mode=pl.Buffered(1)) for h in range(heads)]

    return pl.pallas_call(
        _dilated_kernel, grid=(b,),
        in_specs=full(0) + full(QK_K_BLOCK) + full(REST_V_BLOCK),
        out_specs=pl.BlockSpec((1, s, HEADS_PER_GROUP * HEAD_DIM), lambda bi: (bi, 0, 0)),
        out_shape=jax.ShapeDtypeStruct((b, s, HEADS_PER_GROUP * HEAD_DIM), jnp.bfloat16),
        scratch_shapes=[pltpu.VMEM((heads, n_tiles, HEAD_DIM, KEY_TILE), jnp.bfloat16),
                        pltpu.VMEM((2, sum(DILATED_ROWS), KEY_TILE), jnp.float32),
                        pltpu.VMEM((max(DILATED_ROWS), KEY_TILE), jnp.int32),
                        pltpu.VMEM((HEADS_PER_GROUP, sum(DILATED_ROWS), KEY_TILE), jnp.float32)],
        compiler_params=_params("parallel"), name="dilated_attention",
    )(*([qk] * (2 * heads) + [rest] * heads)).reshape(b * s, HEADS_PER_GROUP * HEAD_DIM)


def _moba_kernel(*refs, n_blocks):
    hs = range(MOBA_HEADS)
    q_refs, k_refs, v_refs = (refs[i * MOBA_HEADS:(i + 1) * MOBA_HEADS] for i in range(3))
    o_ref, kmean_ref, vt_ref, bias_ref, sc_ref, m_ref, l_ref, acc_ref = refs[3 * MOBA_HEADS:]
    for g in hs:
        for n in range(n_blocks):
            kb = k_refs[g][0, n * MOBA_BLOCK:(n + 1) * MOBA_BLOCK, :].astype(jnp.float32)
            kmean_ref[g, n:n + 1, :] = jnp.sum(kb, axis=0, keepdims=True) / MOBA_BLOCK
        _transpose_values(v_refs[g], vt_ref.at[g])
    neg = jnp.float32(-jnp.inf)
    rows = MOBA_CHUNK_BLOCKS * MOBA_BLOCK

    def query_block(own, carry):
        qs = pl.multiple_of(own * MOBA_BLOCK, MOBA_BLOCK)
        top = own // MOBA_CHUNK_BLOCKS
        qv = [q_refs[g][0, pl.ds(qs, MOBA_BLOCK), :] for g in hs]

        def chunk_scores(g, c, causal):
            st = pl.multiple_of(c * rows, rows)
            s = lax.dot_general(k_refs[g][0, pl.ds(st, rows), :], qv[g], _NT,
                                preferred_element_type=jnp.float32) * SCALE_LOG2E
            s = jnp.concatenate([s[u * MOBA_BLOCK:(u + 1) * MOBA_BLOCK]
                                 + bias_ref[g, pl.ds(c * MOBA_CHUNK_BLOCKS + u, 1), :]
                                 for u in range(MOBA_CHUNK_BLOCKS)], axis=0)
            if causal:
                ahead = lax.broadcasted_iota(jnp.int32, s.shape, 0) - lax.broadcasted_iota(jnp.int32, s.shape, 1)
                s = jnp.where(ahead <= (own - c * MOBA_CHUNK_BLOCKS) * MOBA_BLOCK, s, neg)
            return s

        gates = [lax.dot_general(kmean_ref[g], qv[g].astype(jnp.float32), _NT,
                                 precision=lax.Precision.HIGHEST, preferred_element_type=jnp.float32) for g in hs]
        for g in hs:
            blk = lax.broadcasted_iota(jnp.int32, gates[g].shape, 0)
            gate = jnp.where(blk < own, gates[g], neg)
            rank = jnp.zeros(gate.shape, jnp.int32)
            for mth in range(n_blocks):
                gm = gate[mth:mth + 1, :]
                lower = jnp.where(mth < blk, 1, 0)
                rank = rank + jnp.where(gm > gate, 1, 0) + jnp.where(gm == gate, lower, 0)
            rank = jnp.where(blk < own, rank, MOBA_TOPK)
            bias_ref[g] = jnp.where((rank < MOBA_TOPK) | (blk == own), 0.0, neg)
            m_ref[g] = jnp.full(m_ref.shape[1:], neg, jnp.float32)
            l_ref[g] = jnp.zeros(l_ref.shape[1:], jnp.float32)
            acc_ref[g] = jnp.zeros(acc_ref.shape[1:], jnp.float32)
        firsts = [chunk_scores(g, top, True) for g in hs]
        for g in hs:
            sc_ref[g] = firsts[g]
        first_max = [jnp.max(firsts[g], axis=0, keepdims=True) for g in hs]

        def consume(cur, col_max, produce_next):
            nxt_max = []
            for g in hs:
                m_old = m_ref[g]
                m_new = jnp.maximum(m_old, col_max[g])
                alpha = jnp.exp2(m_old - m_new)
                p = jnp.exp2(sc_ref[g] - m_new)
                if produce_next:
                    nxt = chunk_scores(g, cur - 1, False)
                    sc_ref[g] = nxt
                    nxt_max.append(jnp.max(nxt, axis=0, keepdims=True))
                l_ref[g] = alpha * l_ref[g] + jnp.sum(p, axis=0, keepdims=True)
                p = p.astype(vt_ref.dtype)
                pv = jnp.zeros(acc_ref.shape[1:], jnp.float32)
                for u in range(MOBA_CHUNK_BLOCKS):
                    pv = pv + jnp.dot(vt_ref[g, cur * MOBA_CHUNK_BLOCKS + u],
                                      p[u * MOBA_BLOCK:(u + 1) * MOBA_BLOCK, :], preferred_element_type=jnp.float32)
                acc_ref[g] = alpha * acc_ref[g] + pv
                m_ref[g] = m_new
            return tuple(nxt_max)

        last_max = lax.fori_loop(0, top, lambda i, col_max: consume(top - i, col_max, True), tuple(first_max))
        consume(0, last_max, False)
        for g in hs:
            o_ref[0, pl.ds(qs, MOBA_BLOCK), g * HEAD_DIM:(g + 1) * HEAD_DIM] = (
                (acc_ref[g] / l_ref[g]).T.astype(o_ref.dtype))
        return carry

    lax.fori_loop(0, n_blocks, query_block, 0)


def moba_attention(qk, rest):
    b, s, _ = qk.shape
    n_blocks = s // MOBA_BLOCK
    assert s % MOBA_BLOCK == 0 and n_blocks >= MOBA_TOPK and n_blocks % MOBA_CHUNK_BLOCKS == 0

    g = MOBA_HEADS
    assert N_HEADS_B % g == 0

    def full(first_block, j):
        return pl.BlockSpec((1, s, HEAD_DIM), lambda bi, h: (bi, 0, first_block + N_HEADS_A + h * g + j))

    heads = range(g)
    return pl.pallas_call(
        functools.partial(_moba_kernel, n_blocks=n_blocks), grid=(b, N_HEADS_B // g),
        in_specs=([full(0, j) for j in heads] + [full(QK_K_BLOCK, j) for j in heads]
                  + [full(REST_V_BLOCK, j) for j in heads]),
        out_specs=pl.BlockSpec((1, s, g * HEAD_DIM), lambda bi, h: (bi, 0, h)),
        out_shape=jax.ShapeDtypeStruct((b, s, N_HEADS_B * HEAD_DIM), jnp.bfloat16),
        scratch_shapes=[pltpu.VMEM((g, n_blocks, HEAD_DIM), jnp.float32),
                        pltpu.VMEM((g, n_blocks, HEAD_DIM, MOBA_BLOCK), jnp.bfloat16),
                        pltpu.VMEM((g, n_blocks, MOBA_BLOCK), jnp.float32),
                        pltpu.VMEM((g, MOBA_CHUNK_BLOCKS * MOBA_BLOCK, MOBA_BLOCK), jnp.float32),
                        pltpu.VMEM((g, 1, MOBA_BLOCK), jnp.float32), pltpu.VMEM((g, 1, MOBA_BLOCK), jnp.float32),
                        pltpu.VMEM((g, HEAD_DIM, MOBA_BLOCK), jnp.float32)],
        compiler_params=_params("parallel", "parallel"), name="moba_attention",
    )(*([qk] * (2 * g) + [rest] * g)).reshape(b * s, N_HEADS_B * HEAD_DIM)


STICK_TILE = 256
STICK_CHUNK = 256
STICK_HEADS = 4
STICK_CUTOFF = -104.0


def _stick_kernel(*refs):
    hs = range(STICK_HEADS)
    q_refs, k_refs, v_refs = (refs[i * STICK_HEADS:(i + 1) * STICK_HEADS] for i in range(3))
    o_ref, vt_ref, run_ref, acc_ref = refs[3 * STICK_HEADS:]
    for g in hs:
        _transpose_values(v_refs[g], vt_ref.at[g])
    rows = STICK_CHUNK
    chunks_per_q = STICK_TILE // rows
    sq = lax.broadcasted_iota(jnp.int32, (rows, rows), 0)
    sk = lax.broadcasted_iota(jnp.int32, (rows, rows), 1)
    later = jnp.where(sk > sq, 1.0, 0.0).astype(jnp.bfloat16)
    sub = lax.broadcasted_iota(jnp.int32, (rows, STICK_TILE), 0)
    lane = lax.broadcasted_iota(jnp.int32, (rows, STICK_TILE), 1)

    def tile(qv, kt, valid):
        st = pl.multiple_of(kt * rows, rows)
        zs = [lax.dot_general(k_refs[g][0, pl.ds(st, rows), :], qv[g], _NT,
                              preferred_element_type=jnp.float32) * SCALE for g in hs]
        logs = []
        for g in hs:
            log_1m = -(jnp.maximum(zs[g], 0.0) + jnp.log(1.0 + jnp.exp(-jnp.abs(zs[g]))))
            logs.append(log_1m if valid is None else jnp.where(valid, log_1m, 0.0))
        insides = []
        for g in hs:
            hi = logs[g].astype(jnp.bfloat16)
            lo = (logs[g] - hi.astype(jnp.float32)).astype(jnp.bfloat16)
            insides.append(jnp.dot(later, hi, preferred_element_type=jnp.float32)
                           + jnp.dot(later, lo, preferred_element_type=jnp.float32))
        for g in hs:
            a = jnp.exp(zs[g] + logs[g] + insides[g] + run_ref[g])
            if valid is not None:
                a = jnp.where(valid, a, 0.0)
            acc_ref[g] += jnp.dot(vt_ref[g, kt], a.astype(jnp.bfloat16), preferred_element_type=jnp.float32)
            run_ref[g] += insides[g][0:1, :] + logs[g][0:1, :]

    def alive():
        return (jnp.max(run_ref[...]) > STICK_CUTOFF).astype(jnp.int32)

    def query_tile(qi, carry):
        qs = pl.multiple_of(qi * STICK_TILE, STICK_TILE)
        qv = [q_refs[g][0, pl.ds(qs, STICK_TILE), :] for g in hs]
        run_ref[...] = jnp.zeros_like(run_ref)
        acc_ref[...] = jnp.zeros_like(acc_ref)
        for u in reversed(range(chunks_per_q)):
            tile(qv, qi * chunks_per_q + u, u * rows + sub < lane)

        def earlier(c):
            tile(qv, c[0], None)
            return c[0] - 1, alive()

        lax.while_loop(lambda c: (c[0] >= 0) & (c[1] > 0), earlier, (qi * chunks_per_q - 1, alive()))
        for g in hs:
            o_ref[0, pl.ds(qs, STICK_TILE), g * HEAD_DIM:(g + 1) * HEAD_DIM] = acc_ref[g].T.astype(o_ref.dtype)
        return carry

    lax.fori_loop(0, o_ref.shape[1] // STICK_TILE, query_tile, 0)


def stick_attention(rest):
    b, s, _ = rest.shape
    t = STICK_TILE

    g = STICK_HEADS
    assert N_HEADS_C % g == 0 and STICK_TILE % STICK_CHUNK == 0

    def full(first_block, j):
        return pl.BlockSpec((1, s, HEAD_DIM), lambda bi, h: (bi, 0, first_block + h * g + j))

    heads = range(g)
    return pl.pallas_call(
        _stick_kernel, grid=(b, N_HEADS_C // g),
        in_specs=([full(0, j) for j in heads] + [full(REST_KC_BLOCK, j) for j in heads]
                  + [full(REST_V_BLOCK + NORMED_HEADS, j) for j in heads]),
        out_specs=pl.BlockSpec((1, s, g * HEAD_DIM), lambda bi, h: (bi, 0, h)),
        out_shape=jax.ShapeDtypeStruct((b, s, N_HEADS_C * HEAD_DIM), jnp.bfloat16),
        scratch_shapes=[pltpu.VMEM((g, s // STICK_CHUNK, HEAD_DIM, STICK_CHUNK), jnp.bfloat16),
                        pltpu.VMEM((g, 1, t), jnp.float32), pltpu.VMEM((g, HEAD_DIM, t), jnp.float32)],
        compiler_params=_params("parallel", "parallel"), name="stick_attention",
    )(*([rest] * (3 * g))).reshape(b * s, N_HEADS_C * HEAD_DIM)


def _merge_kernel(oa_ref, ob_ref, oc_ref, g_ref, wa_ref, wb_ref, wc_ref, out_ref):
    d = out_ref.shape[1]
    ya = jnp.dot(oa_ref[...], wa_ref[...], preferred_element_type=jnp.float32)
    yb = jnp.dot(ob_ref[...], wb_ref[...], preferred_element_type=jnp.float32)
    yc = jnp.dot(oc_ref[...], wc_ref[...], preferred_element_type=jnp.float32)
    merged = (g_ref[:, 0:d].astype(jnp.float32) * ya + g_ref[:, d:2 * d].astype(jnp.float32) * yb
              + g_ref[:, 2 * d:3 * d].astype(jnp.float32) * yc)
    out_ref[...] = merged.astype(out_ref.dtype)


def merge_branches(o_a, o_b, o_c, gates, w_a, w_b, w_c):
    m = o_b.shape[0]
    d = w_a.shape[1]
    tm = 512

    def rows(a):
        return pl.BlockSpec((tm, a.shape[1]), lambda i: (i, 0))

    def whole(w):
        return pl.BlockSpec(w.shape, lambda i: (0, 0))

    return pl.pallas_call(
        _merge_kernel, grid=(m // tm,),
        in_specs=[rows(o_a), rows(o_b), rows(o_c), rows(gates), whole(w_a), whole(w_b), whole(w_c)],
        out_specs=pl.BlockSpec((tm, d), lambda i: (i, 0)), out_shape=jax.ShapeDtypeStruct((m, d), jnp.bfloat16),
        compiler_params=_params("parallel"), name="merge_branches",
    )(o_a, o_b, o_c, gates, w_a, w_b, w_c)


def _out_proj_kernel(a_ref, w_ref, x_ref, g_ref, xo_ref, ho_ref):
    xn = x_ref[...] + jnp.dot(a_ref[...], w_ref[...], preferred_element_type=jnp.float32)
    xo_ref[...] = xn
    ho_ref[...] = _rms(xn, g_ref[...]).astype(ho_ref.dtype)


def out_project(a, w, x, next_gain):
    m, k = a.shape
    d = w.shape[1]
    tm = 512
    return pl.pallas_call(
        _out_proj_kernel, grid=(m // tm,),
        in_specs=[pl.BlockSpec((tm, k), lambda i: (i, 0)), pl.BlockSpec((k, d), lambda i: (0, 0)),
                  pl.BlockSpec((tm, d), lambda i: (i, 0)), pl.BlockSpec((1, d), lambda i: (0, 0))],
        out_specs=[pl.BlockSpec((tm, d), lambda i: (i, 0)), pl.BlockSpec((tm, d), lambda i: (i, 0))],
        out_shape=[jax.ShapeDtypeStruct((m, d), jnp.float32), jax.ShapeDtypeStruct((m, d), jnp.bfloat16)],
        compiler_params=_params("parallel"), name="out_project",
    )(a, w, x, next_gain.reshape(1, d))


def _mem_kv_kernel(mem_ref, ln_ref, w_ref, gk_ref, kv_ref):
    hm = _rms(mem_ref[...], ln_ref[...]).astype(jnp.bfloat16)
    kv = jnp.dot(hm, w_ref[...], preferred_element_type=jnp.float32)
    half = kv.shape[1] // 2
    for hd in range(N_HEADS_MEM):
        sl = slice(hd * HEAD_DIM, (hd + 1) * HEAD_DIM)
        kv_ref[:, sl] = _rms(kv[:, sl], gk_ref[...]).astype(kv_ref.dtype)
    kv_ref[:, half:] = kv[:, half:].astype(kv_ref.dtype)


def mem_kv(mem2d, ln, wm_kv, gain_k):
    n, d = mem2d.shape
    w = wm_kv.shape[1]
    return pl.pallas_call(
        _mem_kv_kernel, grid=(1,),
        in_specs=[pl.BlockSpec((n, d), lambda i: (0, 0)), pl.BlockSpec((1, d), lambda i: (0, 0)),
                  pl.BlockSpec((d, w), lambda i: (0, 0)), pl.BlockSpec((1, HEAD_DIM), lambda i: (0, 0))],
        out_specs=pl.BlockSpec((n, w), lambda i: (0, 0)),
        out_shape=jax.ShapeDtypeStruct((n, w), jnp.bfloat16),
        compiler_params=_params("arbitrary"), name="mem_kv",
    )(mem2d, ln.reshape(1, d), wm_kv, gain_k.reshape(1, HEAD_DIM))


def _mem_attn_kernel(h_ref, wq_ref, gq_ref, kv_ref, wo_ref, x_ref, g_ref, xo_ref, ho_ref):
    qf = jnp.dot(h_ref[...], wq_ref[...], preferred_element_type=jnp.float32)
    half = kv_ref.shape[2] // 2
    outs = []
    for hd in range(N_HEADS_MEM):
        sl = slice(hd * HEAD_DIM, (hd + 1) * HEAD_DIM)
        qh = _rms(qf[:, sl], gq_ref[...]).astype(jnp.bfloat16)
        s = lax.dot_general(qh, kv_ref[0, :, sl], _NT, preferred_element_type=jnp.float32) * SCALE
        e = jnp.exp(s - jnp.max(s, axis=1, keepdims=True))
        vh = kv_ref[0, :, half + hd * HEAD_DIM:half + (hd + 1) * HEAD_DIM]
        o = jnp.dot(e.astype(jnp.bfloat16), vh, preferred_element_type=jnp.float32)
        outs.append((o / jnp.sum(e, axis=1, keepdims=True)).astype(jnp.bfloat16))
    o_all = jnp.concatenate(outs, axis=1)
    xn = x_ref[...] + jnp.dot(o_all, wo_ref[...], preferred_element_type=jnp.float32)
    xo_ref[...] = xn
    ho_ref[...] = _rms(xn, g_ref[...]).astype(ho_ref.dtype)


def mem_attention(h, wm_q, gain_q, kv, wm_o, x, next_gain, seq):
    m, d = h.shape
    wq = wm_q.shape[1]
    tm = 512
    per_batch = seq // tm
    return pl.pallas_call(
        _mem_attn_kernel, grid=(m // tm,),
        in_specs=[pl.BlockSpec((tm, d), lambda i: (i, 0)), pl.BlockSpec((d, wq), lambda i: (0, 0)),
                  pl.BlockSpec((1, HEAD_DIM), lambda i: (0, 0)),
                  pl.BlockSpec((1,) + kv.shape[1:], lambda i: (i // per_batch, 0, 0)),
                  pl.BlockSpec((wq, d), lambda i: (0, 0)), pl.BlockSpec((tm, d), lambda i: (i, 0)),
                  pl.BlockSpec((1, d), lambda i: (0, 0))],
        out_specs=[pl.BlockSpec((tm, d), lambda i: (i, 0)), pl.BlockSpec((tm, d), lambda i: (i, 0))],
        out_shape=[jax.ShapeDtypeStruct((m, d), jnp.float32), jax.ShapeDtypeStruct((m, d), jnp.bfloat16)],
        compiler_params=_params("parallel"), name="mem_attention",
    )(h, wm_q, gain_q.reshape(1, HEAD_DIM), kv, wm_o, x, next_gain.reshape(1, d))


def _shift_rows(u, prev, k):
    rolled = pltpu.roll(u, k, axis=0)
    row = lax.broadcasted_iota(jnp.int32, prev.shape, 0)
    head = jnp.where(row < k, pltpu.roll(prev, k, axis=0), rolled[:HALO])
    return jnp.concatenate([head, rolled[HALO:]], axis=0)


def _ffn_kernel(h_ref, wg_ref, wv_ref, cwg_ref, cwv_ref, cbg_ref, cbv_ref, wd_ref, x_ref, o_ref,
                halo_ref, *, tiles_per_seq):
    i, f = pl.program_id(0), pl.program_id(1)
    tm = h_ref.shape[0]
    keep = jnp.where(i % tiles_per_seq == 0, 0.0, 1.0)
    prevs = [jnp.where(keep > 0.0, halo_ref[f, part], 0.0) for part in range(2)]

    @pl.when(f == 0)
    def _():
        o_ref[...] = x_ref[...]

    def up(r):
        rows = slice(r * FF_ROWS, (r + 1) * FF_ROWS)
        return [jnp.dot(h_ref[rows, :], w_ref[...], preferred_element_type=jnp.float32) for w_ref in (wg_ref, wv_ref)]

    n_chunks = tm // FF_ROWS
    us = up(0)
    for r in range(n_chunks):
        ahead = up(r + 1) if r + 1 < n_chunks else None
        ys = []
        for part, (cw_ref, cb_ref) in enumerate(((cwg_ref, cbg_ref), (cwv_ref, cbv_ref))):
            u = us[part]
            ys.append(cw_ref[0:1, :] * _shift_rows(u, prevs[part], 2) + cw_ref[1:2, :] * _shift_rows(u, prevs[part], 1)
                      + cw_ref[2:3, :] * u + cb_ref[...])
            prevs[part] = u[FF_ROWS - HALO:, :]
        act = (ys[0] * _sigmoid(ys[0]) * ys[1]).astype(jnp.bfloat16)
        rows = slice(r * FF_ROWS, (r + 1) * FF_ROWS)
        o_ref[rows, :] += jnp.dot(act, wd_ref[...], preferred_element_type=jnp.float32)
        us = ahead
    for part in range(2):
        halo_ref[f, part] = prevs[part]


def _cast_up_kernel(*refs, valid_blocks):
    *w_refs, o_ref = refs
    f = pl.program_id(1)
    for j, w_ref in enumerate(w_refs):
        inside = f * len(w_refs) + j < valid_blocks
        o_ref[:, j * LANES:(j + 1) * LANES] = jnp.where(inside, w_ref[...], 0.0).astype(o_ref.dtype)


def cast_up_weights(w_up, layer, fp):
    _, d, two_ff = w_up.shape
    d_ff = two_ff // 2
    assert d_ff % LANES == 0 and fp % FF_TILE == 0
    half_blocks = d_ff // LANES
    per_tile = FF_TILE // LANES

    def in_spec(j):
        return pl.BlockSpec(
            (None, d, LANES),
            lambda p, f: (layer, 0, jnp.minimum(p * half_blocks + f * per_tile + j, 2 * half_blocks - 1)))

    return pl.pallas_call(
        functools.partial(_cast_up_kernel, valid_blocks=half_blocks), grid=(2, fp // FF_TILE),
        in_specs=[in_spec(j) for j in range(per_tile)],
        out_specs=pl.BlockSpec((None, None, d, FF_TILE), lambda p, f: (p, f, 0, 0)),
        out_shape=jax.ShapeDtypeStruct((2, fp // FF_TILE, d, FF_TILE), jnp.bfloat16),
        compiler_params=_params("parallel", "parallel"), name="cast_up_weights",
    )(*([w_up] * per_tile))


def _cast_down_kernel(w_ref, o_ref, *, valid_blocks):
    o_ref[...] = jnp.where(pl.program_id(0) < valid_blocks, w_ref[...], 0.0).astype(o_ref.dtype)


def cast_down_weights(w_down, layer, fp):
    _, d_ff, d = w_down.shape
    assert d_ff % LANES == 0
    blocks = d_ff // LANES
    return pl.pallas_call(
        functools.partial(_cast_down_kernel, valid_blocks=blocks), grid=(fp // LANES,),
        in_specs=[pl.BlockSpec((None, LANES, d), lambda r: (layer, jnp.minimum(r, blocks - 1), 0))],
        out_specs=pl.BlockSpec((LANES, d), lambda r: (r, 0)),
        out_shape=jax.ShapeDtypeStruct((fp, d), jnp.bfloat16),
        compiler_params=_params("parallel"), name="cast_down_weights",
    )(w_down)


def conv_ffn(h, w_gv, cw_g, cw_v, cb_g, cb_v, w_down, x, seq):
    m, d = h.shape
    fp = w_gv.shape[1] * w_gv.shape[3]
    assert w_gv.shape[3] == FF_TILE
    tm, tf = 512, FF_TILE
    nf = fp // tf
    assert CONV_WIDTH - 1 <= HALO and seq % tm == 0 and tm % FF_ROWS == 0

    def cols(rows):
        return pl.BlockSpec((rows, tf), lambda i, f: (0, f))

    def up_half(part):
        return pl.BlockSpec((None, None, d, tf), lambda i, f: (part, f, 0, 0))

    return pl.pallas_call(
        functools.partial(_ffn_kernel, tiles_per_seq=seq // tm), grid=(m // tm, nf),
        in_specs=[pl.BlockSpec((tm, d), lambda i, f: (i, 0)), up_half(0), up_half(1), cols(CONV_WIDTH),
                  cols(CONV_WIDTH), cols(1), cols(1), pl.BlockSpec((tf, d), lambda i, f: (f, 0)),
                  pl.BlockSpec((tm, d), lambda i, f: (i, 0))],
        out_specs=pl.BlockSpec((tm, d), lambda i, f: (i, 0)),
        out_shape=jax.ShapeDtypeStruct((m, d), jnp.float32),
        scratch_shapes=[pltpu.VMEM((nf, 2, HALO, tf), jnp.float32)],
        compiler_params=_params("arbitrary", "arbitrary"), name="conv_ffn",
    )(h, w_gv, w_gv, cw_g, cw_v, cb_g, cb_v, w_down, x)


def _pad_cols(a, width):
    return jnp.pad(a, ((0, 0), (0, width - a.shape[1])))


def kernel(x, mem, positions, ln_mix, w_qkv, qk_gain, w_br_a, w_br_b, w_br_c, w_gate, b_gate, w_o,
           ln_mem_q, ln_mem_kv, wm_q, wm_kv, wm_o, mem_qk_gain, ln_ffn, w_up, conv_w, conv_b, w_down):
    b, s, d = x.shape
    depth = ln_mix.shape[0]
    bf = jnp.bfloat16
    d_ff = w_down.shape[1]
    fp = -(-d_ff // FF_TILE) * FF_TILE

    tables = rope_tables(positions)
    xf = x.reshape(b * s, d)
    mem2d = mem.reshape(b * mem.shape[1], d)
    h = rmsnorm_bf16(xf, ln_mix[0])
    for l in range(depth):
        gain_cols = jnp.concatenate(
            [jnp.tile(qk_gain[l, 0], N_HEADS_A), jnp.tile(qk_gain[l, 2], N_HEADS_B),
             jnp.tile(qk_gain[l, 1], N_HEADS_A), jnp.tile(qk_gain[l, 3], N_HEADS_B)]).reshape(1, -1)
        qk, rest = qkv_project(h, w_qkv[l].astype(bf), gain_cols, tables)
        qk, rest = qk.reshape(b, s, -1), rest.reshape(b, s, -1)
        gates = gate_project(h, w_gate[l].astype(bf), b_gate[l])
        o_a = dilated_attention(qk, rest)
        o_b = moba_attention(qk, rest)
        o_c = stick_attention(rest)
        merged = merge_branches(o_a, o_b, o_c, gates,
                                w_br_a[l].astype(bf), w_br_b[l].astype(bf), w_br_c[l].astype(bf))
        xf, h = out_project(merged, w_o[l].astype(bf), xf, ln_mem_q[l])

        kv = mem_kv(mem2d, ln_mem_kv[l], wm_kv[l].astype(bf), mem_qk_gain[l, 1])
        kv = kv.reshape(b, mem.shape[1], kv.shape[1])
        xf, h = mem_attention(h, wm_q[l].astype(bf), mem_qk_gain[l, 0], kv, wm_o[l].astype(bf), xf,
                              ln_ffn[l], s)

        cw_g, cw_v = _pad_cols(conv_w[l, :, :d_ff], fp), _pad_cols(conv_w[l, :, d_ff:], fp)
        cb_g = _pad_cols(conv_b[l, :d_ff].reshape(1, d_ff), fp)
        cb_v = _pad_cols(conv_b[l, d_ff:].reshape(1, d_ff), fp)
        xf = conv_ffn(h, cast_up_weights(w_up, l, fp), cw_g, cw_v, cb_g, cb_v,
                      cast_down_weights(w_down, l, fp), xf, s)
        if l + 1 < depth:
            h = rmsnorm_bf16(xf, ln_mix[l + 1])
    return xf.reshape(b, s, d)
```

```python
import functools

import jax
import jax.numpy as jnp
from jax import lax
from jax.experimental import pallas as pl
from jax.experimental.pallas import tpu as pltpu

HEAD_DIM = 128
LANES = 128
DILATIONS = (1, 4, 16)
WINDOW_STEPS = 128
HEADS_PER_GROUP = 2
N_HEADS_A = 6
N_HEADS_B = 6
N_HEADS_C = 4
N_HEADS_MIX = 16
MIX_WIDTH = N_HEADS_MIX * HEAD_DIM
KEY_TILE = 128
MOBA_BLOCK = 256
MOBA_TOPK = 3
MOBA_CHUNK_BLOCKS = 4
MOBA_HEADS = 3
N_HEADS_MEM = 4
ROPE_THETA = 500000.0
ROT_DIM = HEAD_DIM // 4
ROT_HALF = ROT_DIM // 2
CONV_WIDTH = 3
EPS = 1e-6
SCALE = HEAD_DIM ** -0.5
SCALE_LOG2E = SCALE * 1.4426950408889634
FF_TILE = 512
FF_ROWS = 128
HALO = 8
VMEM_LIMIT = 56 * 1024 * 1024

_NT = (((1,), (1,)), ((), ()))


def _params(*sem):
    return pltpu.CompilerParams(dimension_semantics=sem, vmem_limit_bytes=VMEM_LIMIT)


def _rms(y, gain):
    return y * lax.rsqrt(jnp.mean(y * y, axis=-1, keepdims=True) + EPS) * gain


def _sigmoid(y):
    return 1.0 / (1.0 + jnp.exp(-y))


def _rope_table_kernel(pos_ref, inv_ref, cos_ref, sin_lo_ref, sin_hi_ref):
    ang = pos_ref[...].astype(jnp.float32) * inv_ref[...]
    lane = lax.broadcasted_iota(jnp.int32, ang.shape, 1)
    s = jnp.sin(ang)
    cos_ref[...] = jnp.cos(ang)
    sin_lo_ref[...] = jnp.where(lane < ROT_HALF, -s, 0.0)
    sin_hi_ref[...] = jnp.where((lane >= ROT_HALF) & (lane < ROT_DIM), s, 0.0)


def rope_tables(positions):
    m = positions.size
    tm = 1024
    inv = ROPE_THETA ** (-jnp.arange(0, ROT_DIM, 2, dtype=jnp.float32) / ROT_DIM)
    inv_row = jnp.zeros((1, HEAD_DIM), jnp.float32).at[0, :ROT_DIM].set(jnp.concatenate([inv, inv]))
    tab = jax.ShapeDtypeStruct((m, HEAD_DIM), jnp.float32)
    spec = pl.BlockSpec((tm, HEAD_DIM), lambda i: (i, 0))
    return pl.pallas_call(
        _rope_table_kernel, grid=(m // tm,),
        in_specs=[pl.BlockSpec((tm, 1), lambda i: (i, 0)), pl.BlockSpec((1, HEAD_DIM), lambda i: (0, 0))],
        out_specs=[spec, spec, spec], out_shape=[tab, tab, tab],
        compiler_params=_params("parallel"), name="rope_tables",
    )(positions.reshape(m, 1), inv_row)


def _rmsnorm_kernel(x_ref, g_ref, o_ref):
    o_ref[...] = _rms(x_ref[...], g_ref[...]).astype(o_ref.dtype)


def rmsnorm_bf16(x, gain):
    m, d = x.shape
    tm = 512
    return pl.pallas_call(
        _rmsnorm_kernel, grid=(m // tm,),
        in_specs=[pl.BlockSpec((tm, d), lambda i: (i, 0)), pl.BlockSpec((1, d), lambda i: (0, 0))],
        out_specs=pl.BlockSpec((tm, d), lambda i: (i, 0)),
        out_shape=jax.ShapeDtypeStruct((m, d), jnp.bfloat16),
        compiler_params=_params("parallel"), name="rmsnorm",
    )(x, gain.reshape(1, d))


PROJ_TILE = 512
PROJ_ROWS = 256
NORMED_HEADS = N_HEADS_A + N_HEADS_B
QK_K_BLOCK = NORMED_HEADS
REST_KC_BLOCK = N_HEADS_C
REST_V_BLOCK = 2 * N_HEADS_C


def _qk_norm_kernel(h_ref, w_ref, gain_ref, ones_ref, cos_ref, sin_lo_ref, sin_hi_ref, o_ref):
    def project(r):
        return jnp.dot(h_ref[r * PROJ_ROWS:(r + 1) * PROJ_ROWS, :], w_ref[...], preferred_element_type=jnp.float32)

    n_chunks = h_ref.shape[0] // PROJ_ROWS
    ahead = project(0)
    for r in range(n_chunks):
        rows = slice(r * PROJ_ROWS, (r + 1) * PROJ_ROWS)
        acc = ahead
        ahead = project(r + 1) if r + 1 < n_chunks else None
        ss = jnp.dot((acc * acc).astype(jnp.bfloat16), ones_ref[...], preferred_element_type=jnp.float32)
        y = acc * lax.rsqrt(ss * (1.0 / HEAD_DIM) + EPS) * gain_ref[...]
        c, s_lo, s_hi = cos_ref[rows, :], sin_lo_ref[rows, :], sin_hi_ref[rows, :]
        for hd in range(PROJ_TILE // HEAD_DIM):
            sl = slice(hd * HEAD_DIM, (hd + 1) * HEAD_DIM)
            yh = y[:, sl]
            yh = (yh * c + pltpu.roll(yh, HEAD_DIM - ROT_HALF, axis=1) * s_lo
                  + pltpu.roll(yh, ROT_HALF, axis=1) * s_hi)
            o_ref[rows, sl] = yh.astype(o_ref.dtype)


def _plain_proj_kernel(h_ref, w_ref, o_ref):
    o_ref[...] = jnp.dot(h_ref[...], w_ref[...], preferred_element_type=jnp.float32).astype(o_ref.dtype)


def qkv_project(h, w_qkv, gain_cols, tables):
    m, d = h.shape
    tm, tn = 1024, PROJ_TILE
    per_part = MIX_WIDTH // tn
    normed_tiles = NORMED_HEADS * HEAD_DIM // tn
    assert normed_tiles * tn == NORMED_HEADS * HEAD_DIM and normed_tiles + 1 == per_part
    n_out = 2 * normed_tiles * tn
    ones = jnp.kron(jnp.eye(tn // HEAD_DIM, dtype=jnp.float32),
                    jnp.ones((HEAD_DIM, HEAD_DIM), jnp.float32)).astype(jnp.bfloat16)
    tab_spec = pl.BlockSpec((tm, HEAD_DIM), lambda i, j: (i, 0))
    h_spec = pl.BlockSpec((tm, d), lambda i, j: (i, 0))
    out_spec = pl.BlockSpec((tm, tn), lambda i, j: (i, j))
    out_sds = jax.ShapeDtypeStruct((m, n_out), jnp.bfloat16)
    qk = pl.pallas_call(
        _qk_norm_kernel, grid=(m // tm, 2 * normed_tiles),
        in_specs=[h_spec, pl.BlockSpec((d, tn), lambda i, j: (0, j + j // normed_tiles)),
                  pl.BlockSpec((1, tn), lambda i, j: (0, j)), pl.BlockSpec((tn, tn), lambda i, j: (0, 0)),
                  tab_spec, tab_spec, tab_spec],
        out_specs=out_spec, out_shape=out_sds,
        compiler_params=_params("parallel", "arbitrary"), name="qk_norm_project",
    )(h, w_qkv, gain_cols, ones, *tables)
    rest = pl.pallas_call(
        _plain_proj_kernel, grid=(m // tm, 2 + per_part),
        in_specs=[h_spec, pl.BlockSpec(
            (d, tn), lambda i, j: (0, jnp.where(j < 2, normed_tiles + j * per_part, j + 2 * per_part - 2)))],
        out_specs=out_spec, out_shape=out_sds,
        compiler_params=_params("parallel", "arbitrary"), name="plain_project",
    )(h, w_qkv)
    return qk, rest


def _gate_kernel(h_ref, w_ref, b_ref, o_ref):
    acc = jnp.dot(h_ref[...], w_ref[...], preferred_element_type=jnp.float32)
    o_ref[...] = _sigmoid(acc + b_ref[...]).astype(o_ref.dtype)


def gate_project(h, w_gate, b_gate):
    m, d = h.shape
    n = w_gate.shape[1]
    tm, tn = 1024, 2048
    return pl.pallas_call(
        _gate_kernel, grid=(m // tm, n // tn),
        in_specs=[pl.BlockSpec((tm, d), lambda i, j: (i, 0)), pl.BlockSpec((d, tn), lambda i, j: (0, j)),
                  pl.BlockSpec((1, tn), lambda i, j: (0, j))],
        out_specs=pl.BlockSpec((tm, tn), lambda i, j: (i, j)),
        out_shape=jax.ShapeDtypeStruct((m, n), jnp.bfloat16),
        compiler_params=_params("parallel", "arbitrary"), name="gate_project",
    )(h, w_gate, b_gate.reshape(1, n))


def _transpose_values(v_ref, vt_ref):
    tile = vt_ref.shape[-1]

    def one(c, carry):
        st = pl.multiple_of(c * tile, tile)
        vt_ref[c] = v_ref[0, pl.ds(st, tile), :].astype(jnp.float32).T.astype(vt_ref.dtype)
        return carry
    lax.fori_loop(0, vt_ref.shape[0], one, 0)


def _softmax_tile(st, vt_tile, m_ref, l_ref, acc_ref):
    m_old = m_ref[...]
    m_new = jnp.maximum(m_old, jnp.max(st, axis=0, keepdims=True))
    alpha = jnp.exp(m_old - m_new)
    p = jnp.exp(st - m_new)
    l_ref[...] = alpha * l_ref[...] + jnp.sum(p, axis=0, keepdims=True)
    acc_ref[...] = alpha * acc_ref[...] + jnp.dot(vt_tile, p.astype(vt_tile.dtype),
                                                  preferred_element_type=jnp.float32)
    m_ref[...] = m_new


def _softmax_init(m_ref, l_ref, acc_ref):
    m_ref[...] = jnp.full_like(m_ref, -jnp.inf)
    l_ref[...] = jnp.zeros_like(l_ref)
    acc_ref[...] = jnp.zeros_like(acc_ref)


DILATED_ROWS = tuple((d + 1) * KEY_TILE for d in DILATIONS)
DILATED_OFFSETS = tuple(sum(DILATED_ROWS[:g]) for g in range(len(DILATIONS)))


def _dilated_kernel(*refs):
    n_groups, slots = len(DILATIONS), HEADS_PER_GROUP
    per = n_groups * slots
    q_refs, k_refs, v_refs = (refs[i * per:(i + 1) * per] for i in range(3))
    o_ref, vt_ref, bias_ref, ahead_ref, sc_ref = refs[3 * per:]
    neg = jnp.float32(-jnp.inf)

    ahead_rows = max(DILATED_ROWS)
    ahead = (lax.broadcasted_iota(jnp.int32, (ahead_rows, KEY_TILE), 0)
             - lax.broadcasted_iota(jnp.int32, (ahead_rows, KEY_TILE), 1))
    ahead_ref[...] = ahead
    for g, d in enumerate(DILATIONS):
        for j in range(slots):
            _transpose_values(v_refs[g * slots + j], vt_ref.at[g * slots + j])
        sl = slice(DILATED_OFFSETS[g], DILATED_OFFSETS[g] + DILATED_ROWS[g])
        on_grid = jnp.where((ahead[:DILATED_ROWS[g]] & (d - 1)) == 0, 0.0, neg)
        bias_ref[0, sl, :] = on_grid
        bias_ref[1, sl, :] = jnp.where(ahead[:DILATED_ROWS[g]] >= 0, on_grid, neg)

    def query_tile(qi, carry):
        qs = pl.multiple_of(qi * KEY_TILE, KEY_TILE)
        starts = [jnp.maximum(qi - d, 0) for d in DILATIONS]
        ms = []
        for j in range(slots):
            m = jnp.full((1, KEY_TILE), neg, jnp.float32)
            for g, d in enumerate(DILATIONS):
                q = q_refs[g * slots + j][0, pl.ds(qs, KEY_TILE), :]
                which = jnp.where(qi >= d, 1, 0)
                newest = (qi - starts[g]) * KEY_TILE
                st = pl.multiple_of(starts[g] * KEY_TILE, KEY_TILE)
                raw = lax.dot_general(k_refs[g * slots + j][0, pl.ds(st, DILATED_ROWS[g]), :], q, _NT,
                                      preferred_element_type=jnp.float32)
                for a in range(d + 1):
                    r0 = DILATED_OFFSETS[g] + a * KEY_TILE
                    scores = raw[a * KEY_TILE:(a + 1) * KEY_TILE, :] * SCALE_LOG2E
                    scores = scores + bias_ref[which, r0:r0 + KEY_TILE, :]
                    scores = jnp.where(ahead_ref[a * KEY_TILE:(a + 1) * KEY_TILE, :] <= newest, scores, neg)
                    sc_ref[j, r0:r0 + KEY_TILE, :] = scores
                    m = jnp.maximum(m, jnp.max(scores, axis=0, keepdims=True))
            ms.append(m)

        for j in range(slots):
            l = jnp.zeros((1, KEY_TILE), jnp.float32)
            acc = jnp.zeros((HEAD_DIM, KEY_TILE), jnp.float32)
            for g, d in enumerate(DILATIONS):
                for a in range(d + 1):
                    r0 = DILATED_OFFSETS[g] + a * KEY_TILE
                    p = jnp.exp2(sc_ref[j, r0:r0 + KEY_TILE, :] - ms[j])
                    l = l + jnp.sum(p, axis=0, keepdims=True)
                    acc = acc + jnp.dot(vt_ref[g * slots + j, starts[g] + a], p.astype(vt_ref.dtype),
                                        preferred_element_type=jnp.float32)
            o_ref[0, pl.ds(qs, KEY_TILE), j * HEAD_DIM:(j + 1) * HEAD_DIM] = (acc / l).T.astype(o_ref.dtype)
        return carry

    lax.fori_loop(0, o_ref.shape[1] // KEY_TILE, query_tile, 0)


def dilated_attention(qk, rest):
    b, s, _ = qk.shape
    n_tiles = s // KEY_TILE
    assert WINDOW_STEPS == KEY_TILE and all(d & (d - 1) == 0 for d in DILATIONS)
    assert s >= max(DILATED_ROWS)

    heads = len(DILATIONS) * HEADS_PER_GROUP

    def full(first_block):
        return [pl.BlockSpec((1, s, HEAD_DIM), functools.partial(lambda bi, blk: (bi, 0, blk), blk=first_block + h),
                             pipeline_mode=pl.Buffered(1)) for h in range(heads)]

    return pl.pallas_call(
        _dilated_kernel, grid=(b,),
        in_specs=full(0) + full(QK_K_BLOCK) + full(REST_V_BLOCK),
        out_specs=pl.BlockSpec((1, s, HEADS_PER_GROUP * HEAD_DIM), lambda bi: (bi, 0, 0)),
        out_shape=jax.ShapeDtypeStruct((b, s, HEADS_PER_GROUP * HEAD_DIM), jnp.bfloat16),
        scratch_shapes=[pltpu.VMEM((heads, n_tiles, HEAD_DIM, KEY_TILE), jnp.bfloat16),
                        pltpu.VMEM((2, sum(DILATED_ROWS), KEY_TILE), jnp.float32),
                        pltpu.VMEM((max(DILATED_ROWS), KEY_TILE), jnp.int32),
                        pltpu.VMEM((HEADS_PER_GROUP, sum(DILATED_ROWS), KEY_TILE), jnp.float32)],
        compiler_params=_params("parallel"), name="dilated_attention",
    )(*([qk] * (2 * heads) + [rest] * heads)).reshape(b * s, HEADS_PER_GROUP * HEAD_DIM)


def _moba_kernel(*refs, n_blocks):
    hs = range(MOBA_HEADS)
    q_refs, k_refs, v_refs = (refs[i * MOBA_HEADS:(i + 1) * MOBA_HEADS] for i in range(3))
    o_ref, kmean_ref, vt_ref, bias_ref, sc_ref, m_ref, l_ref, acc_ref = refs[3 * MOBA_HEADS:]
    for g in hs:
        for n in range(n_blocks):
            kb = k_refs[g][0, n * MOBA_BLOCK:(n + 1) * MOBA_BLOCK, :].astype(jnp.float32)
            kmean_ref[g, n:n + 1, :] = jnp.sum(kb, axis=0, keepdims=True) / MOBA_BLOCK
        _transpose_values(v_refs[g], vt_ref.at[g])
    neg = jnp.float32(-jnp.inf)
    rows = MOBA_CHUNK_BLOCKS * MOBA_BLOCK

    def query_block(own, carry):
        qs = pl.multiple_of(own * MOBA_BLOCK, MOBA_BLOCK)
        top = own // MOBA_CHUNK_BLOCKS
        qv = [q_refs[g][0, pl.ds(qs, MOBA_BLOCK), :] for g in hs]

        def chunk_scores(g, c, causal):
            st = pl.multiple_of(c * rows, rows)
            s = lax.dot_general(k_refs[g][0, pl.ds(st, rows), :], qv[g], _NT,
                                preferred_element_type=jnp.float32) * SCALE_LOG2E
            s = jnp.concatenate([s[u * MOBA_BLOCK:(u + 1) * MOBA_BLOCK]
                                 + bias_ref[g, pl.ds(c * MOBA_CHUNK_BLOCKS + u, 1), :]
                                 for u in range(MOBA_CHUNK_BLOCKS)], axis=0)
            if causal:
                ahead = lax.broadcasted_iota(jnp.int32, s.shape, 0) - lax.broadcasted_iota(jnp.int32, s.shape, 1)
                s = jnp.where(ahead <= (own - c * MOBA_CHUNK_BLOCKS) * MOBA_BLOCK, s, neg)
            return s

        gates = [lax.dot_general(kmean_ref[g], qv[g].astype(jnp.float32), _NT,
                                 precision=lax.Precision.HIGHEST, preferred_element_type=jnp.float32) for g in hs]
        for g in hs:
            blk = lax.broadcasted_iota(jnp.int32, gates[g].shape, 0)
            gate = jnp.where(blk < own, gates[g], neg)
            rank = jnp.zeros(gate.shape, jnp.int32)
            for mth in range(n_blocks):
                gm = gate[mth:mth + 1, :]
                lower = jnp.where(mth < blk, 1, 0)
                rank = rank + jnp.where(gm > gate, 1, 0) + jnp.where(gm == gate, lower, 0)
            rank = jnp.where(blk < own, rank, MOBA_TOPK)
            bias_ref[g] = jnp.where((rank < MOBA_TOPK) | (blk == own), 0.0, neg)
            m_ref[g] = jnp.full(m_ref.shape[1:], neg, jnp.float32)
            l_ref[g] = jnp.zeros(l_ref.shape[1:], jnp.float32)
            acc_ref[g] = jnp.zeros(acc_ref.shape[1:], jnp.float32)
        firsts = [chunk_scores(g, top, True) for g in hs]
        for g in hs:
            sc_ref[g] = firsts[g]
        first_max = [jnp.max(firsts[g], axis=0, keepdims=True) for g in hs]

        def consume(cur, col_max, produce_next):
            nxt_max = []
            for g in hs:
                m_old = m_ref[g]
                m_new = jnp.maximum(m_old, col_max[g])
                alpha = jnp.exp2(m_old - m_new)
                p = jnp.exp2(sc_ref[g] - m_new)
                if produce_next:
                    nxt = chunk_scores(g, cur - 1, False)
                    sc_ref[g] = nxt
                    nxt_max.append(jnp.max(nxt, axis=0, keepdims=True))
                l_ref[g] = alpha * l_ref[g] + jnp.sum(p, axis=0, keepdims=True)
                p = p.astype(vt_ref.dtype)
                pv = jnp.zeros(acc_ref.shape[1:], jnp.float32)
                for u in range(MOBA_CHUNK_BLOCKS):
                    pv = pv + jnp.dot(vt_ref[g, cur * MOBA_CHUNK_BLOCKS + u],
                                      p[u * MOBA_BLOCK:(u + 1) * MOBA_BLOCK, :], preferred_element_type=jnp.float32)
                acc_ref[g] = alpha * acc_ref[g] + pv
                m_ref[g] = m_new
            return tuple(nxt_max)

        last_max = lax.fori_loop(0, top, lambda i, col_max: consume(top - i, col_max, True), tuple(first_max))
        consume(0, last_max, False)
        for g in hs:
            o_ref[0, pl.ds(qs, MOBA_BLOCK), g * HEAD_DIM:(g + 1) * HEAD_DIM] = (
                (acc_ref[g] / l_ref[g]).T.astype(o_ref.dtype))
        return carry

    lax.fori_loop(0, n_blocks, query_block, 0)


def moba_attention(qk, rest):
    b, s, _ = qk.shape
    n_blocks = s // MOBA_BLOCK
    assert s % MOBA_BLOCK == 0 and n_blocks >= MOBA_TOPK and n_blocks % MOBA_CHUNK_BLOCKS == 0

    g = MOBA_HEADS
    assert N_HEADS_B % g == 0

    def full(first_block, j):
        return pl.BlockSpec((1, s, HEAD_DIM), lambda bi, h: (bi, 0, first_block + N_HEADS_A + h * g + j))

    heads = range(g)
    return pl.pallas_call(
        functools.partial(_moba_kernel, n_blocks=n_blocks), grid=(b, N_HEADS_B // g),
        in_specs=([full(0, j) for j in heads] + [full(QK_K_BLOCK, j) for j in heads]
                  + [full(REST_V_BLOCK, j) for j in heads]),
        out_specs=pl.BlockSpec((1, s, g * HEAD_DIM), lambda bi, h: (bi, 0, h)),
        out_shape=jax.ShapeDtypeStruct((b, s, N_HEADS_B * HEAD_DIM), jnp.bfloat16),
        scratch_shapes=[pltpu.VMEM((g, n_blocks, HEAD_DIM), jnp.float32),
                        pltpu.VMEM((g, n_blocks, HEAD_DIM, MOBA_BLOCK), jnp.bfloat16),
                        pltpu.VMEM((g, n_blocks, MOBA_BLOCK), jnp.float32),
                        pltpu.VMEM((g, MOBA_CHUNK_BLOCKS * MOBA_BLOCK, MOBA_BLOCK), jnp.float32),
                        pltpu.VMEM((g, 1, MOBA_BLOCK), jnp.float32), pltpu.VMEM((g, 1, MOBA_BLOCK), jnp.float32),
                        pltpu.VMEM((g, HEAD_DIM, MOBA_BLOCK), jnp.float32)],
        compiler_params=_params("parallel", "parallel"), name="moba_attention",
    )(*([qk] * (2 * g) + [rest] * g)).reshape(b * s, N_HEADS_B * HEAD_DIM)


STICK_TILE = 256
STICK_CHUNK = 256
STICK_HEADS = 4
STICK_CUTOFF = -104.0


def _stick_kernel(*refs):
    hs = range(STICK_HEADS)
    q_refs, k_refs, v_refs = (refs[i * STICK_HEADS:(i + 1) * STICK_HEADS] for i in range(3))
    o_ref, vt_ref, run_ref, acc_ref = refs[3 * STICK_HEADS:]
    for g in hs:
        _transpose_values(v_refs[g], vt_ref.at[g])
    rows = STICK_CHUNK
    chunks_per_q = STICK_TILE // rows
    sq = lax.broadcasted_iota(jnp.int32, (rows, rows), 0)
    sk = lax.broadcasted_iota(jnp.int32, (rows, rows), 1)
    later = jnp.where(sk > sq, 1.0, 0.0).astype(jnp.bfloat16)
    sub = lax.broadcasted_iota(jnp.int32, (rows, STICK_TILE), 0)
    lane = lax.broadcasted_iota(jnp.int32, (rows, STICK_TILE), 1)

    def tile(qv, kt, valid):
        st = pl.multiple_of(kt * rows, rows)
        zs = [lax.dot_general(k_refs[g][0, pl.ds(st, rows), :], qv[g], _NT,
                              preferred_element_type=jnp.float32) * SCALE for g in hs]
        logs = []
        for g in hs:
            log_1m = -(jnp.maximum(zs[g], 0.0) + jnp.log(1.0 + jnp.exp(-jnp.abs(zs[g]))))
            logs.append(log_1m if valid is None else jnp.where(valid, log_1m, 0.0))
        insides = []
        for g in hs:
            hi = logs[g].astype(jnp.bfloat16)
            lo = (logs[g] - hi.astype(jnp.float32)).astype(jnp.bfloat16)
            insides.append(jnp.dot(later, hi, preferred_element_type=jnp.float32)
                           + jnp.dot(later, lo, preferred_element_type=jnp.float32))
        for g in hs:
            a = jnp.exp(zs[g] + logs[g] + insides[g] + run_ref[g])
            if valid is not None:
                a = jnp.where(valid, a, 0.0)
            acc_ref[g] += jnp.dot(vt_ref[g, kt], a.astype(jnp.bfloat16), preferred_element_type=jnp.float32)
            run_ref[g] += insides[g][0:1, :] + logs[g][0:1, :]

    def alive():
        return (jnp.max(run_ref[...]) > STICK_CUTOFF).astype(jnp.int32)

    def query_tile(qi, carry):
        qs = pl.multiple_of(qi * STICK_TILE, STICK_TILE)
        qv = [q_refs[g][0, pl.ds(qs, STICK_TILE), :] for g in hs]
        run_ref[...] = jnp.zeros_like(run_ref)
        acc_ref[...] = jnp.zeros_like(acc_ref)
        for u in reversed(range(chunks_per_q)):
            tile(qv, qi * chunks_per_q + u, u * rows + sub < lane)

        def earlier(c):
            tile(qv, c[0], None)
            return c[0] - 1, alive()

        lax.while_loop(lambda c: (c[0] >= 0) & (c[1] > 0), earlier, (qi * chunks_per_q - 1, alive()))
        for g in hs:
            o_ref[0, pl.ds(qs, STICK_TILE), g * HEAD_DIM:(g + 1) * HEAD_DIM] = acc_ref[g].T.astype(o_ref.dtype)
        return carry

    lax.fori_loop(0, o_ref.shape[1] // STICK_TILE, query_tile, 0)


def stick_attention(rest):
    b, s, _ = rest.shape
    t = STICK_TILE

    g = STICK_HEADS
    assert N_HEADS_C % g == 0 and STICK_TILE % STICK_CHUNK == 0

    def full(first_block, j):
        return pl.BlockSpec((1, s, HEAD_DIM), lambda bi, h: (bi, 0, first_block + h * g + j))

    heads = range(g)
    return pl.pallas_call(
        _stick_kernel, grid=(b, N_HEADS_C // g),
        in_specs=([full(0, j) for j in heads] + [full(REST_KC_BLOCK, j) for j in heads]
                  + [full(REST_V_BLOCK + NORMED_HEADS, j) for j in heads]),
        out_specs=pl.BlockSpec((1, s, g * HEAD_DIM), lambda bi, h: (bi, 0, h)),
        out_shape=jax.ShapeDtypeStruct((b, s, N_HEADS_C * HEAD_DIM), jnp.bfloat16),
        scratch_shapes=[pltpu.VMEM((g, s // STICK_CHUNK, HEAD_DIM, STICK_CHUNK), jnp.bfloat16),
                        pltpu.VMEM((g, 1, t), jnp.float32), pltpu.VMEM((g, HEAD_DIM, t), jnp.float32)],
        compiler_params=_params("parallel", "parallel"), name="stick_attention",
    )(*([rest] * (3 * g))).reshape(b * s, N_HEADS_C * HEAD_DIM)


def _merge_kernel(oa_ref, ob_ref, oc_ref, g_ref, wa_ref, wb_ref, wc_ref, out_ref):
    d = out_ref.shape[1]
    ya = jnp.dot(oa_ref[...], wa_ref[...], preferred_element_type=jnp.float32)
    yb = jnp.dot(ob_ref[...], wb_ref[...], preferred_element_type=jnp.float32)
    yc = jnp.dot(oc_ref[...], wc_ref[...], preferred_element_type=jnp.float32)
    merged = (g_ref[:, 0:d].astype(jnp.float32) * ya + g_ref[:, d:2 * d].astype(jnp.float32) * yb
              + g_ref[:, 2 * d:3 * d].astype(jnp.float32) * yc)
    out_ref[...] = merged.astype(out_ref.dtype)


def merge_branches(o_a, o_b, o_c, gates, w_a, w_b, w_c):
    m = o_b.shape[0]
    d = w_a.shape[1]
    tm = 512

    def rows(a):
        return pl.BlockSpec((tm, a.shape[1]), lambda i: (i, 0))

    def whole(w):
        return pl.BlockSpec(w.shape, lambda i: (0, 0))

    return pl.pallas_call(
        _merge_kernel, grid=(m // tm,),
        in_specs=[rows(o_a), rows(o_b), rows(o_c), rows(gates), whole(w_a), whole(w_b), whole(w_c)],
        out_specs=pl.BlockSpec((tm, d), lambda i: (i, 0)), out_shape=jax.ShapeDtypeStruct((m, d), jnp.bfloat16),
        compiler_params=_params("parallel"), name="merge_branches",
    )(o_a, o_b, o_c, gates, w_a, w_b, w_c)


def _out_proj_kernel(a_ref, w_ref, x_ref, g_ref, xo_ref, ho_ref):
    xn = x_ref[...] + jnp.dot(a_ref[...], w_ref[...], preferred_element_type=jnp.float32)
    xo_ref[...] = xn
    ho_ref[...] = _rms(xn, g_ref[...]).astype(ho_ref.dtype)


def out_project(a, w, x, next_gain):
    m, k = a.shape
    d = w.shape[1]
    tm = 512
    return pl.pallas_call(
        _out_proj_kernel, grid=(m // tm,),
        in_specs=[pl.BlockSpec((tm, k), lambda i: (i, 0)), pl.BlockSpec((k, d), lambda i: (0, 0)),
                  pl.BlockSpec((tm, d), lambda i: (i, 0)), pl.BlockSpec((1, d), lambda i: (0, 0))],
        out_specs=[pl.BlockSpec((tm, d), lambda i: (i, 0)), pl.BlockSpec((tm, d), lambda i: (i, 0))],
        out_shape=[jax.ShapeDtypeStruct((m, d), jnp.float32), jax.ShapeDtypeStruct((m, d), jnp.bfloat16)],
        compiler_params=_params("parallel"), name="out_project",
    )(a, w, x, next_gain.reshape(1, d))


def _mem_kv_kernel(mem_ref, ln_ref, w_ref, gk_ref, kv_ref):
    hm = _rms(mem_ref[...], ln_ref[...]).astype(jnp.bfloat16)
    kv = jnp.dot(hm, w_ref[...], preferred_element_type=jnp.float32)
    half = kv.shape[1] // 2
    for hd in range(N_HEADS_MEM):
        sl = slice(hd * HEAD_DIM, (hd + 1) * HEAD_DIM)
        kv_ref[:, sl] = _rms(kv[:, sl], gk_ref[...]).astype(kv_ref.dtype)
    kv_ref[:, half:] = kv[:, half:].astype(kv_ref.dtype)


def mem_kv(mem2d, ln, wm_kv, gain_k):
    n, d = mem2d.shape
    w = wm_kv.shape[1]
    return pl.pallas_call(
        _mem_kv_kernel, grid=(1,),
        in_specs=[pl.BlockSpec((n, d), lambda i: (0, 0)), pl.BlockSpec((1, d), lambda i: (0, 0)),
                  pl.BlockSpec((d, w), lambda i: (0, 0)), pl.BlockSpec((1, HEAD_DIM), lambda i: (0, 0))],
        out_specs=pl.BlockSpec((n, w), lambda i: (0, 0)),
        out_shape=jax.ShapeDtypeStruct((n, w), jnp.bfloat16),
        compiler_params=_params("arbitrary"), name="mem_kv",
    )(mem2d, ln.reshape(1, d), wm_kv, gain_k.reshape(1, HEAD_DIM))


def _mem_attn_kernel(h_ref, wq_ref, gq_ref, kv_ref, wo_ref, x_ref, g_ref, xo_ref, ho_ref):
    qf = jnp.dot(h_ref[...], wq_ref[...], preferred_element_type=jnp.float32)
    half = kv_ref.shape[2] // 2
    outs = []
    for hd in range(N_HEADS_MEM):
        sl = slice(hd * HEAD_DIM, (hd + 1) * HEAD_DIM)
        qh = _rms(qf[:, sl], gq_ref[...]).astype(jnp.bfloat16)
        s = lax.dot_general(qh, kv_ref[0, :, sl], _NT, preferred_element_type=jnp.float32) * SCALE
        e = jnp.exp(s - jnp.max(s, axis=1, keepdims=True))
        vh = kv_ref[0, :, half + hd * HEAD_DIM:half + (hd + 1) * HEAD_DIM]
        o = jnp.dot(e.astype(jnp.bfloat16), vh, preferred_element_type=jnp.float32)
        outs.append((o / jnp.sum(e, axis=1, keepdims=True)).astype(jnp.bfloat16))
    o_all = jnp.concatenate(outs, axis=1)
    xn = x_ref[...] + jnp.dot(o_all, wo_ref[...], preferred_element_type=jnp.float32)
    xo_ref[...] = xn
    ho_ref[...] = _rms(xn, g_ref[...]).astype(ho_ref.dtype)


def mem_attention(h, wm_q, gain_q, kv, wm_o, x, next_gain, seq):
    m, d = h.shape
    wq = wm_q.shape[1]
    tm = 512
    per_batch = seq // tm
    return pl.pallas_call(
        _mem_attn_kernel, grid=(m // tm,),
        in_specs=[pl.BlockSpec((tm, d), lambda i: (i, 0)), pl.BlockSpec((d, wq), lambda i: (0, 0)),
                  pl.BlockSpec((1, HEAD_DIM), lambda i: (0, 0)),
                  pl.BlockSpec((1,) + kv.shape[1:], lambda i: (i // per_batch, 0, 0)),
                  pl.BlockSpec((wq, d), lambda i: (0, 0)), pl.BlockSpec((tm, d), lambda i: (i, 0)),
                  pl.BlockSpec((1, d), lambda i: (0, 0))],
        out_specs=[pl.BlockSpec((tm, d), lambda i: (i, 0)), pl.BlockSpec((tm, d), lambda i: (i, 0))],
        out_shape=[jax.ShapeDtypeStruct((m, d), jnp.float32), jax.ShapeDtypeStruct((m, d), jnp.bfloat16)],
        compiler_params=_params("parallel"), name="mem_attention",
    )(h, wm_q, gain_q.reshape(1, HEAD_DIM), kv, wm_o, x, next_gain.reshape(1, d))


def _shift_rows(u, prev, k):
    rolled = pltpu.roll(u, k, axis=0)
    row = lax.broadcasted_iota(jnp.int32, prev.shape, 0)
    head = jnp.where(row < k, pltpu.roll(prev, k, axis=0), rolled[:HALO])
    return jnp.concatenate([head, rolled[HALO:]], axis=0)


def _ffn_kernel(h_ref, wg_ref, wv_ref, cwg_ref, cwv_ref, cbg_ref, cbv_ref, wd_ref, x_ref, o_ref,
                halo_ref, *, tiles_per_seq):
    i, f = pl.program_id(0), pl.program_id(1)
    tm = h_ref.shape[0]
    keep = jnp.where(i % tiles_per_seq == 0, 0.0, 1.0)
    prevs = [jnp.where(keep > 0.0, halo_ref[f, part], 0.0) for part in range(2)]

    @pl.when(f == 0)
    def _():
        o_ref[...] = x_ref[...]

    def up(r):
        rows = slice(r * FF_ROWS, (r + 1) * FF_ROWS)
        return [jnp.dot(h_ref[rows, :], w_ref[...], preferred_element_type=jnp.float32) for w_ref in (wg_ref, wv_ref)]

    n_chunks = tm // FF_ROWS
    us = up(0)
    for r in range(n_chunks):
        ahead = up(r + 1) if r + 1 < n_chunks else None
        ys = []
        for part, (cw_ref, cb_ref) in enumerate(((cwg_ref, cbg_ref), (cwv_ref, cbv_ref))):
            u = us[part]
            ys.append(cw_ref[0:1, :] * _shift_rows(u, prevs[part], 2) + cw_ref[1:2, :] * _shift_rows(u, prevs[part], 1)
                      + cw_ref[2:3, :] * u + cb_ref[...])
            prevs[part] = u[FF_ROWS - HALO:, :]
        act = (ys[0] * _sigmoid(ys[0]) * ys[1]).astype(jnp.bfloat16)
        rows = slice(r * FF_ROWS, (r + 1) * FF_ROWS)
        o_ref[rows, :] += jnp.dot(act, wd_ref[...], preferred_element_type=jnp.float32)
        us = ahead
    for part in range(2):
        halo_ref[f, part] = prevs[part]


def _cast_up_kernel(*refs, valid_blocks):
    *w_refs, o_ref = refs
    f = pl.program_id(1)
    for j, w_ref in enumerate(w_refs):
        inside = f * len(w_refs) + j < valid_blocks
        o_ref[:, j * LANES:(j + 1) * LANES] = jnp.where(inside, w_ref[...], 0.0).astype(o_ref.dtype)


def cast_up_weights(w_up, layer, fp):
    _, d, two_ff = w_up.shape
    d_ff = two_ff // 2
    assert d_ff % LANES == 0 and fp % FF_TILE == 0
    half_blocks = d_ff // LANES
    per_tile = FF_TILE // LANES

    def in_spec(j):
        return pl.BlockSpec(
            (None, d, LANES),
            lambda p, f: (layer, 0, jnp.minimum(p * half_blocks + f * per_tile + j, 2 * half_blocks - 1)))

    return pl.pallas_call(
        functools.partial(_cast_up_kernel, valid_blocks=half_blocks), grid=(2, fp // FF_TILE),
        in_specs=[in_spec(j) for j in range(per_tile)],
        out_specs=pl.BlockSpec((None, None, d, FF_TILE), lambda p, f: (p, f, 0, 0)),
        out_shape=jax.ShapeDtypeStruct((2, fp // FF_TILE, d, FF_TILE), jnp.bfloat16),
        compiler_params=_params("parallel", "parallel"), name="cast_up_weights",
    )(*([w_up] * per_tile))


def _cast_down_kernel(w_ref, o_ref, *, valid_blocks):
    o_ref[...] = jnp.where(pl.program_id(0) < valid_blocks, w_ref[...], 0.0).astype(o_ref.dtype)


def cast_down_weights(w_down, layer, fp):
    _, d_ff, d = w_down.shape
    assert d_ff % LANES == 0
    blocks = d_ff // LANES
    return pl.pallas_call(
        functools.partial(_cast_down_kernel, valid_blocks=blocks), grid=(fp // LANES,),
        in_specs=[pl.BlockSpec((None, LANES, d), lambda r: (layer, jnp.minimum(r, blocks - 1), 0))],
        out_specs=pl.BlockSpec((LANES, d), lambda r: (r, 0)),
        out_shape=jax.ShapeDtypeStruct((fp, d), jnp.bfloat16),
        compiler_params=_params("parallel"), name="cast_down_weights",
    )(w_down)


def conv_ffn(h, w_gv, cw_g, cw_v, cb_g, cb_v, w_down, x, seq):
    m, d = h.shape
    fp = w_gv.shape[1] * w_gv.shape[3]
    assert w_gv.shape[3] == FF_TILE
    tm, tf = 512, FF_TILE
    nf = fp // tf
    assert CONV_WIDTH - 1 <= HALO and seq % tm == 0 and tm % FF_ROWS == 0

    def cols(rows):
        return pl.BlockSpec((rows, tf), lambda i, f: (0, f))

    def up_half(part):
        return pl.BlockSpec((None, None, d, tf), lambda i, f: (part, f, 0, 0))

    return pl.pallas_call(
        functools.partial(_ffn_kernel, tiles_per_seq=seq // tm), grid=(m // tm, nf),
        in_specs=[pl.BlockSpec((tm, d), lambda i, f: (i, 0)), up_half(0), up_half(1), cols(CONV_WIDTH),
                  cols(CONV_WIDTH), cols(1), cols(1), pl.BlockSpec((tf, d), lambda i, f: (f, 0)),
                  pl.BlockSpec((tm, d), lambda i, f: (i, 0))],
        out_specs=pl.BlockSpec((tm, d), lambda i, f: (i, 0)),
        out_shape=jax.ShapeDtypeStruct((m, d), jnp.float32),
        scratch_shapes=[pltpu.VMEM((nf, 2, HALO, tf), jnp.float32)],
        compiler_params=_params("arbitrary", "arbitrary"), name="conv_ffn",
    )(h, w_gv, w_gv, cw_g, cw_v, cb_g, cb_v, w_down, x)


CAST_BLOCK_BYTES = 4 * 1024 * 1024


def _cast_kernel(w_ref, o_ref):
    o_ref[...] = w_ref[...].astype(o_ref.dtype)


def cast_layer(w, layer):
    _, r, c = w.shape
    tr = r
    while tr * c * 4 > CAST_BLOCK_BYTES and tr % 16 == 0:
        tr //= 2
    assert r % tr == 0
    return pl.pallas_call(
        _cast_kernel, grid=(r // tr,),
        in_specs=[pl.BlockSpec((None, tr, c), lambda i: (layer, i, 0))],
        out_specs=pl.BlockSpec((tr, c), lambda i: (i, 0)),
        out_shape=jax.ShapeDtypeStruct((r, c), jnp.bfloat16),
        compiler_params=_params("parallel"), name="cast_layer",
    )(w)


def _pad_cols(a, width):
    return jnp.pad(a, ((0, 0), (0, width - a.shape[1])))


def kernel(x, mem, positions, ln_mix, w_qkv, qk_gain, w_br_a, w_br_b, w_br_c, w_gate, b_gate, w_o,
           ln_mem_q, ln_mem_kv, wm_q, wm_kv, wm_o, mem_qk_gain, ln_ffn, w_up, conv_w, conv_b, w_down):
    b, s, d = x.shape
    depth = ln_mix.shape[0]
    d_ff = w_down.shape[1]
    fp = -(-d_ff // FF_TILE) * FF_TILE

    tables = rope_tables(positions)
    xf = x.reshape(b * s, d)
    mem2d = mem.reshape(b * mem.shape[1], d)
    h = rmsnorm_bf16(xf, ln_mix[0])
    for l in range(depth):
        gain_cols = jnp.concatenate(
            [jnp.tile(qk_gain[l, 0], N_HEADS_A), jnp.tile(qk_gain[l, 2], N_HEADS_B),
             jnp.tile(qk_gain[l, 1], N_HEADS_A), jnp.tile(qk_gain[l, 3], N_HEADS_B)]).reshape(1, -1)
        qk, rest = qkv_project(h, cast_layer(w_qkv, l), gain_cols, tables)
        qk, rest = qk.reshape(b, s, -1), rest.reshape(b, s, -1)
        gates = gate_project(h, cast_layer(w_gate, l), b_gate[l])
        o_a = dilated_attention(qk, rest)
        o_b = moba_attention(qk, rest)
        o_c = stick_attention(rest)
        merged = merge_branches(o_a, o_b, o_c, gates,
                                cast_layer(w_br_a, l), cast_layer(w_br_b, l), cast_layer(w_br_c, l))
        xf, h = out_project(merged, cast_layer(w_o, l), xf, ln_mem_q[l])

        kv = mem_kv(mem2d, ln_mem_kv[l], cast_layer(wm_kv, l), mem_qk_gain[l, 1])
        kv = kv.reshape(b, mem.shape[1], kv.shape[1])
        xf, h = mem_attention(h, cast_layer(wm_q, l), mem_qk_gain[l, 0], kv, cast_layer(wm_o, l), xf,
                              ln_ffn[l], s)

        cw_g, cw_v = _pad_cols(conv_w[l, :, :d_ff], fp), _pad_cols(conv_w[l, :, d_ff:], fp)
        cb_g = _pad_cols(conv_b[l, :d_ff].reshape(1, d_ff), fp)
        cb_v = _pad_cols(conv_b[l, d_ff:].reshape(1, d_ff), fp)
        xf = conv_ffn(h, cast_up_weights(w_up, l, fp), cw_g, cw_v, cb_g, cb_v,
                      cast_down_weights(w_down, l, fp), xf, s)
        if l + 1 < depth:
            h = rmsnorm_bf16(xf, ln_mix[l + 1])
    return xf.reshape(b, s, d)
```

```python
import functools

import jax
import jax.numpy as jnp
from jax import lax
from jax.experimental import pallas as pl
from jax.experimental.pallas import tpu as pltpu

HEAD_DIM = 128
LANES = 128
DILATIONS = (1, 4, 16)
WINDOW_STEPS = 128
HEADS_PER_GROUP = 2
N_HEADS_A = 6
N_HEADS_B = 6
N_HEADS_C = 4
N_HEADS_MIX = 16
MIX_WIDTH = N_HEADS_MIX * HEAD_DIM
KEY_TILE = 128
MOBA_BLOCK = 256
MOBA_TOPK = 3
MOBA_CHUNK_BLOCKS = 4
MOBA_HEADS = 3
N_HEADS_MEM = 4
ROPE_THETA = 500000.0
ROT_DIM = HEAD_DIM // 4
ROT_HALF = ROT_DIM // 2
CONV_WIDTH = 3
EPS = 1e-6
SCALE = HEAD_DIM ** -0.5
SCALE_LOG2E = SCALE * 1.4426950408889634
FF_TILE = 512
FF_ROWS = 512
HALO = 8
VMEM_LIMIT = 56 * 1024 * 1024

_NT = (((1,), (1,)), ((), ()))


def _params(*sem):
    return pltpu.CompilerParams(dimension_semantics=sem, vmem_limit_bytes=VMEM_LIMIT)


def _rms(y, gain):
    return y * lax.rsqrt(jnp.mean(y * y, axis=-1, keepdims=True) + EPS) * gain


def _sigmoid(y):
    return 1.0 / (1.0 + jnp.exp(-y))


def _rope_table_kernel(pos_ref, inv_ref, cos_ref, sin_lo_ref, sin_hi_ref):
    ang = pos_ref[...].astype(jnp.float32) * inv_ref[...]
    lane = lax.broadcasted_iota(jnp.int32, ang.shape, 1)
    s = jnp.sin(ang)
    cos_ref[...] = jnp.cos(ang)
    sin_lo_ref[...] = jnp.where(lane < ROT_HALF, -s, 0.0)
    sin_hi_ref[...] = jnp.where((lane >= ROT_HALF) & (lane < ROT_DIM), s, 0.0)


def rope_tables(positions):
    m = positions.size
    tm = 1024
    inv = ROPE_THETA ** (-jnp.arange(0, ROT_DIM, 2, dtype=jnp.float32) / ROT_DIM)
    inv_row = jnp.zeros((1, HEAD_DIM), jnp.float32).at[0, :ROT_DIM].set(jnp.concatenate([inv, inv]))
    tab = jax.ShapeDtypeStruct((m, HEAD_DIM), jnp.float32)
    spec = pl.BlockSpec((tm, HEAD_DIM), lambda i: (i, 0))
    return pl.pallas_call(
        _rope_table_kernel, grid=(m // tm,),
        in_specs=[pl.BlockSpec((tm, 1), lambda i: (i, 0)), pl.BlockSpec((1, HEAD_DIM), lambda i: (0, 0))],
        out_specs=[spec, spec, spec], out_shape=[tab, tab, tab],
        compiler_params=_params("parallel"), name="rope_tables",
    )(positions.reshape(m, 1), inv_row)


def _rmsnorm_kernel(x_ref, g_ref, o_ref):
    o_ref[...] = _rms(x_ref[...], g_ref[...]).astype(o_ref.dtype)


def rmsnorm_bf16(x, gain):
    m, d = x.shape
    tm = 512
    return pl.pallas_call(
        _rmsnorm_kernel, grid=(m // tm,),
        in_specs=[pl.BlockSpec((tm, d), lambda i: (i, 0)), pl.BlockSpec((1, d), lambda i: (0, 0))],
        out_specs=pl.BlockSpec((tm, d), lambda i: (i, 0)),
        out_shape=jax.ShapeDtypeStruct((m, d), jnp.bfloat16),
        compiler_params=_params("parallel"), name="rmsnorm",
    )(x, gain.reshape(1, d))


PROJ_TILE = 512
PROJ_ROWS = 256
NORMED_HEADS = N_HEADS_A + N_HEADS_B
QK_K_BLOCK = NORMED_HEADS
REST_KC_BLOCK = N_HEADS_C
REST_V_BLOCK = 2 * N_HEADS_C


def _qk_norm_kernel(h_ref, w_ref, gain_ref, ones_ref, cos_ref, sin_lo_ref, sin_hi_ref, o_ref):
    def project(r):
        return jnp.dot(h_ref[r * PROJ_ROWS:(r + 1) * PROJ_ROWS, :], w_ref[...], preferred_element_type=jnp.float32)

    n_chunks = h_ref.shape[0] // PROJ_ROWS
    ahead = project(0)
    for r in range(n_chunks):
        rows = slice(r * PROJ_ROWS, (r + 1) * PROJ_ROWS)
        acc = ahead
        ahead = project(r + 1) if r + 1 < n_chunks else None
        ss = jnp.dot((acc * acc).astype(jnp.bfloat16), ones_ref[...], preferred_element_type=jnp.float32)
        y = acc * lax.rsqrt(ss * (1.0 / HEAD_DIM) + EPS) * gain_ref[...]
        c, s_lo, s_hi = cos_ref[rows, :], sin_lo_ref[rows, :], sin_hi_ref[rows, :]
        for hd in range(PROJ_TILE // HEAD_DIM):
            sl = slice(hd * HEAD_DIM, (hd + 1) * HEAD_DIM)
            yh = y[:, sl]
            yh = (yh * c + pltpu.roll(yh, HEAD_DIM - ROT_HALF, axis=1) * s_lo
                  + pltpu.roll(yh, ROT_HALF, axis=1) * s_hi)
            o_ref[rows, sl] = yh.astype(o_ref.dtype)


def _plain_proj_kernel(h_ref, w_ref, o_ref):
    o_ref[...] = jnp.dot(h_ref[...], w_ref[...], preferred_element_type=jnp.float32).astype(o_ref.dtype)


def qkv_project(h, w_qkv, gain_cols, tables):
    m, d = h.shape
    tm, tn = 1024, PROJ_TILE
    per_part = MIX_WIDTH // tn
    normed_tiles = NORMED_HEADS * HEAD_DIM // tn
    assert normed_tiles * tn == NORMED_HEADS * HEAD_DIM and normed_tiles + 1 == per_part
    n_out = 2 * normed_tiles * tn
    ones = jnp.kron(jnp.eye(tn // HEAD_DIM, dtype=jnp.float32),
                    jnp.ones((HEAD_DIM, HEAD_DIM), jnp.float32)).astype(jnp.bfloat16)
    tab_spec = pl.BlockSpec((tm, HEAD_DIM), lambda i, j: (i, 0))
    h_spec = pl.BlockSpec((tm, d), lambda i, j: (i, 0))
    out_spec = pl.BlockSpec((tm, tn), lambda i, j: (i, j))
    out_sds = jax.ShapeDtypeStruct((m, n_out), jnp.bfloat16)
    qk = pl.pallas_call(
        _qk_norm_kernel, grid=(m // tm, 2 * normed_tiles),
        in_specs=[h_spec, pl.BlockSpec((d, tn), lambda i, j: (0, j + j // normed_tiles)),
                  pl.BlockSpec((1, tn), lambda i, j: (0, j)), pl.BlockSpec((tn, tn), lambda i, j: (0, 0)),
                  tab_spec, tab_spec, tab_spec],
        out_specs=out_spec, out_shape=out_sds,
        compiler_params=_params("parallel", "arbitrary"), name="qk_norm_project",
    )(h, w_qkv, gain_cols, ones, *tables)
    rest = pl.pallas_call(
        _plain_proj_kernel, grid=(m // tm, 2 + per_part),
        in_specs=[h_spec, pl.BlockSpec(
            (d, tn), lambda i, j: (0, jnp.where(j < 2, normed_tiles + j * per_part, j + 2 * per_part - 2)))],
        out_specs=out_spec, out_shape=out_sds,
        compiler_params=_params("parallel", "arbitrary"), name="plain_project",
    )(h, w_qkv)
    return qk, rest


def _gate_kernel(h_ref, w_ref, b_ref, o_ref):
    acc = jnp.dot(h_ref[...], w_ref[...], preferred_element_type=jnp.float32)
    o_ref[...] = _sigmoid(acc + b_ref[...]).astype(o_ref.dtype)


def gate_project(h, w_gate, b_gate):
    m, d = h.shape
    n = w_gate.shape[1]
    tm, tn = 1024, 2048
    return pl.pallas_call(
        _gate_kernel, grid=(m // tm, n // tn),
        in_specs=[pl.BlockSpec((tm, d), lambda i, j: (i, 0)), pl.BlockSpec((d, tn), lambda i, j: (0, j)),
                  pl.BlockSpec((1, tn), lambda i, j: (0, j))],
        out_specs=pl.BlockSpec((tm, tn), lambda i, j: (i, j)),
        out_shape=jax.ShapeDtypeStruct((m, n), jnp.bfloat16),
        compiler_params=_params("parallel", "arbitrary"), name="gate_project",
    )(h, w_gate, b_gate.reshape(1, n))


def _transpose_values(v_ref, vt_ref):
    tile = vt_ref.shape[-1]

    def one(c, carry):
        st = pl.multiple_of(c * tile, tile)
        vt_ref[c] = v_ref[0, pl.ds(st, tile), :].astype(jnp.float32).T.astype(vt_ref.dtype)
        return carry
    lax.fori_loop(0, vt_ref.shape[0], one, 0)


def _softmax_tile(st, vt_tile, m_ref, l_ref, acc_ref):
    m_old = m_ref[...]
    m_new = jnp.maximum(m_old, jnp.max(st, axis=0, keepdims=True))
    alpha = jnp.exp(m_old - m_new)
    p = jnp.exp(st - m_new)
    l_ref[...] = alpha * l_ref[...] + jnp.sum(p, axis=0, keepdims=True)
    acc_ref[...] = alpha * acc_ref[...] + jnp.dot(vt_tile, p.astype(vt_tile.dtype),
                                                  preferred_element_type=jnp.float32)
    m_ref[...] = m_new


def _softmax_init(m_ref, l_ref, acc_ref):
    m_ref[...] = jnp.full_like(m_ref, -jnp.inf)
    l_ref[...] = jnp.zeros_like(l_ref)
    acc_ref[...] = jnp.zeros_like(acc_ref)


DILATED_ROWS = tuple((d + 1) * KEY_TILE for d in DILATIONS)
DILATED_OFFSETS = tuple(sum(DILATED_ROWS[:g]) for g in range(len(DILATIONS)))


def _dilated_kernel(*refs):
    n_groups, slots = len(DILATIONS), HEADS_PER_GROUP
    per = n_groups * slots
    q_refs, k_refs, v_refs = (refs[i * per:(i + 1) * per] for i in range(3))
    o_ref, vt_ref, bias_ref, ahead_ref, sc_ref = refs[3 * per:]
    neg = jnp.float32(-jnp.inf)

    ahead_rows = max(DILATED_ROWS)
    ahead = (lax.broadcasted_iota(jnp.int32, (ahead_rows, KEY_TILE), 0)
             - lax.broadcasted_iota(jnp.int32, (ahead_rows, KEY_TILE), 1))
    ahead_ref[...] = ahead
    for g, d in enumerate(DILATIONS):
        for j in range(slots):
            _transpose_values(v_refs[g * slots + j], vt_ref.at[g * slots + j])
        sl = slice(DILATED_OFFSETS[g], DILATED_OFFSETS[g] + DILATED_ROWS[g])
        on_grid = jnp.where((ahead[:DILATED_ROWS[g]] & (d - 1)) == 0, 0.0, neg)
        bias_ref[0, sl, :] = on_grid
        bias_ref[1, sl, :] = jnp.where(ahead[:DILATED_ROWS[g]] >= 0, on_grid, neg)

    def query_tile(qi, carry):
        qs = pl.multiple_of(qi * KEY_TILE, KEY_TILE)
        starts = [jnp.maximum(qi - d, 0) for d in DILATIONS]
        ms = []
        for j in range(slots):
            m = jnp.full((1, KEY_TILE), neg, jnp.float32)
            for g, d in enumerate(DILATIONS):
                q = q_refs[g * slots + j][0, pl.ds(qs, KEY_TILE), :]
                which = jnp.where(qi >= d, 1, 0)
                newest = (qi - starts[g]) * KEY_TILE
                st = pl.multiple_of(starts[g] * KEY_TILE, KEY_TILE)
                raw = lax.dot_general(k_refs[g * slots + j][0, pl.ds(st, DILATED_ROWS[g]), :], q, _NT,
                                      preferred_element_type=jnp.float32)
                for a in range(d + 1):
                    r0 = DILATED_OFFSETS[g] + a * KEY_TILE
                    scores = raw[a * KEY_TILE:(a + 1) * KEY_TILE, :] * SCALE_LOG2E
                    scores = scores + bias_ref[which, r0:r0 + KEY_TILE, :]
                    scores = jnp.where(ahead_ref[a * KEY_TILE:(a + 1) * KEY_TILE, :] <= newest, scores, neg)
                    sc_ref[j, r0:r0 + KEY_TILE, :] = scores
                    m = jnp.maximum(m, jnp.max(scores, axis=0, keepdims=True))
            ms.append(m)

        for j in range(slots):
            l = jnp.zeros((1, KEY_TILE), jnp.float32)
            acc = jnp.zeros((HEAD_DIM, KEY_TILE), jnp.float32)
            for g, d in enumerate(DILATIONS):
                for a in range(d + 1):
                    r0 = DILATED_OFFSETS[g] + a * KEY_TILE
                    p = jnp.exp2(sc_ref[j, r0:r0 + KEY_TILE, :] - ms[j])
                    l = l + jnp.sum(p, axis=0, keepdims=True)
                    acc = acc + jnp.dot(vt_ref[g * slots + j, starts[g] + a], p.astype(vt_ref.dtype),
                                        preferred_element_type=jnp.float32)
            o_ref[0, pl.ds(qs, KEY_TILE), j * HEAD_DIM:(j + 1) * HEAD_DIM] = (acc / l).T.astype(o_ref.dtype)
        return carry

    lax.fori_loop(0, o_ref.shape[1] // KEY_TILE, query_tile, 0)


def dilated_attention(qk, rest):
    b, s, _ = qk.shape
    n_tiles = s // KEY_TILE
    assert WINDOW_STEPS == KEY_TILE and all(d & (d - 1) == 0 for d in DILATIONS)
    assert s >= max(DILATED_ROWS)

    heads = len(DILATIONS) * HEADS_PER_GROUP

    def full(first_block):
        return [pl.BlockSpec((1, s, HEAD_DIM), functools.partial(lambda bi, blk: (bi, 0, blk), blk=first_block + h),
                             pipeline_mode=pl.Buffered(1)) for h in range(heads)]

    return pl.pallas_call(
        _dilated_kernel, grid=(b,),
        in_specs=full(0) + full(QK_K_BLOCK) + full(REST_V_BLOCK),
        out_specs=pl.BlockSpec((1, s, HEADS_PER_GROUP * HEAD_DIM), lambda bi: (bi, 0, 0)),
        out_shape=jax.ShapeDtypeStruct((b, s, HEADS_PER_GROUP * HEAD_DIM), jnp.bfloat16),
        scratch_shapes=[pltpu.VMEM((heads, n_tiles, HEAD_DIM, KEY_TILE), jnp.bfloat16),
                        pltpu.VMEM((2, sum(DILATED_ROWS), KEY_TILE), jnp.float32),
                        pltpu.VMEM((max(DILATED_ROWS), KEY_TILE), jnp.int32),
                        pltpu.VMEM((HEADS_PER_GROUP, sum(DILATED_ROWS), KEY_TILE), jnp.float32)],
        compiler_params=_params("parallel"), name="dilated_attention",
    )(*([qk] * (2 * heads) + [rest] * heads)).reshape(b * s, HEADS_PER_GROUP * HEAD_DIM)


def _moba_kernel(*refs, n_blocks):
    hs = range(MOBA_HEADS)
    q_refs, k_refs, v_refs = (refs[i * MOBA_HEADS:(i + 1) * MOBA_HEADS] for i in range(3))
    o_ref, kmean_ref, vt_ref, bias_ref, sc_ref, m_ref, l_ref, acc_ref = refs[3 * MOBA_HEADS:]
    for g in hs:
        for n in range(n_blocks):
            kb = k_refs[g][0, n * MOBA_BLOCK:(n + 1) * MOBA_BLOCK, :].astype(jnp.float32)
            kmean_ref[g, n:n + 1, :] = jnp.sum(kb, axis=0, keepdims=True) / MOBA_BLOCK
        _transpose_values(v_refs[g], vt_ref.at[g])
    neg = jnp.float32(-jnp.inf)
    rows = MOBA_CHUNK_BLOCKS * MOBA_BLOCK

    def query_block(own, carry):
        qs = pl.multiple_of(own * MOBA_BLOCK, MOBA_BLOCK)
        top = own // MOBA_CHUNK_BLOCKS
        qv = [q_refs[g][0, pl.ds(qs, MOBA_BLOCK), :] for g in hs]

        def chunk_scores(g, c, causal):
            st = pl.multiple_of(c * rows, rows)
            s = lax.dot_general(k_refs[g][0, pl.ds(st, rows), :], qv[g], _NT,
                                preferred_element_type=jnp.float32) * SCALE_LOG2E
            s = jnp.concatenate([s[u * MOBA_BLOCK:(u + 1) * MOBA_BLOCK]
                                 + bias_ref[g, pl.ds(c * MOBA_CHUNK_BLOCKS + u, 1), :]
                                 for u in range(MOBA_CHUNK_BLOCKS)], axis=0)
            if causal:
                ahead = lax.broadcasted_iota(jnp.int32, s.shape, 0) - lax.broadcasted_iota(jnp.int32, s.shape, 1)
                s = jnp.where(ahead <= (own - c * MOBA_CHUNK_BLOCKS) * MOBA_BLOCK, s, neg)
            return s

        gates = [lax.dot_general(kmean_ref[g], qv[g].astype(jnp.float32), _NT,
                                 precision=lax.Precision.HIGHEST, preferred_element_type=jnp.float32) for g in hs]
        for g in hs:
            blk = lax.broadcasted_iota(jnp.int32, gates[g].shape, 0)
            gate = jnp.where(blk < own, gates[g], neg)
            rank = jnp.zeros(gate.shape, jnp.int32)
            for mth in range(n_blocks):
                gm = gate[mth:mth + 1, :]
                lower = jnp.where(mth < blk, 1, 0)
                rank = rank + jnp.where(gm > gate, 1, 0) + jnp.where(gm == gate, lower, 0)
            rank = jnp.where(blk < own, rank, MOBA_TOPK)
            bias_ref[g] = jnp.where((rank < MOBA_TOPK) | (blk == own), 0.0, neg)
            m_ref[g] = jnp.full(m_ref.shape[1:], neg, jnp.float32)
            l_ref[g] = jnp.zeros(l_ref.shape[1:], jnp.float32)
            acc_ref[g] = jnp.zeros(acc_ref.shape[1:], jnp.float32)
        firsts = [chunk_scores(g, top, True) for g in hs]
        for g in hs:
            sc_ref[g] = firsts[g]
        first_max = [jnp.max(firsts[g], axis=0, keepdims=True) for g in hs]

        def consume(cur, col_max, produce_next):
            nxt_max = []
            for g in hs:
                m_old = m_ref[g]
                m_new = jnp.maximum(m_old, col_max[g])
                alpha = jnp.exp2(m_old - m_new)
                p = jnp.exp2(sc_ref[g] - m_new)
                if produce_next:
                    nxt = chunk_scores(g, cur - 1, False)
                    sc_ref[g] = nxt
                    nxt_max.append(jnp.max(nxt, axis=0, keepdims=True))
                l_ref[g] = alpha * l_ref[g] + jnp.sum(p, axis=0, keepdims=True)
                p = p.astype(vt_ref.dtype)
                pv = jnp.zeros(acc_ref.shape[1:], jnp.float32)
                for u in range(MOBA_CHUNK_BLOCKS):
                    pv = pv + jnp.dot(vt_ref[g, cur * MOBA_CHUNK_BLOCKS + u],
                                      p[u * MOBA_BLOCK:(u + 1) * MOBA_BLOCK, :], preferred_element_type=jnp.float32)
                acc_ref[g] = alpha * acc_ref[g] + pv
                m_ref[g] = m_new
            return tuple(nxt_max)

        last_max = lax.fori_loop(0, top, lambda i, col_max: consume(top - i, col_max, True), tuple(first_max))
        consume(0, last_max, False)
        for g in hs:
            o_ref[0, pl.ds(qs, MOBA_BLOCK), g * HEAD_DIM:(g + 1) * HEAD_DIM] = (
                (acc_ref[g] / l_ref[g]).T.astype(o_ref.dtype))
        return carry

    lax.fori_loop(0, n_blocks, query_block, 0)


def moba_attention(qk, rest):
    b, s, _ = qk.shape
    n_blocks = s // MOBA_BLOCK
    assert s % MOBA_BLOCK == 0 and n_blocks >= MOBA_TOPK and n_blocks % MOBA_CHUNK_BLOCKS == 0

    g = MOBA_HEADS
    assert N_HEADS_B % g == 0

    def full(first_block, j):
        return pl.BlockSpec((1, s, HEAD_DIM), lambda bi, h: (bi, 0, first_block + N_HEADS_A + h * g + j))

    heads = range(g)
    return pl.pallas_call(
        functools.partial(_moba_kernel, n_blocks=n_blocks), grid=(b, N_HEADS_B // g),
        in_specs=([full(0, j) for j in heads] + [full(QK_K_BLOCK, j) for j in heads]
                  + [full(REST_V_BLOCK, j) for j in heads]),
        out_specs=pl.BlockSpec((1, s, g * HEAD_DIM), lambda bi, h: (bi, 0, h)),
        out_shape=jax.ShapeDtypeStruct((b, s, N_HEADS_B * HEAD_DIM), jnp.bfloat16),
        scratch_shapes=[pltpu.VMEM((g, n_blocks, HEAD_DIM), jnp.float32),
                        pltpu.VMEM((g, n_blocks, HEAD_DIM, MOBA_BLOCK), jnp.bfloat16),
                        pltpu.VMEM((g, n_blocks, MOBA_BLOCK), jnp.float32),
                        pltpu.VMEM((g, MOBA_CHUNK_BLOCKS * MOBA_BLOCK, MOBA_BLOCK), jnp.float32),
                        pltpu.VMEM((g, 1, MOBA_BLOCK), jnp.float32), pltpu.VMEM((g, 1, MOBA_BLOCK), jnp.float32),
                        pltpu.VMEM((g, HEAD_DIM, MOBA_BLOCK), jnp.float32)],
        compiler_params=_params("parallel", "parallel"), name="moba_attention",
    )(*([qk] * (2 * g) + [rest] * g)).reshape(b * s, N_HEADS_B * HEAD_DIM)


STICK_TILE = 256
STICK_CHUNK = 256
STICK_HEADS = 4
STICK_CUTOFF = -104.0


def _stick_kernel(*refs):
    hs = range(STICK_HEADS)
    q_refs, k_refs, v_refs = (refs[i * STICK_HEADS:(i + 1) * STICK_HEADS] for i in range(3))
    o_ref, vt_ref, run_ref, acc_ref = refs[3 * STICK_HEADS:]
    for g in hs:
        _transpose_values(v_refs[g], vt_ref.at[g])
    rows = STICK_CHUNK
    chunks_per_q = STICK_TILE // rows
    sq = lax.broadcasted_iota(jnp.int32, (rows, rows), 0)
    sk = lax.broadcasted_iota(jnp.int32, (rows, rows), 1)
    later = jnp.where(sk > sq, 1.0, 0.0).astype(jnp.bfloat16)
    sub = lax.broadcasted_iota(jnp.int32, (rows, STICK_TILE), 0)
    lane = lax.broadcasted_iota(jnp.int32, (rows, STICK_TILE), 1)

    def tile(qv, kt, valid):
        st = pl.multiple_of(kt * rows, rows)
        zs = [lax.dot_general(k_refs[g][0, pl.ds(st, rows), :], qv[g], _NT,
                              preferred_element_type=jnp.float32) * SCALE for g in hs]
        logs = []
        for g in hs:
            log_1m = -(jnp.maximum(zs[g], 0.0) + jnp.log(1.0 + jnp.exp(-jnp.abs(zs[g]))))
            logs.append(log_1m if valid is None else jnp.where(valid, log_1m, 0.0))
        insides = []
        for g in hs:
            hi = logs[g].astype(jnp.bfloat16)
            lo = (logs[g] - hi.astype(jnp.float32)).astype(jnp.bfloat16)
            insides.append(jnp.dot(later, hi, preferred_element_type=jnp.float32)
                           + jnp.dot(later, lo, preferred_element_type=jnp.float32))
        for g in hs:
            a = jnp.exp(zs[g] + logs[g] + insides[g] + run_ref[g])
            if valid is not None:
                a = jnp.where(valid, a, 0.0)
            acc_ref[g] += jnp.dot(vt_ref[g, kt], a.astype(jnp.bfloat16), preferred_element_type=jnp.float32)
            run_ref[g] += insides[g][0:1, :] + logs[g][0:1, :]

    def alive():
        return (jnp.max(run_ref[...]) > STICK_CUTOFF).astype(jnp.int32)

    def query_tile(qi, carry):
        qs = pl.multiple_of(qi * STICK_TILE, STICK_TILE)
        qv = [q_refs[g][0, pl.ds(qs, STICK_TILE), :] for g in hs]
        run_ref[...] = jnp.zeros_like(run_ref)
        acc_ref[...] = jnp.zeros_like(acc_ref)
        for u in reversed(range(chunks_per_q)):
            tile(qv, qi * chunks_per_q + u, u * rows + sub < lane)

        def earlier(c):
            tile(qv, c[0], None)
            return c[0] - 1, alive()

        lax.while_loop(lambda c: (c[0] >= 0) & (c[1] > 0), earlier, (qi * chunks_per_q - 1, alive()))
        for g in hs:
            o_ref[0, pl.ds(qs, STICK_TILE), g * HEAD_DIM:(g + 1) * HEAD_DIM] = acc_ref[g].T.astype(o_ref.dtype)
        return carry

    lax.fori_loop(0, o_ref.shape[1] // STICK_TILE, query_tile, 0)


def stick_attention(rest):
    b, s, _ = rest.shape
    t = STICK_TILE

    g = STICK_HEADS
    assert N_HEADS_C % g == 0 and STICK_TILE % STICK_CHUNK == 0

    def full(first_block, j):
        return pl.BlockSpec((1, s, HEAD_DIM), lambda bi, h: (bi, 0, first_block + h * g + j))

    heads = range(g)
    return pl.pallas_call(
        _stick_kernel, grid=(b, N_HEADS_C // g),
        in_specs=([full(0, j) for j in heads] + [full(REST_KC_BLOCK, j) for j in heads]
                  + [full(REST_V_BLOCK + NORMED_HEADS, j) for j in heads]),
        out_specs=pl.BlockSpec((1, s, g * HEAD_DIM), lambda bi, h: (bi, 0, h)),
        out_shape=jax.ShapeDtypeStruct((b, s, N_HEADS_C * HEAD_DIM), jnp.bfloat16),
        scratch_shapes=[pltpu.VMEM((g, s // STICK_CHUNK, HEAD_DIM, STICK_CHUNK), jnp.bfloat16),
                        pltpu.VMEM((g, 1, t), jnp.float32), pltpu.VMEM((g, HEAD_DIM, t), jnp.float32)],
        compiler_params=_params("parallel", "parallel"), name="stick_attention",
    )(*([rest] * (3 * g))).reshape(b * s, N_HEADS_C * HEAD_DIM)


def _merge_kernel(oa_ref, ob_ref, oc_ref, g_ref, wa_ref, wb_ref, wc_ref, out_ref):
    d = out_ref.shape[1]
    ya = jnp.dot(oa_ref[...], wa_ref[...], preferred_element_type=jnp.float32)
    yb = jnp.dot(ob_ref[...], wb_ref[...], preferred_element_type=jnp.float32)
    yc = jnp.dot(oc_ref[...], wc_ref[...], preferred_element_type=jnp.float32)
    merged = (g_ref[:, 0:d].astype(jnp.float32) * ya + g_ref[:, d:2 * d].astype(jnp.float32) * yb
              + g_ref[:, 2 * d:3 * d].astype(jnp.float32) * yc)
    out_ref[...] = merged.astype(out_ref.dtype)


def merge_branches(o_a, o_b, o_c, gates, w_a, w_b, w_c):
    m = o_b.shape[0]
    d = w_a.shape[1]
    tm = 512

    def rows(a):
        return pl.BlockSpec((tm, a.shape[1]), lambda i: (i, 0))

    def whole(w):
        return pl.BlockSpec(w.shape, lambda i: (0, 0))

    return pl.pallas_call(
        _merge_kernel, grid=(m // tm,),
        in_specs=[rows(o_a), rows(o_b), rows(o_c), rows(gates), whole(w_a), whole(w_b), whole(w_c)],
        out_specs=pl.BlockSpec((tm, d), lambda i: (i, 0)), out_shape=jax.ShapeDtypeStruct((m, d), jnp.bfloat16),
        compiler_params=_params("parallel"), name="merge_branches",
    )(o_a, o_b, o_c, gates, w_a, w_b, w_c)


def _out_proj_kernel(a_ref, w_ref, x_ref, g_ref, xo_ref, ho_ref):
    xn = x_ref[...] + jnp.dot(a_ref[...], w_ref[...], preferred_element_type=jnp.float32)
    xo_ref[...] = xn
    ho_ref[...] = _rms(xn, g_ref[...]).astype(ho_ref.dtype)


def out_project(a, w, x, next_gain):
    m, k = a.shape
    d = w.shape[1]
    tm = 512
    return pl.pallas_call(
        _out_proj_kernel, grid=(m // tm,),
        in_specs=[pl.BlockSpec((tm, k), lambda i: (i, 0)), pl.BlockSpec((k, d), lambda i: (0, 0)),
                  pl.BlockSpec((tm, d), lambda i: (i, 0)), pl.BlockSpec((1, d), lambda i: (0, 0))],
        out_specs=[pl.BlockSpec((tm, d), lambda i: (i, 0)), pl.BlockSpec((tm, d), lambda i: (i, 0))],
        out_shape=[jax.ShapeDtypeStruct((m, d), jnp.float32), jax.ShapeDtypeStruct((m, d), jnp.bfloat16)],
        compiler_params=_params("parallel"), name="out_project",
    )(a, w, x, next_gain.reshape(1, d))


def _mem_kv_kernel(mem_ref, ln_ref, w_ref, gk_ref, kv_ref):
    hm = _rms(mem_ref[...], ln_ref[...]).astype(jnp.bfloat16)
    kv = jnp.dot(hm, w_ref[...], preferred_element_type=jnp.float32)
    half = kv.shape[1] // 2
    for hd in range(N_HEADS_MEM):
        sl = slice(hd * HEAD_DIM, (hd + 1) * HEAD_DIM)
        kv_ref[:, sl] = _rms(kv[:, sl], gk_ref[...]).astype(kv_ref.dtype)
    kv_ref[:, half:] = kv[:, half:].astype(kv_ref.dtype)


def mem_kv(mem2d, ln, wm_kv, gain_k):
    n, d = mem2d.shape
    w = wm_kv.shape[1]
    return pl.pallas_call(
        _mem_kv_kernel, grid=(1,),
        in_specs=[pl.BlockSpec((n, d), lambda i: (0, 0)), pl.BlockSpec((1, d), lambda i: (0, 0)),
                  pl.BlockSpec((d, w), lambda i: (0, 0)), pl.BlockSpec((1, HEAD_DIM), lambda i: (0, 0))],
        out_specs=pl.BlockSpec((n, w), lambda i: (0, 0)),
        out_shape=jax.ShapeDtypeStruct((n, w), jnp.bfloat16),
        compiler_params=_params("arbitrary"), name="mem_kv",
    )(mem2d, ln.reshape(1, d), wm_kv, gain_k.reshape(1, HEAD_DIM))


def _mem_attn_kernel(h_ref, wq_ref, gq_ref, kv_ref, wo_ref, x_ref, g_ref, xo_ref, ho_ref):
    qf = jnp.dot(h_ref[...], wq_ref[...], preferred_element_type=jnp.float32)
    half = kv_ref.shape[2] // 2
    outs = []
    for hd in range(N_HEADS_MEM):
        sl = slice(hd * HEAD_DIM, (hd + 1) * HEAD_DIM)
        qh = _rms(qf[:, sl], gq_ref[...]).astype(jnp.bfloat16)
        s = lax.dot_general(qh, kv_ref[0, :, sl], _NT, preferred_element_type=jnp.float32) * SCALE
        e = jnp.exp(s - jnp.max(s, axis=1, keepdims=True))
        vh = kv_ref[0, :, half + hd * HEAD_DIM:half + (hd + 1) * HEAD_DIM]
        o = jnp.dot(e.astype(jnp.bfloat16), vh, preferred_element_type=jnp.float32)
        outs.append((o / jnp.sum(e, axis=1, keepdims=True)).astype(jnp.bfloat16))
    o_all = jnp.concatenate(outs, axis=1)
    xn = x_ref[...] + jnp.dot(o_all, wo_ref[...], preferred_element_type=jnp.float32)
    xo_ref[...] = xn
    ho_ref[...] = _rms(xn, g_ref[...]).astype(ho_ref.dtype)


def mem_attention(h, wm_q, gain_q, kv, wm_o, x, next_gain, seq):
    m, d = h.shape
    wq = wm_q.shape[1]
    tm = 512
    per_batch = seq // tm
    return pl.pallas_call(
        _mem_attn_kernel, grid=(m // tm,),
        in_specs=[pl.BlockSpec((tm, d), lambda i: (i, 0)), pl.BlockSpec((d, wq), lambda i: (0, 0)),
                  pl.BlockSpec((1, HEAD_DIM), lambda i: (0, 0)),
                  pl.BlockSpec((1,) + kv.shape[1:], lambda i: (i // per_batch, 0, 0)),
                  pl.BlockSpec((wq, d), lambda i: (0, 0)), pl.BlockSpec((tm, d), lambda i: (i, 0)),
                  pl.BlockSpec((1, d), lambda i: (0, 0))],
        out_specs=[pl.BlockSpec((tm, d), lambda i: (i, 0)), pl.BlockSpec((tm, d), lambda i: (i, 0))],
        out_shape=[jax.ShapeDtypeStruct((m, d), jnp.float32), jax.ShapeDtypeStruct((m, d), jnp.bfloat16)],
        compiler_params=_params("parallel"), name="mem_attention",
    )(h, wm_q, gain_q.reshape(1, HEAD_DIM), kv, wm_o, x, next_gain.reshape(1, d))


def _shift_rows(u, prev, k):
    rolled = pltpu.roll(u, k, axis=0)
    row = lax.broadcasted_iota(jnp.int32, prev.shape, 0)
    head = jnp.where(row < k, pltpu.roll(prev, k, axis=0), rolled[:HALO])
    return jnp.concatenate([head, rolled[HALO:]], axis=0)


def _ffn_kernel(h_ref, wg_ref, wv_ref, cwg_ref, cwv_ref, cbg_ref, cbv_ref, wd_ref, x_ref, o_ref,
                halo_ref, *, tiles_per_seq):
    i, f = pl.program_id(0), pl.program_id(1)
    tm = h_ref.shape[0]
    keep = jnp.where(i % tiles_per_seq == 0, 0.0, 1.0)
    prevs = [jnp.where(keep > 0.0, halo_ref[f, part], 0.0) for part in range(2)]

    @pl.when(f == 0)
    def _():
        o_ref[...] = x_ref[...]

    def up(r):
        rows = slice(r * FF_ROWS, (r + 1) * FF_ROWS)
        return [jnp.dot(h_ref[rows, :], w_ref[...], preferred_element_type=jnp.float32) for w_ref in (wg_ref, wv_ref)]

    n_chunks = tm // FF_ROWS
    us = up(0)
    for r in range(n_chunks):
        ahead = up(r + 1) if r + 1 < n_chunks else None
        ys = []
        for part, (cw_ref, cb_ref) in enumerate(((cwg_ref, cbg_ref), (cwv_ref, cbv_ref))):
            u = us[part]
            ys.append(cw_ref[0:1, :] * _shift_rows(u, prevs[part], 2) + cw_ref[1:2, :] * _shift_rows(u, prevs[part], 1)
                      + cw_ref[2:3, :] * u + cb_ref[...])
            prevs[part] = u[FF_ROWS - HALO:, :]
        act = (ys[0] * _sigmoid(ys[0]) * ys[1]).astype(jnp.bfloat16)
        rows = slice(r * FF_ROWS, (r + 1) * FF_ROWS)
        o_ref[rows, :] += jnp.dot(act, wd_ref[...], preferred_element_type=jnp.float32)
        us = ahead
    for part in range(2):
        halo_ref[f, part] = prevs[part]


def _cast_up_kernel(*refs, valid_blocks):
    *w_refs, o_ref = refs
    f = pl.program_id(1)
    for j, w_ref in enumerate(w_refs):
        inside = f * len(w_refs) + j < valid_blocks
        o_ref[:, j * LANES:(j + 1) * LANES] = jnp.where(inside, w_ref[...], 0.0).astype(o_ref.dtype)


def cast_up_weights(w_up, layer, fp):
    _, d, two_ff = w_up.shape
    d_ff = two_ff // 2
    assert d_ff % LANES == 0 and fp % FF_TILE == 0
    half_blocks = d_ff // LANES
    per_tile = FF_TILE // LANES

    def in_spec(j):
        return pl.BlockSpec(
            (None, d, LANES),
            lambda p, f: (layer, 0, jnp.minimum(p * half_blocks + f * per_tile + j, 2 * half_blocks - 1)))

    return pl.pallas_call(
        functools.partial(_cast_up_kernel, valid_blocks=half_blocks), grid=(2, fp // FF_TILE),
        in_specs=[in_spec(j) for j in range(per_tile)],
        out_specs=pl.BlockSpec((None, None, d, FF_TILE), lambda p, f: (p, f, 0, 0)),
        out_shape=jax.ShapeDtypeStruct((2, fp // FF_TILE, d, FF_TILE), jnp.bfloat16),
        compiler_params=_params("parallel", "parallel"), name="cast_up_weights",
    )(*([w_up] * per_tile))


def _cast_down_kernel(w_ref, o_ref, *, valid_blocks):
    o_ref[...] = jnp.where(pl.program_id(0) < valid_blocks, w_ref[...], 0.0).astype(o_ref.dtype)


def cast_down_weights(w_down, layer, fp):
    _, d_ff, d = w_down.shape
    assert d_ff % LANES == 0
    blocks = d_ff // LANES
    return pl.pallas_call(
        functools.partial(_cast_down_kernel, valid_blocks=blocks), grid=(fp // LANES,),
        in_specs=[pl.BlockSpec((None, LANES, d), lambda r: (layer, jnp.minimum(r, blocks - 1), 0))],
        out_specs=pl.BlockSpec((LANES, d), lambda r: (r, 0)),
        out_shape=jax.ShapeDtypeStruct((fp, d), jnp.bfloat16),
        compiler_params=_params("parallel"), name="cast_down_weights",
    )(w_down)


def conv_ffn(h, w_gv, cw_g, cw_v, cb_g, cb_v, w_down, x, seq):
    m, d = h.shape
    fp = w_gv.shape[1] * w_gv.shape[3]
    assert w_gv.shape[3] == FF_TILE
    tm, tf = 1024, FF_TILE
    nf = fp // tf
    assert CONV_WIDTH - 1 <= HALO and seq % tm == 0 and tm % FF_ROWS == 0

    def cols(rows):
        return pl.BlockSpec((rows, tf), lambda i, f: (0, f))

    def up_half(part):
        return pl.BlockSpec((None, None, d, tf), lambda i, f: (part, f, 0, 0))

    return pl.pallas_call(
        functools.partial(_ffn_kernel, tiles_per_seq=seq // tm), grid=(m // tm, nf),
        in_specs=[pl.BlockSpec((tm, d), lambda i, f: (i, 0)), up_half(0), up_half(1), cols(CONV_WIDTH),
                  cols(CONV_WIDTH), cols(1), cols(1), pl.BlockSpec((tf, d), lambda i, f: (f, 0)),
                  pl.BlockSpec((tm, d), lambda i, f: (i, 0), pipeline_mode=pl.Buffered(1))],
        out_specs=pl.BlockSpec((tm, d), lambda i, f: (i, 0)),
        out_shape=jax.ShapeDtypeStruct((m, d), jnp.float32),
        scratch_shapes=[pltpu.VMEM((nf, 2, HALO, tf), jnp.float32)],
        compiler_params=_params("arbitrary", "arbitrary"), name="conv_ffn",
    )(h, w_gv, w_gv, cw_g, cw_v, cb_g, cb_v, w_down, x)


CAST_BLOCK_BYTES = 4 * 1024 * 1024


def _cast_kernel(w_ref, o_ref):
    o_ref[...] = w_ref[...].astype(o_ref.dtype)


def cast_layer(w, layer):
    _, r, c = w.shape
    tr = r
    while tr * c * 4 > CAST_BLOCK_BYTES and tr % 16 == 0:
        tr //= 2
    assert r % tr == 0
    return pl.pallas_call(
        _cast_kernel, grid=(r // tr,),
        in_specs=[pl.BlockSpec((None, tr, c), lambda i: (layer, i, 0))],
        out_specs=pl.BlockSpec((tr, c), lambda i: (i, 0)),
        out_shape=jax.ShapeDtypeStruct((r, c), jnp.bfloat16),
        compiler_params=_params("parallel"), name="cast_layer",
    )(w)


def _pad_cols(a, width):
    return jnp.pad(a, ((0, 0), (0, width - a.shape[1])))


def kernel(x, mem, positions, ln_mix, w_qkv, qk_gain, w_br_a, w_br_b, w_br_c, w_gate, b_gate, w_o,
           ln_mem_q, ln_mem_kv, wm_q, wm_kv, wm_o, mem_qk_gain, ln_ffn, w_up, conv_w, conv_b, w_down):
    b, s, d = x.shape
    depth = ln_mix.shape[0]
    d_ff = w_down.shape[1]
    fp = -(-d_ff // FF_TILE) * FF_TILE

    tables = rope_tables(positions)
    xf = x.reshape(b * s, d)
    mem2d = mem.reshape(b * mem.shape[1], d)
    h = rmsnorm_bf16(xf, ln_mix[0])
    for l in range(depth):
        gain_cols = jnp.concatenate(
            [jnp.tile(qk_gain[l, 0], N_HEADS_A), jnp.tile(qk_gain[l, 2], N_HEADS_B),
             jnp.tile(qk_gain[l, 1], N_HEADS_A), jnp.tile(qk_gain[l, 3], N_HEADS_B)]).reshape(1, -1)
        qk, rest = qkv_project(h, cast_layer(w_qkv, l), gain_cols, tables)
        qk, rest = qk.reshape(b, s, -1), rest.reshape(b, s, -1)
        gates = gate_project(h, cast_layer(w_gate, l), b_gate[l])
        o_a = dilated_attention(qk, rest)
        o_b = moba_attention(qk, rest)
        o_c = stick_attention(rest)
        merged = merge_branches(o_a, o_b, o_c, gates,
                                cast_layer(w_br_a, l), cast_layer(w_br_b, l), cast_layer(w_br_c, l))
        xf, h = out_project(merged, cast_layer(w_o, l), xf, ln_mem_q[l])

        kv = mem_kv(mem2d, ln_mem_kv[l], cast_layer(wm_kv, l), mem_qk_gain[l, 1])
        kv = kv.reshape(b, mem.shape[1], kv.shape[1])
        xf, h = mem_attention(h, cast_layer(wm_q, l), mem_qk_gain[l, 0], kv, cast_layer(wm_o, l), xf,
                              ln_ffn[l], s)

        cw_g, cw_v = _pad_cols(conv_w[l, :, :d_ff], fp), _pad_cols(conv_w[l, :, d_ff:], fp)
        cb_g = _pad_cols(conv_b[l, :d_ff].reshape(1, d_ff), fp)
        cb_v = _pad_cols(conv_b[l, d_ff:].reshape(1, d_ff), fp)
        xf = conv_ffn(h, cast_up_weights(w_up, l, fp), cw_g, cw_v, cb_g, cb_v,
                      cast_down_weights(w_down, l, fp), xf, s)
        if l + 1 < depth:
            h = rmsnorm_bf16(xf, ln_mix[l + 1])
    return xf.reshape(b, s, d)
```

```python
import functools

import jax
import jax.numpy as jnp
from jax import lax
from jax.experimental import pallas as pl
from jax.experimental.pallas import tpu as pltpu

HEAD_DIM = 128
LANES = 128
DILATIONS = (1, 4, 16)
WINDOW_STEPS = 128
HEADS_PER_GROUP = 2
N_HEADS_A = 6
N_HEADS_B = 6
N_HEADS_C = 4
N_HEADS_MIX = 16
MIX_WIDTH = N_HEADS_MIX * HEAD_DIM
KEY_TILE = 128
MOBA_BLOCK = 256
MOBA_TOPK = 3
MOBA_CHUNK_BLOCKS = 2
MOBA_HEADS = 3
N_HEADS_MEM = 4
ROPE_THETA = 500000.0
ROT_DIM = HEAD_DIM // 4
ROT_HALF = ROT_DIM // 2
CONV_WIDTH = 3
EPS = 1e-6
SCALE = HEAD_DIM ** -0.5
SCALE_LOG2E = SCALE * 1.4426950408889634
FF_TILE = 512
FF_ROWS = 512
HALO = 8
VMEM_LIMIT = 60 * 1024 * 1024

_NT = (((1,), (1,)), ((), ()))


def _params(*sem):
    return pltpu.CompilerParams(dimension_semantics=sem, vmem_limit_bytes=VMEM_LIMIT)


def _rms(y, gain):
    return y * lax.rsqrt(jnp.mean(y * y, axis=-1, keepdims=True) + EPS) * gain


def _sigmoid(y):
    return 1.0 / (1.0 + jnp.exp(-y))


def _rope_table_kernel(pos_ref, inv_ref, cos_ref, sin_lo_ref, sin_hi_ref):
    ang = pos_ref[...].astype(jnp.float32) * inv_ref[...]
    lane = lax.broadcasted_iota(jnp.int32, ang.shape, 1)
    s = jnp.sin(ang)
    cos_ref[...] = jnp.cos(ang)
    sin_lo_ref[...] = jnp.where(lane < ROT_HALF, -s, 0.0)
    sin_hi_ref[...] = jnp.where((lane >= ROT_HALF) & (lane < ROT_DIM), s, 0.0)


def rope_tables(positions):
    m = positions.size
    tm = 1024
    inv = ROPE_THETA ** (-jnp.arange(0, ROT_DIM, 2, dtype=jnp.float32) / ROT_DIM)
    inv_row = jnp.zeros((1, HEAD_DIM), jnp.float32).at[0, :ROT_DIM].set(jnp.concatenate([inv, inv]))
    tab = jax.ShapeDtypeStruct((m, HEAD_DIM), jnp.float32)
    spec = pl.BlockSpec((tm, HEAD_DIM), lambda i: (i, 0))
    return pl.pallas_call(
        _rope_table_kernel, grid=(m // tm,),
        in_specs=[pl.BlockSpec((tm, 1), lambda i: (i, 0)), pl.BlockSpec((1, HEAD_DIM), lambda i: (0, 0))],
        out_specs=[spec, spec, spec], out_shape=[tab, tab, tab],
        compiler_params=_params("parallel"), name="rope_tables",
    )(positions.reshape(m, 1), inv_row)


def _rmsnorm_kernel(x_ref, g_ref, o_ref):
    o_ref[...] = _rms(x_ref[...], g_ref[...]).astype(o_ref.dtype)


def rmsnorm_bf16(x, gain):
    m, d = x.shape
    tm = 512
    return pl.pallas_call(
        _rmsnorm_kernel, grid=(m // tm,),
        in_specs=[pl.BlockSpec((tm, d), lambda i: (i, 0)), pl.BlockSpec((1, d), lambda i: (0, 0))],
        out_specs=pl.BlockSpec((tm, d), lambda i: (i, 0)),
        out_shape=jax.ShapeDtypeStruct((m, d), jnp.bfloat16),
        compiler_params=_params("parallel"), name="rmsnorm",
    )(x, gain.reshape(1, d))


PROJ_TILE = 512
PROJ_ROWS = 256
NORMED_HEADS = N_HEADS_A + N_HEADS_B
QK_K_BLOCK = NORMED_HEADS
REST_KC_BLOCK = N_HEADS_C
REST_V_BLOCK = 2 * N_HEADS_C


def _qk_norm_kernel(h_ref, w_ref, gain_ref, ones_ref, cos_ref, sin_lo_ref, sin_hi_ref, o_ref):
    def project(r):
        return jnp.dot(h_ref[r * PROJ_ROWS:(r + 1) * PROJ_ROWS, :], w_ref[...], preferred_element_type=jnp.float32)

    n_chunks = h_ref.shape[0] // PROJ_ROWS
    ahead = project(0)
    for r in range(n_chunks):
        rows = slice(r * PROJ_ROWS, (r + 1) * PROJ_ROWS)
        acc = ahead
        ahead = project(r + 1) if r + 1 < n_chunks else None
        ss = jnp.dot((acc * acc).astype(jnp.bfloat16), ones_ref[...], preferred_element_type=jnp.float32)
        y = acc * lax.rsqrt(ss * (1.0 / HEAD_DIM) + EPS) * gain_ref[...]
        c, s_lo, s_hi = cos_ref[rows, :], sin_lo_ref[rows, :], sin_hi_ref[rows, :]
        for hd in range(PROJ_TILE // HEAD_DIM):
            sl = slice(hd * HEAD_DIM, (hd + 1) * HEAD_DIM)
            yh = y[:, sl]
            yh = (yh * c + pltpu.roll(yh, HEAD_DIM - ROT_HALF, axis=1) * s_lo
                  + pltpu.roll(yh, ROT_HALF, axis=1) * s_hi)
            o_ref[rows, sl] = yh.astype(o_ref.dtype)


def _plain_proj_kernel(h_ref, w_ref, o_ref):
    o_ref[...] = jnp.dot(h_ref[...], w_ref[...], preferred_element_type=jnp.float32).astype(o_ref.dtype)


def qkv_project(h, w_qkv, gain_cols, tables):
    m, d = h.shape
    tm, tn = 1024, PROJ_TILE
    per_part = MIX_WIDTH // tn
    normed_tiles = NORMED_HEADS * HEAD_DIM // tn
    assert normed_tiles * tn == NORMED_HEADS * HEAD_DIM and normed_tiles + 1 == per_part
    n_out = 2 * normed_tiles * tn
    ones = jnp.kron(jnp.eye(tn // HEAD_DIM, dtype=jnp.float32),
                    jnp.ones((HEAD_DIM, HEAD_DIM), jnp.float32)).astype(jnp.bfloat16)
    tab_spec = pl.BlockSpec((tm, HEAD_DIM), lambda i, j: (i, 0))
    h_spec = pl.BlockSpec((tm, d), lambda i, j: (i, 0))
    out_spec = pl.BlockSpec((tm, tn), lambda i, j: (i, j))
    out_sds = jax.ShapeDtypeStruct((m, n_out), jnp.bfloat16)
    qk = pl.pallas_call(
        _qk_norm_kernel, grid=(m // tm, 2 * normed_tiles),
        in_specs=[h_spec, pl.BlockSpec((d, tn), lambda i, j: (0, j + j // normed_tiles)),
                  pl.BlockSpec((1, tn), lambda i, j: (0, j)), pl.BlockSpec((tn, tn), lambda i, j: (0, 0)),
                  tab_spec, tab_spec, tab_spec],
        out_specs=out_spec, out_shape=out_sds,
        compiler_params=_params("parallel", "arbitrary"), name="qk_norm_project",
    )(h, w_qkv, gain_cols, ones, *tables)
    rest = pl.pallas_call(
        _plain_proj_kernel, grid=(m // tm, 2 + per_part),
        in_specs=[h_spec, pl.BlockSpec(
            (d, tn), lambda i, j: (0, jnp.where(j < 2, normed_tiles + j * per_part, j + 2 * per_part - 2)))],
        out_specs=out_spec, out_shape=out_sds,
        compiler_params=_params("parallel", "arbitrary"), name="plain_project",
    )(h, w_qkv)
    return qk, rest


def _gate_kernel(h_ref, w_ref, b_ref, o_ref):
    acc = jnp.dot(h_ref[...], w_ref[...], preferred_element_type=jnp.float32)
    o_ref[...] = _sigmoid(acc + b_ref[...]).astype(o_ref.dtype)


def gate_project(h, w_gate, b_gate):
    m, d = h.shape
    n = w_gate.shape[1]
    tm, tn = 1024, 2048
    return pl.pallas_call(
        _gate_kernel, grid=(m // tm, n // tn),
        in_specs=[pl.BlockSpec((tm, d), lambda i, j: (i, 0)), pl.BlockSpec((d, tn), lambda i, j: (0, j)),
                  pl.BlockSpec((1, tn), lambda i, j: (0, j))],
        out_specs=pl.BlockSpec((tm, tn), lambda i, j: (i, j)),
        out_shape=jax.ShapeDtypeStruct((m, n), jnp.bfloat16),
        compiler_params=_params("parallel", "arbitrary"), name="gate_project",
    )(h, w_gate, b_gate.reshape(1, n))


def _transpose_values(v_ref, vt_ref):
    tile = vt_ref.shape[-1]

    def one(c, carry):
        st = pl.multiple_of(c * tile, tile)
        vt_ref[c] = v_ref[0, pl.ds(st, tile), :].astype(jnp.float32).T.astype(vt_ref.dtype)
        return carry
    lax.fori_loop(0, vt_ref.shape[0], one, 0)


def _softmax_tile(st, vt_tile, m_ref, l_ref, acc_ref):
    m_old = m_ref[...]
    m_new = jnp.maximum(m_old, jnp.max(st, axis=0, keepdims=True))
    alpha = jnp.exp(m_old - m_new)
    p = jnp.exp(st - m_new)
    l_ref[...] = alpha * l_ref[...] + jnp.sum(p, axis=0, keepdims=True)
    acc_ref[...] = alpha * acc_ref[...] + jnp.dot(vt_tile, p.astype(vt_tile.dtype),
                                                  preferred_element_type=jnp.float32)
    m_ref[...] = m_new


def _softmax_init(m_ref, l_ref, acc_ref):
    m_ref[...] = jnp.full_like(m_ref, -jnp.inf)
    l_ref[...] = jnp.zeros_like(l_ref)
    acc_ref[...] = jnp.zeros_like(acc_ref)


DILATED_ROWS = tuple((d + 1) * KEY_TILE for d in DILATIONS)
DILATED_OFFSETS = tuple(sum(DILATED_ROWS[:g]) for g in range(len(DILATIONS)))


def _dilated_kernel(*refs):
    n_groups, slots = len(DILATIONS), HEADS_PER_GROUP
    per = n_groups * slots
    q_refs, k_refs, v_refs = (refs[i * per:(i + 1) * per] for i in range(3))
    o_ref, vt_ref, bias_ref, ahead_ref, sc_ref = refs[3 * per:]
    neg = jnp.float32(-jnp.inf)

    ahead_rows = max(DILATED_ROWS)
    ahead = (lax.broadcasted_iota(jnp.int32, (ahead_rows, KEY_TILE), 0)
             - lax.broadcasted_iota(jnp.int32, (ahead_rows, KEY_TILE), 1))
    ahead_ref[...] = ahead
    for g, d in enumerate(DILATIONS):
        for j in range(slots):
            _transpose_values(v_refs[g * slots + j], vt_ref.at[g * slots + j])
        sl = slice(DILATED_OFFSETS[g], DILATED_OFFSETS[g] + DILATED_ROWS[g])
        on_grid = jnp.where((ahead[:DILATED_ROWS[g]] & (d - 1)) == 0, 0.0, neg)
        bias_ref[0, sl, :] = on_grid
        bias_ref[1, sl, :] = jnp.where(ahead[:DILATED_ROWS[g]] >= 0, on_grid, neg)

    def query_tile(qi, carry):
        qs = pl.multiple_of(qi * KEY_TILE, KEY_TILE)
        starts = [jnp.maximum(qi - d, 0) for d in DILATIONS]
        ms = []
        for j in range(slots):
            m = jnp.full((1, KEY_TILE), neg, jnp.float32)
            for g, d in enumerate(DILATIONS):
                q = q_refs[g * slots + j][0, pl.ds(qs, KEY_TILE), :]
                which = jnp.where(qi >= d, 1, 0)
                newest = (qi - starts[g]) * KEY_TILE
                st = pl.multiple_of(starts[g] * KEY_TILE, KEY_TILE)
                raw = lax.dot_general(k_refs[g * slots + j][0, pl.ds(st, DILATED_ROWS[g]), :], q, _NT,
                                      preferred_element_type=jnp.float32)
                for a in range(d + 1):
                    r0 = DILATED_OFFSETS[g] + a * KEY_TILE
                    scores = raw[a * KEY_TILE:(a + 1) * KEY_TILE, :] * SCALE_LOG2E
                    scores = scores + bias_ref[which, r0:r0 + KEY_TILE, :]
                    scores = jnp.where(ahead_ref[a * KEY_TILE:(a + 1) * KEY_TILE, :] <= newest, scores, neg)
                    sc_ref[j, r0:r0 + KEY_TILE, :] = scores
                    m = jnp.maximum(m, jnp.max(scores, axis=0, keepdims=True))
            ms.append(m)

        for j in range(slots):
            l = jnp.zeros((1, KEY_TILE), jnp.float32)
            acc = jnp.zeros((HEAD_DIM, KEY_TILE), jnp.float32)
            for g, d in enumerate(DILATIONS):
                for a in range(d + 1):
                    r0 = DILATED_OFFSETS[g] + a * KEY_TILE
                    p = jnp.exp2(sc_ref[j, r0:r0 + KEY_TILE, :] - ms[j])
                    l = l + jnp.sum(p, axis=0, keepdims=True)
                    acc = acc + jnp.dot(vt_ref[g * slots + j, starts[g] + a], p.astype(vt_ref.dtype),
                                        preferred_element_type=jnp.float32)
            o_ref[0, pl.ds(qs, KEY_TILE), j * HEAD_DIM:(j + 1) * HEAD_DIM] = (acc / l).T.astype(o_ref.dtype)
        return carry

    lax.fori_loop(0, o_ref.shape[1] // KEY_TILE, query_tile, 0)


def dilated_attention(qk, rest):
    b, s, _ = qk.shape
    n_tiles = s // KEY_TILE
    assert WINDOW_STEPS == KEY_TILE and all(d & (d - 1) == 0 for d in DILATIONS)
    assert s >= max(DILATED_ROWS)

    heads = len(DILATIONS) * HEADS_PER_GROUP

    def full(first_block):
        return [pl.BlockSpec((1, s, HEAD_DIM), functools.partial(lambda bi, blk: (bi, 0, blk), blk=first_block + h),
                             pipeline_mode=pl.Buffered(1)) for h in range(heads)]

    return pl.pallas_call(
        _dilated_kernel, grid=(b,),
        in_specs=full(0) + full(QK_K_BLOCK) + full(REST_V_BLOCK),
        out_specs=pl.BlockSpec((1, s, HEADS_PER_GROUP * HEAD_DIM), lambda bi: (bi, 0, 0)),
        out_shape=jax.ShapeDtypeStruct((b, s, HEADS_PER_GROUP * HEAD_DIM), jnp.bfloat16),
        scratch_shapes=[pltpu.VMEM((heads, n_tiles, HEAD_DIM, KEY_TILE), jnp.bfloat16),
                        pltpu.VMEM((2, sum(DILATED_ROWS), KEY_TILE), jnp.float32),
                        pltpu.VMEM((max(DILATED_ROWS), KEY_TILE), jnp.int32),
                        pltpu.VMEM((HEADS_PER_GROUP, sum(DILATED_ROWS), KEY_TILE), jnp.float32)],
        compiler_params=_params("parallel"), name="dilated_attention",
    )(*([qk] * (2 * heads) + [rest] * heads)).reshape(b * s, HEADS_PER_GROUP * HEAD_DIM)


def _moba_kernel(*refs, n_blocks):
    hs = range(MOBA_HEADS)
    q_refs, k_refs, v_refs = (refs[i * MOBA_HEADS:(i + 1) * MOBA_HEADS] for i in range(3))
    o_ref, kmean_ref, vt_ref, bias_ref, sc_ref, m_ref, l_ref, acc_ref = refs[3 * MOBA_HEADS:]
    for g in hs:
        for n in range(n_blocks):
            kb = k_refs[g][0, n * MOBA_BLOCK:(n + 1) * MOBA_BLOCK, :].astype(jnp.float32)
            kmean_ref[g, n:n + 1, :] = jnp.sum(kb, axis=0, keepdims=True) / MOBA_BLOCK
        _transpose_values(v_refs[g], vt_ref.at[g])
    neg = jnp.float32(-jnp.inf)
    rows = MOBA_CHUNK_BLOCKS * MOBA_BLOCK

    def query_block(own, carry):
        qs = pl.multiple_of(own * MOBA_BLOCK, MOBA_BLOCK)
        top = own // MOBA_CHUNK_BLOCKS
        qv = [q_refs[g][0, pl.ds(qs, MOBA_BLOCK), :] for g in hs]

        def chunk_scores(g, c, causal):
            st = pl.multiple_of(c * rows, rows)
            s = lax.dot_general(k_refs[g][0, pl.ds(st, rows), :], qv[g], _NT,
                                preferred_element_type=jnp.float32) * SCALE_LOG2E
            s = jnp.concatenate([s[u * MOBA_BLOCK:(u + 1) * MOBA_BLOCK]
                                 + bias_ref[g, pl.ds(c * MOBA_CHUNK_BLOCKS + u, 1), :]
                                 for u in range(MOBA_CHUNK_BLOCKS)], axis=0)
            if causal:
                ahead = lax.broadcasted_iota(jnp.int32, s.shape, 0) - lax.broadcasted_iota(jnp.int32, s.shape, 1)
                s = jnp.where(ahead <= (own - c * MOBA_CHUNK_BLOCKS) * MOBA_BLOCK, s, neg)
            return s

        gates = [lax.dot_general(kmean_ref[g], qv[g].astype(jnp.float32), _NT,
                                 precision=lax.Precision.HIGHEST, preferred_element_type=jnp.float32) for g in hs]
        for g in hs:
            blk = lax.broadcasted_iota(jnp.int32, gates[g].shape, 0)
            gate = jnp.where(blk < own, gates[g], neg)
            rank = jnp.zeros(gate.shape, jnp.int32)
            for mth in range(n_blocks):
                gm = gate[mth:mth + 1, :]
                lower = jnp.where(mth < blk, 1, 0)
                rank = rank + jnp.where(gm > gate, 1, 0) + jnp.where(gm == gate, lower, 0)
            rank = jnp.where(blk < own, rank, MOBA_TOPK)
            bias_ref[g] = jnp.where((rank < MOBA_TOPK) | (blk == own), 0.0, neg)
            m_ref[g] = jnp.full(m_ref.shape[1:], neg, jnp.float32)
            l_ref[g] = jnp.zeros(l_ref.shape[1:], jnp.float32)
            acc_ref[g] = jnp.zeros(acc_ref.shape[1:], jnp.float32)
        firsts = [chunk_scores(g, top, True) for g in hs]
        for g in hs:
            sc_ref[g] = firsts[g]
        first_max = [jnp.max(firsts[g], axis=0, keepdims=True) for g in hs]

        def consume(cur, col_max, produce_next):
            nxt_max = []
            for g in hs:
                m_old = m_ref[g]
                m_new = jnp.maximum(m_old, col_max[g])
                alpha = jnp.exp2(m_old - m_new)
                p = jnp.exp2(sc_ref[g] - m_new)
                if produce_next:
                    nxt = chunk_scores(g, cur - 1, False)
                    sc_ref[g] = nxt
                    nxt_max.append(jnp.max(nxt, axis=0, keepdims=True))
                l_ref[g] = alpha * l_ref[g] + jnp.sum(p, axis=0, keepdims=True)
                p = p.astype(vt_ref.dtype)
                pv = jnp.zeros(acc_ref.shape[1:], jnp.float32)
                for u in range(MOBA_CHUNK_BLOCKS):
                    pv = pv + jnp.dot(vt_ref[g, cur * MOBA_CHUNK_BLOCKS + u],
                                      p[u * MOBA_BLOCK:(u + 1) * MOBA_BLOCK, :], preferred_element_type=jnp.float32)
                acc_ref[g] = alpha * acc_ref[g] + pv
                m_ref[g] = m_new
            return tuple(nxt_max)

        last_max = lax.fori_loop(0, top, lambda i, col_max: consume(top - i, col_max, True), tuple(first_max))
        consume(0, last_max, False)
        for g in hs:
            o_ref[0, pl.ds(qs, MOBA_BLOCK), g * HEAD_DIM:(g + 1) * HEAD_DIM] = (
                (acc_ref[g] / l_ref[g]).T.astype(o_ref.dtype))
        return carry

    lax.fori_loop(0, n_blocks, query_block, 0)


def moba_attention(qk, rest):
    b, s, _ = qk.shape
    n_blocks = s // MOBA_BLOCK
    assert s % MOBA_BLOCK == 0 and n_blocks >= MOBA_TOPK and n_blocks % MOBA_CHUNK_BLOCKS == 0

    g = MOBA_HEADS
    assert N_HEADS_B % g == 0

    def full(first_block, j):
        return pl.BlockSpec((1, s, HEAD_DIM), lambda bi, h: (bi, 0, first_block + N_HEADS_A + h * g + j))

    heads = range(g)
    return pl.pallas_call(
        functools.partial(_moba_kernel, n_blocks=n_blocks), grid=(b, N_HEADS_B // g),
        in_specs=([full(0, j) for j in heads] + [full(QK_K_BLOCK, j) for j in heads]
                  + [full(REST_V_BLOCK, j) for j in heads]),
        out_specs=pl.BlockSpec((1, s, g * HEAD_DIM), lambda bi, h: (bi, 0, h)),
        out_shape=jax.ShapeDtypeStruct((b, s, N_HEADS_B * HEAD_DIM), jnp.bfloat16),
        scratch_shapes=[pltpu.VMEM((g, n_blocks, HEAD_DIM), jnp.float32),
                        pltpu.VMEM((g, n_blocks, HEAD_DIM, MOBA_BLOCK), jnp.bfloat16),
                        pltpu.VMEM((g, n_blocks, MOBA_BLOCK), jnp.float32),
                        pltpu.VMEM((g, MOBA_CHUNK_BLOCKS * MOBA_BLOCK, MOBA_BLOCK), jnp.float32),
                        pltpu.VMEM((g, 1, MOBA_BLOCK), jnp.float32), pltpu.VMEM((g, 1, MOBA_BLOCK), jnp.float32),
                        pltpu.VMEM((g, HEAD_DIM, MOBA_BLOCK), jnp.float32)],
        compiler_params=_params("parallel", "parallel"), name="moba_attention",
    )(*([qk] * (2 * g) + [rest] * g)).reshape(b * s, N_HEADS_B * HEAD_DIM)


STICK_TILE = 256
STICK_CHUNK = 256
STICK_HEADS = 4
STICK_CUTOFF = -104.0


def _stick_kernel(*refs):
    hs = range(STICK_HEADS)
    q_refs, k_refs, v_refs = (refs[i * STICK_HEADS:(i + 1) * STICK_HEADS] for i in range(3))
    o_ref, vt_ref, run_ref, acc_ref = refs[3 * STICK_HEADS:]
    for g in hs:
        _transpose_values(v_refs[g], vt_ref.at[g])
    rows = STICK_CHUNK
    chunks_per_q = STICK_TILE // rows
    sq = lax.broadcasted_iota(jnp.int32, (rows, rows), 0)
    sk = lax.broadcasted_iota(jnp.int32, (rows, rows), 1)
    later = jnp.where(sk > sq, 1.0, 0.0).astype(jnp.bfloat16)
    sub = lax.broadcasted_iota(jnp.int32, (rows, STICK_TILE), 0)
    lane = lax.broadcasted_iota(jnp.int32, (rows, STICK_TILE), 1)

    def tile(qv, kt, valid):
        st = pl.multiple_of(kt * rows, rows)
        zs = [lax.dot_general(k_refs[g][0, pl.ds(st, rows), :], qv[g], _NT,
                              preferred_element_type=jnp.float32) * SCALE for g in hs]
        logs = []
        for g in hs:
            log_1m = -(jnp.maximum(zs[g], 0.0) + jnp.log(1.0 + jnp.exp(-jnp.abs(zs[g]))))
            logs.append(log_1m if valid is None else jnp.where(valid, log_1m, 0.0))
        insides = []
        for g in hs:
            hi = logs[g].astype(jnp.bfloat16)
            lo = (logs[g] - hi.astype(jnp.float32)).astype(jnp.bfloat16)
            insides.append(jnp.dot(later, hi, preferred_element_type=jnp.float32)
                           + jnp.dot(later, lo, preferred_element_type=jnp.float32))
        for g in hs:
            a = jnp.exp(zs[g] + logs[g] + insides[g] + run_ref[g])
            if valid is not None:
                a = jnp.where(valid, a, 0.0)
            acc_ref[g] += jnp.dot(vt_ref[g, kt], a.astype(jnp.bfloat16), preferred_element_type=jnp.float32)
            run_ref[g] += insides[g][0:1, :] + logs[g][0:1, :]

    def alive():
        return (jnp.max(run_ref[...]) > STICK_CUTOFF).astype(jnp.int32)

    def query_tile(qi, carry):
        qs = pl.multiple_of(qi * STICK_TILE, STICK_TILE)
        qv = [q_refs[g][0, pl.ds(qs, STICK_TILE), :] for g in hs]
        run_ref[...] = jnp.zeros_like(run_ref)
        acc_ref[...] = jnp.zeros_like(acc_ref)
        for u in reversed(range(chunks_per_q)):
            tile(qv, qi * chunks_per_q + u, u * rows + sub < lane)

        def earlier(c):
            tile(qv, c[0], None)
            return c[0] - 1, alive()

        lax.while_loop(lambda c: (c[0] >= 0) & (c[1] > 0), earlier, (qi * chunks_per_q - 1, alive()))
        for g in hs:
            o_ref[0, pl.ds(qs, STICK_TILE), g * HEAD_DIM:(g + 1) * HEAD_DIM] = acc_ref[g].T.astype(o_ref.dtype)
        return carry

    lax.fori_loop(0, o_ref.shape[1] // STICK_TILE, query_tile, 0)


def stick_attention(rest):
    b, s, _ = rest.shape
    t = STICK_TILE

    g = STICK_HEADS
    assert N_HEADS_C % g == 0 and STICK_TILE % STICK_CHUNK == 0

    def full(first_block, j):
        return pl.BlockSpec((1, s, HEAD_DIM), lambda bi, h: (bi, 0, first_block + h * g + j))

    heads = range(g)
    return pl.pallas_call(
        _stick_kernel, grid=(b, N_HEADS_C // g),
        in_specs=([full(0, j) for j in heads] + [full(REST_KC_BLOCK, j) for j in heads]
                  + [full(REST_V_BLOCK + NORMED_HEADS, j) for j in heads]),
        out_specs=pl.BlockSpec((1, s, g * HEAD_DIM), lambda bi, h: (bi, 0, h)),
        out_shape=jax.ShapeDtypeStruct((b, s, N_HEADS_C * HEAD_DIM), jnp.bfloat16),
        scratch_shapes=[pltpu.VMEM((g, s // STICK_CHUNK, HEAD_DIM, STICK_CHUNK), jnp.bfloat16),
                        pltpu.VMEM((g, 1, t), jnp.float32), pltpu.VMEM((g, HEAD_DIM, t), jnp.float32)],
        compiler_params=_params("parallel", "parallel"), name="stick_attention",
    )(*([rest] * (3 * g))).reshape(b * s, N_HEADS_C * HEAD_DIM)


def _merge_kernel(oa_ref, ob_ref, oc_ref, g_ref, wa_ref, wb_ref, wc_ref, out_ref):
    d = out_ref.shape[1]
    ya = jnp.dot(oa_ref[...], wa_ref[...], preferred_element_type=jnp.float32)
    yb = jnp.dot(ob_ref[...], wb_ref[...], preferred_element_type=jnp.float32)
    yc = jnp.dot(oc_ref[...], wc_ref[...], preferred_element_type=jnp.float32)
    merged = (g_ref[:, 0:d].astype(jnp.float32) * ya + g_ref[:, d:2 * d].astype(jnp.float32) * yb
              + g_ref[:, 2 * d:3 * d].astype(jnp.float32) * yc)
    out_ref[...] = merged.astype(out_ref.dtype)


def merge_branches(o_a, o_b, o_c, gates, w_a, w_b, w_c):
    m = o_b.shape[0]
    d = w_a.shape[1]
    tm = 512

    def rows(a):
        return pl.BlockSpec((tm, a.shape[1]), lambda i: (i, 0))

    def whole(w):
        return pl.BlockSpec(w.shape, lambda i: (0, 0))

    return pl.pallas_call(
        _merge_kernel, grid=(m // tm,),
        in_specs=[rows(o_a), rows(o_b), rows(o_c), rows(gates), whole(w_a), whole(w_b), whole(w_c)],
        out_specs=pl.BlockSpec((tm, d), lambda i: (i, 0)), out_shape=jax.ShapeDtypeStruct((m, d), jnp.bfloat16),
        compiler_params=_params("parallel"), name="merge_branches",
    )(o_a, o_b, o_c, gates, w_a, w_b, w_c)


def _out_proj_kernel(a_ref, w_ref, x_ref, g_ref, xo_ref, ho_ref):
    xn = x_ref[...] + jnp.dot(a_ref[...], w_ref[...], preferred_element_type=jnp.float32)
    xo_ref[...] = xn
    ho_ref[...] = _rms(xn, g_ref[...]).astype(ho_ref.dtype)


def out_project(a, w, x, next_gain):
    m, k = a.shape
    d = w.shape[1]
    tm = 512
    return pl.pallas_call(
        _out_proj_kernel, grid=(m // tm,),
        in_specs=[pl.BlockSpec((tm, k), lambda i: (i, 0)), pl.BlockSpec((k, d), lambda i: (0, 0)),
                  pl.BlockSpec((tm, d), lambda i: (i, 0)), pl.BlockSpec((1, d), lambda i: (0, 0))],
        out_specs=[pl.BlockSpec((tm, d), lambda i: (i, 0)), pl.BlockSpec((tm, d), lambda i: (i, 0))],
        out_shape=[jax.ShapeDtypeStruct((m, d), jnp.float32), jax.ShapeDtypeStruct((m, d), jnp.bfloat16)],
        compiler_params=_params("parallel"), name="out_project",
    )(a, w, x, next_gain.reshape(1, d))


def _mem_kv_kernel(mem_ref, ln_ref, w_ref, gk_ref, kv_ref):
    hm = _rms(mem_ref[...], ln_ref[...]).astype(jnp.bfloat16)
    kv = jnp.dot(hm, w_ref[...], preferred_element_type=jnp.float32)
    half = kv.shape[1] // 2
    for hd in range(N_HEADS_MEM):
        sl = slice(hd * HEAD_DIM, (hd + 1) * HEAD_DIM)
        kv_ref[:, sl] = _rms(kv[:, sl], gk_ref[...]).astype(kv_ref.dtype)
    kv_ref[:, half:] = kv[:, half:].astype(kv_ref.dtype)


def mem_kv(mem2d, ln, wm_kv, gain_k):
    n, d = mem2d.shape
    w = wm_kv.shape[1]
    return pl.pallas_call(
        _mem_kv_kernel, grid=(1,),
        in_specs=[pl.BlockSpec((n, d), lambda i: (0, 0)), pl.BlockSpec((1, d), lambda i: (0, 0)),
                  pl.BlockSpec((d, w), lambda i: (0, 0)), pl.BlockSpec((1, HEAD_DIM), lambda i: (0, 0))],
        out_specs=pl.BlockSpec((n, w), lambda i: (0, 0)),
        out_shape=jax.ShapeDtypeStruct((n, w), jnp.bfloat16),
        compiler_params=_params("arbitrary"), name="mem_kv",
    )(mem2d, ln.reshape(1, d), wm_kv, gain_k.reshape(1, HEAD_DIM))


def _mem_attn_kernel(h_ref, wq_ref, gq_ref, kv_ref, wo_ref, x_ref, g_ref, xo_ref, ho_ref):
    qf = jnp.dot(h_ref[...], wq_ref[...], preferred_element_type=jnp.float32)
    half = kv_ref.shape[2] // 2
    outs = []
    for hd in range(N_HEADS_MEM):
        sl = slice(hd * HEAD_DIM, (hd + 1) * HEAD_DIM)
        qh = _rms(qf[:, sl], gq_ref[...]).astype(jnp.bfloat16)
        s = lax.dot_general(qh, kv_ref[0, :, sl], _NT, preferred_element_type=jnp.float32) * SCALE
        e = jnp.exp(s - jnp.max(s, axis=1, keepdims=True))
        vh = kv_ref[0, :, half + hd * HEAD_DIM:half + (hd + 1) * HEAD_DIM]
        o = jnp.dot(e.astype(jnp.bfloat16), vh, preferred_element_type=jnp.float32)
        outs.append((o / jnp.sum(e, axis=1, keepdims=True)).astype(jnp.bfloat16))
    o_all = jnp.concatenate(outs, axis=1)
    xn = x_ref[...] + jnp.dot(o_all, wo_ref[...], preferred_element_type=jnp.float32)
    xo_ref[...] = xn
    ho_ref[...] = _rms(xn, g_ref[...]).astype(ho_ref.dtype)


def mem_attention(h, wm_q, gain_q, kv, wm_o, x, next_gain, seq):
    m, d = h.shape
    wq = wm_q.shape[1]
    tm = 512
    per_batch = seq // tm
    return pl.pallas_call(
        _mem_attn_kernel, grid=(m // tm,),
        in_specs=[pl.BlockSpec((tm, d), lambda i: (i, 0)), pl.BlockSpec((d, wq), lambda i: (0, 0)),
                  pl.BlockSpec((1, HEAD_DIM), lambda i: (0, 0)),
                  pl.BlockSpec((1,) + kv.shape[1:], lambda i: (i // per_batch, 0, 0)),
                  pl.BlockSpec((wq, d), lambda i: (0, 0)), pl.BlockSpec((tm, d), lambda i: (i, 0)),
                  pl.BlockSpec((1, d), lambda i: (0, 0))],
        out_specs=[pl.BlockSpec((tm, d), lambda i: (i, 0)), pl.BlockSpec((tm, d), lambda i: (i, 0))],
        out_shape=[jax.ShapeDtypeStruct((m, d), jnp.float32), jax.ShapeDtypeStruct((m, d), jnp.bfloat16)],
        compiler_params=_params("parallel"), name="mem_attention",
    )(h, wm_q, gain_q.reshape(1, HEAD_DIM), kv, wm_o, x, next_gain.reshape(1, d))


def _shift_rows(u, prev, k):
    rolled = pltpu.roll(u, k, axis=0)
    row = lax.broadcasted_iota(jnp.int32, prev.shape, 0)
    head = jnp.where(row < k, pltpu.roll(prev, k, axis=0), rolled[:HALO])
    return jnp.concatenate([head, rolled[HALO:]], axis=0)


def _ffn_kernel(h_ref, wg_ref, wv_ref, cwg_ref, cwv_ref, cbg_ref, cbv_ref, wd_ref, x_ref, o_ref,
                halo_ref, *, tiles_per_seq):
    i, f = pl.program_id(0), pl.program_id(1)
    tm = h_ref.shape[0]
    keep = jnp.where(i % tiles_per_seq == 0, 0.0, 1.0)
    prevs = [jnp.where(keep > 0.0, halo_ref[f, part], 0.0) for part in range(2)]

    @pl.when(f == 0)
    def _():
        o_ref[...] = x_ref[...]

    def up(r):
        rows = slice(r * FF_ROWS, (r + 1) * FF_ROWS)
        return [jnp.dot(h_ref[rows, :], w_ref[...], preferred_element_type=jnp.float32) for w_ref in (wg_ref, wv_ref)]

    n_chunks = tm // FF_ROWS
    us = up(0)
    for r in range(n_chunks):
        ahead = up(r + 1) if r + 1 < n_chunks else None
        ys = []
        for part, (cw_ref, cb_ref) in enumerate(((cwg_ref, cbg_ref), (cwv_ref, cbv_ref))):
            u = us[part]
            ys.append(cw_ref[0:1, :] * _shift_rows(u, prevs[part], 2) + cw_ref[1:2, :] * _shift_rows(u, prevs[part], 1)
                      + cw_ref[2:3, :] * u + cb_ref[...])
            prevs[part] = u[FF_ROWS - HALO:, :]
        act = (ys[0] * _sigmoid(ys[0]) * ys[1]).astype(jnp.bfloat16)
        rows = slice(r * FF_ROWS, (r + 1) * FF_ROWS)
        o_ref[rows, :] += jnp.dot(act, wd_ref[...], preferred_element_type=jnp.float32)
        us = ahead
    for part in range(2):
        halo_ref[f, part] = prevs[part]


def _cast_up_kernel(*refs, valid_blocks):
    *w_refs, o_ref = refs
    f = pl.program_id(1)
    for j, w_ref in enumerate(w_refs):
        inside = f * len(w_refs) + j < valid_blocks
        o_ref[:, j * LANES:(j + 1) * LANES] = jnp.where(inside, w_ref[...], 0.0).astype(o_ref.dtype)


def cast_up_weights(w_up, layer, fp):
    _, d, two_ff = w_up.shape
    d_ff = two_ff // 2
    assert d_ff % LANES == 0 and fp % FF_TILE == 0
    half_blocks = d_ff // LANES
    per_tile = FF_TILE // LANES

    def in_spec(j):
        return pl.BlockSpec(
            (None, d, LANES),
            lambda p, f: (layer, 0, jnp.minimum(p * half_blocks + f * per_tile + j, 2 * half_blocks - 1)))

    return pl.pallas_call(
        functools.partial(_cast_up_kernel, valid_blocks=half_blocks), grid=(2, fp // FF_TILE),
        in_specs=[in_spec(j) for j in range(per_tile)],
        out_specs=pl.BlockSpec((None, None, d, FF_TILE), lambda p, f: (p, f, 0, 0)),
        out_shape=jax.ShapeDtypeStruct((2, fp // FF_TILE, d, FF_TILE), jnp.bfloat16),
        compiler_params=_params("parallel", "parallel"), name="cast_up_weights",
    )(*([w_up] * per_tile))


def _cast_down_kernel(w_ref, o_ref, *, valid_blocks):
    o_ref[...] = jnp.where(pl.program_id(0) < valid_blocks, w_ref[...], 0.0).astype(o_ref.dtype)


def cast_down_weights(w_down, layer, fp):
    _, d_ff, d = w_down.shape
    assert d_ff % LANES == 0
    blocks = d_ff // LANES
    return pl.pallas_call(
        functools.partial(_cast_down_kernel, valid_blocks=blocks), grid=(fp // LANES,),
        in_specs=[pl.BlockSpec((None, LANES, d), lambda r: (layer, jnp.minimum(r, blocks - 1), 0))],
        out_specs=pl.BlockSpec((LANES, d), lambda r: (r, 0)),
        out_shape=jax.ShapeDtypeStruct((fp, d), jnp.bfloat16),
        compiler_params=_params("parallel"), name="cast_down_weights",
    )(w_down)


def conv_ffn(h, w_gv, cw_g, cw_v, cb_g, cb_v, w_down, x, seq):
    m, d = h.shape
    fp = w_gv.shape[1] * w_gv.shape[3]
    assert w_gv.shape[3] == FF_TILE
    tm, tf = 1024, FF_TILE
    nf = fp // tf
    assert CONV_WIDTH - 1 <= HALO and seq % tm == 0 and tm % FF_ROWS == 0

    def cols(rows):
        return pl.BlockSpec((rows, tf), lambda i, f: (0, f))

    def up_half(part):
        return pl.BlockSpec((None, None, d, tf), lambda i, f: (part, f, 0, 0))

    return pl.pallas_call(
        functools.partial(_ffn_kernel, tiles_per_seq=seq // tm), grid=(m // tm, nf),
        in_specs=[pl.BlockSpec((tm, d), lambda i, f: (i, 0)), up_half(0), up_half(1), cols(CONV_WIDTH),
                  cols(CONV_WIDTH), cols(1), cols(1), pl.BlockSpec((tf, d), lambda i, f: (f, 0)),
                  pl.BlockSpec((tm, d), lambda i, f: (i, 0))],
        out_specs=pl.BlockSpec((tm, d), lambda i, f: (i, 0)),
        out_shape=jax.ShapeDtypeStruct((m, d), jnp.float32),
        scratch_shapes=[pltpu.VMEM((nf, 2, HALO, tf), jnp.float32)],
        compiler_params=_params("arbitrary", "arbitrary"), name="conv_ffn",
    )(h, w_gv, w_gv, cw_g, cw_v, cb_g, cb_v, w_down, x)


CAST_BLOCK_BYTES = 4 * 1024 * 1024


def _cast_kernel(w_ref, o_ref):
    o_ref[...] = w_ref[...].astype(o_ref.dtype)


def cast_layer(w, layer):
    _, r, c = w.shape
    tr = r
    while tr * c * 4 > CAST_BLOCK_BYTES and tr % 16 == 0:
        tr //= 2
    assert r % tr == 0
    return pl.pallas_call(
        _cast_kernel, grid=(r // tr,),
        in_specs=[pl.BlockSpec((None, tr, c), lambda i: (layer, i, 0))],
        out_specs=pl.BlockSpec((tr, c), lambda i: (i, 0)),
        out_shape=jax.ShapeDtypeStruct((r, c), jnp.bfloat16),
        compiler_params=_params("parallel"), name="cast_layer",
    )(w)


def _pad_cols(a, width):
    return jnp.pad(a, ((0, 0), (0, width - a.shape[1])))


def kernel(x, mem, positions, ln_mix, w_qkv, qk_gain, w_br_a, w_br_b, w_br_c, w_gate, b_gate, w_o,
           ln_mem_q, ln_mem_kv, wm_q, wm_kv, wm_o, mem_qk_gain, ln_ffn, w_up, conv_w, conv_b, w_down):
    b, s, d = x.shape
    depth = ln_mix.shape[0]
    d_ff = w_down.shape[1]
    fp = -(-d_ff // FF_TILE) * FF_TILE

    tables = rope_tables(positions)
    xf = x.reshape(b * s, d)
    mem2d = mem.reshape(b * mem.shape[1], d)
    h = rmsnorm_bf16(xf, ln_mix[0])
    for l in range(depth):
        gain_cols = jnp.concatenate(
            [jnp.tile(qk_gain[l, 0], N_HEADS_A), jnp.tile(qk_gain[l, 2], N_HEADS_B),
             jnp.tile(qk_gain[l, 1], N_HEADS_A), jnp.tile(qk_gain[l, 3], N_HEADS_B)]).reshape(1, -1)
        qk, rest = qkv_project(h, cast_layer(w_qkv, l), gain_cols, tables)
        qk, rest = qk.reshape(b, s, -1), rest.reshape(b, s, -1)
        gates = gate_project(h, cast_layer(w_gate, l), b_gate[l])
        o_a = dilated_attention(qk, rest)
        o_b = moba_attention(qk, rest)
        o_c = stick_attention(rest)
        merged = merge_branches(o_a, o_b, o_c, gates,
                                cast_layer(w_br_a, l), cast_layer(w_br_b, l), cast_layer(w_br_c, l))
        xf, h = out_project(merged, cast_layer(w_o, l), xf, ln_mem_q[l])

        kv = mem_kv(mem2d, ln_mem_kv[l], cast_layer(wm_kv, l), mem_qk_gain[l, 1])
        kv = kv.reshape(b, mem.shape[1], kv.shape[1])
        xf, h = mem_attention(h, cast_layer(wm_q, l), mem_qk_gain[l, 0], kv, cast_layer(wm_o, l), xf,
                              ln_ffn[l], s)

        cw_g, cw_v = _pad_cols(conv_w[l, :, :d_ff], fp), _pad_cols(conv_w[l, :, d_ff:], fp)
        cb_g = _pad_cols(conv_b[l, :d_ff].reshape(1, d_ff), fp)
        cb_v = _pad_cols(conv_b[l, d_ff:].reshape(1, d_ff), fp)
        xf = conv_ffn(h, cast_up_weights(w_up, l, fp), cw_g, cw_v, cb_g, cb_v,
                      cast_down_weights(w_down, l, fp), xf, s)
        if l + 1 < depth:
            h = rmsnorm_bf16(xf, ln_mix[l + 1])
    return xf.reshape(b, s, d)
```

```python
import functools

import jax
import jax.numpy as jnp
from jax import lax
from jax.experimental import pallas as pl
from jax.experimental.pallas import tpu as pltpu

HEAD_DIM = 128
LANES = 128
DILATIONS = (1, 4, 16)
WINDOW_STEPS = 128
HEADS_PER_GROUP = 2
N_HEADS_A = 6
N_HEADS_B = 6
N_HEADS_C = 4
N_HEADS_MIX = 16
MIX_WIDTH = N_HEADS_MIX * HEAD_DIM
KEY_TILE = 128
MOBA_BLOCK = 256
MOBA_TOPK = 3
MOBA_CHUNK_BLOCKS = 2
MOBA_HEADS = 3
N_HEADS_MEM = 4
ROPE_THETA = 500000.0
ROT_DIM = HEAD_DIM // 4
ROT_HALF = ROT_DIM // 2
CONV_WIDTH = 3
EPS = 1e-6
SCALE = HEAD_DIM ** -0.5
SCALE_LOG2E = SCALE * 1.4426950408889634
FF_TILE = 512
FF_ROWS = 512
HALO = 8
VMEM_LIMIT = 60 * 1024 * 1024

_NT = (((1,), (1,)), ((), ()))


def _params(*sem):
    return pltpu.CompilerParams(dimension_semantics=sem, vmem_limit_bytes=VMEM_LIMIT)


def _rms(y, gain):
    return y * lax.rsqrt(jnp.mean(y * y, axis=-1, keepdims=True) + EPS) * gain


def _sigmoid(y):
    return 1.0 / (1.0 + jnp.exp(-y))


def _rope_table_kernel(pos_ref, inv_ref, cos_ref, sin_lo_ref, sin_hi_ref):
    ang = pos_ref[...].astype(jnp.float32) * inv_ref[...]
    lane = lax.broadcasted_iota(jnp.int32, ang.shape, 1)
    s = jnp.sin(ang)
    cos_ref[...] = jnp.cos(ang)
    sin_lo_ref[...] = jnp.where(lane < ROT_HALF, -s, 0.0)
    sin_hi_ref[...] = jnp.where((lane >= ROT_HALF) & (lane < ROT_DIM), s, 0.0)


def rope_tables(positions):
    m = positions.size
    tm = 1024
    inv = ROPE_THETA ** (-jnp.arange(0, ROT_DIM, 2, dtype=jnp.float32) / ROT_DIM)
    inv_row = jnp.zeros((1, HEAD_DIM), jnp.float32).at[0, :ROT_DIM].set(jnp.concatenate([inv, inv]))
    tab = jax.ShapeDtypeStruct((m, HEAD_DIM), jnp.float32)
    spec = pl.BlockSpec((tm, HEAD_DIM), lambda i: (i, 0))
    return pl.pallas_call(
        _rope_table_kernel, grid=(m // tm,),
        in_specs=[pl.BlockSpec((tm, 1), lambda i: (i, 0)), pl.BlockSpec((1, HEAD_DIM), lambda i: (0, 0))],
        out_specs=[spec, spec, spec], out_shape=[tab, tab, tab],
        compiler_params=_params("parallel"), name="rope_tables",
    )(positions.reshape(m, 1), inv_row)


def _rmsnorm_kernel(x_ref, g_ref, o_ref):
    o_ref[...] = _rms(x_ref[...], g_ref[...]).astype(o_ref.dtype)


def rmsnorm_bf16(x, gain):
    m, d = x.shape
    tm = 512
    return pl.pallas_call(
        _rmsnorm_kernel, grid=(m // tm,),
        in_specs=[pl.BlockSpec((tm, d), lambda i: (i, 0)), pl.BlockSpec((1, d), lambda i: (0, 0))],
        out_specs=pl.BlockSpec((tm, d), lambda i: (i, 0)),
        out_shape=jax.ShapeDtypeStruct((m, d), jnp.bfloat16),
        compiler_params=_params("parallel"), name="rmsnorm",
    )(x, gain.reshape(1, d))


PROJ_TILE = 512
PROJ_ROWS = 256
NORMED_HEADS = N_HEADS_A + N_HEADS_B
QK_K_BLOCK = NORMED_HEADS
REST_KC_BLOCK = N_HEADS_C
REST_V_BLOCK = 2 * N_HEADS_C


def _qk_norm_kernel(h_ref, w_ref, gain_ref, ones_ref, cos_ref, sin_lo_ref, sin_hi_ref, o_ref):
    def project(r):
        return jnp.dot(h_ref[r * PROJ_ROWS:(r + 1) * PROJ_ROWS, :], w_ref[...], preferred_element_type=jnp.float32)

    n_chunks = h_ref.shape[0] // PROJ_ROWS
    ahead = project(0)
    for r in range(n_chunks):
        rows = slice(r * PROJ_ROWS, (r + 1) * PROJ_ROWS)
        acc = ahead
        ahead = project(r + 1) if r + 1 < n_chunks else None
        ss = jnp.dot((acc * acc).astype(jnp.bfloat16), ones_ref[...], preferred_element_type=jnp.float32)
        y = acc * lax.rsqrt(ss * (1.0 / HEAD_DIM) + EPS) * gain_ref[...]
        c, s_lo, s_hi = cos_ref[rows, :], sin_lo_ref[rows, :], sin_hi_ref[rows, :]
        for hd in range(PROJ_TILE // HEAD_DIM):
            sl = slice(hd * HEAD_DIM, (hd + 1) * HEAD_DIM)
            yh = y[:, sl]
            yh = (yh * c + pltpu.roll(yh, HEAD_DIM - ROT_HALF, axis=1) * s_lo
                  + pltpu.roll(yh, ROT_HALF, axis=1) * s_hi)
            o_ref[rows, sl] = yh.astype(o_ref.dtype)


def _plain_proj_kernel(h_ref, w_ref, o_ref):
    o_ref[...] = jnp.dot(h_ref[...], w_ref[...], preferred_element_type=jnp.float32).astype(o_ref.dtype)


def qkv_project(h, w_qkv, gain_cols, tables):
    m, d = h.shape
    tm, tn = 1024, PROJ_TILE
    per_part = MIX_WIDTH // tn
    normed_tiles = NORMED_HEADS * HEAD_DIM // tn
    assert normed_tiles * tn == NORMED_HEADS * HEAD_DIM and normed_tiles + 1 == per_part
    n_out = 2 * normed_tiles * tn
    ones = jnp.kron(jnp.eye(tn // HEAD_DIM, dtype=jnp.float32),
                    jnp.ones((HEAD_DIM, HEAD_DIM), jnp.float32)).astype(jnp.bfloat16)
    tab_spec = pl.BlockSpec((tm, HEAD_DIM), lambda i, j: (i, 0))
    h_spec = pl.BlockSpec((tm, d), lambda i, j: (i, 0))
    out_spec = pl.BlockSpec((tm, tn), lambda i, j: (i, j))
    out_sds = jax.ShapeDtypeStruct((m, n_out), jnp.bfloat16)
    qk = pl.pallas_call(
        _qk_norm_kernel, grid=(m // tm, 2 * normed_tiles),
        in_specs=[h_spec, pl.BlockSpec((d, tn), lambda i, j: (0, j + j // normed_tiles)),
                  pl.BlockSpec((1, tn), lambda i, j: (0, j)), pl.BlockSpec((tn, tn), lambda i, j: (0, 0)),
                  tab_spec, tab_spec, tab_spec],
        out_specs=out_spec, out_shape=out_sds,
        compiler_params=_params("parallel", "arbitrary"), name="qk_norm_project",
    )(h, w_qkv, gain_cols, ones, *tables)
    rest = pl.pallas_call(
        _plain_proj_kernel, grid=(m // tm, 2 + per_part),
        in_specs=[h_spec, pl.BlockSpec(
            (d, tn), lambda i, j: (0, jnp.where(j < 2, normed_tiles + j * per_part, j + 2 * per_part - 2)))],
        out_specs=out_spec, out_shape=out_sds,
        compiler_params=_params("parallel", "arbitrary"), name="plain_project",
    )(h, w_qkv)
    return qk, rest


def _gate_kernel(h_ref, w_ref, b_ref, o_ref):
    acc = jnp.dot(h_ref[...], w_ref[...], preferred_element_type=jnp.float32)
    o_ref[...] = _sigmoid(acc + b_ref[...]).astype(o_ref.dtype)


def gate_project(h, w_gate, b_gate):
    m, d = h.shape
    n = w_gate.shape[1]
    tm, tn = 1024, 2048
    return pl.pallas_call(
        _gate_kernel, grid=(m // tm, n // tn),
        in_specs=[pl.BlockSpec((tm, d), lambda i, j: (i, 0)), pl.BlockSpec((d, tn), lambda i, j: (0, j)),
                  pl.BlockSpec((1, tn), lambda i, j: (0, j))],
        out_specs=pl.BlockSpec((tm, tn), lambda i, j: (i, j)),
        out_shape=jax.ShapeDtypeStruct((m, n), jnp.bfloat16),
        compiler_params=_params("parallel", "arbitrary"), name="gate_project",
    )(h, w_gate, b_gate.reshape(1, n))


def _transpose_values(v_ref, vt_ref):
    tile = vt_ref.shape[-1]

    def one(c, carry):
        st = pl.multiple_of(c * tile, tile)
        vt_ref[c] = v_ref[0, pl.ds(st, tile), :].astype(jnp.float32).T.astype(vt_ref.dtype)
        return carry
    lax.fori_loop(0, vt_ref.shape[0], one, 0)


def _softmax_tile(st, vt_tile, m_ref, l_ref, acc_ref):
    m_old = m_ref[...]
    m_new = jnp.maximum(m_old, jnp.max(st, axis=0, keepdims=True))
    alpha = jnp.exp(m_old - m_new)
    p = jnp.exp(st - m_new)
    l_ref[...] = alpha * l_ref[...] + jnp.sum(p, axis=0, keepdims=True)
    acc_ref[...] = alpha * acc_ref[...] + jnp.dot(vt_tile, p.astype(vt_tile.dtype),
                                                  preferred_element_type=jnp.float32)
    m_ref[...] = m_new


def _softmax_init(m_ref, l_ref, acc_ref):
    m_ref[...] = jnp.full_like(m_ref, -jnp.inf)
    l_ref[...] = jnp.zeros_like(l_ref)
    acc_ref[...] = jnp.zeros_like(acc_ref)


DILATED_ROWS = tuple((d + 1) * KEY_TILE for d in DILATIONS)
DILATED_OFFSETS = tuple(sum(DILATED_ROWS[:g]) for g in range(len(DILATIONS)))


def _dilated_kernel(*refs):
    n_groups, slots = len(DILATIONS), HEADS_PER_GROUP
    per = n_groups * slots
    q_refs, k_refs, v_refs = (refs[i * per:(i + 1) * per] for i in range(3))
    o_ref, vt_ref, bias_ref, ahead_ref, sc_ref = refs[3 * per:]
    neg = jnp.float32(-jnp.inf)

    ahead_rows = max(DILATED_ROWS)
    ahead = (lax.broadcasted_iota(jnp.int32, (ahead_rows, KEY_TILE), 0)
             - lax.broadcasted_iota(jnp.int32, (ahead_rows, KEY_TILE), 1))
    ahead_ref[...] = ahead
    for g, d in enumerate(DILATIONS):
        for j in range(slots):
            _transpose_values(v_refs[g * slots + j], vt_ref.at[g * slots + j])
        sl = slice(DILATED_OFFSETS[g], DILATED_OFFSETS[g] + DILATED_ROWS[g])
        on_grid = jnp.where((ahead[:DILATED_ROWS[g]] & (d - 1)) == 0, 0.0, neg)
        bias_ref[0, sl, :] = on_grid
        bias_ref[1, sl, :] = jnp.where(ahead[:DILATED_ROWS[g]] >= 0, on_grid, neg)

    def query_tile(qi, carry):
        qs = pl.multiple_of(qi * KEY_TILE, KEY_TILE)
        starts = [jnp.maximum(qi - d, 0) for d in DILATIONS]
        ms = []
        for j in range(slots):
            m = jnp.full((1, KEY_TILE), neg, jnp.float32)
            for g, d in enumerate(DILATIONS):
                q = q_refs[g * slots + j][0, pl.ds(qs, KEY_TILE), :]
                which = jnp.where(qi >= d, 1, 0)
                newest = (qi - starts[g]) * KEY_TILE
                st = pl.multiple_of(starts[g] * KEY_TILE, KEY_TILE)
                raw = lax.dot_general(k_refs[g * slots + j][0, pl.ds(st, DILATED_ROWS[g]), :], q, _NT,
                                      preferred_element_type=jnp.float32)
                for a in range(d + 1):
                    r0 = DILATED_OFFSETS[g] + a * KEY_TILE
                    scores = raw[a * KEY_TILE:(a + 1) * KEY_TILE, :] * SCALE_LOG2E
                    scores = scores + bias_ref[which, r0:r0 + KEY_TILE, :]
                    scores = jnp.where(ahead_ref[a * KEY_TILE:(a + 1) * KEY_TILE, :] <= newest, scores, neg)
                    sc_ref[j, r0:r0 + KEY_TILE, :] = scores
                    m = jnp.maximum(m, jnp.max(scores, axis=0, keepdims=True))
            ms.append(m)

        for j in range(slots):
            l = jnp.zeros((1, KEY_TILE), jnp.float32)
            acc = jnp.zeros((HEAD_DIM, KEY_TILE), jnp.float32)
            for g, d in enumerate(DILATIONS):
                for a in range(d + 1):
                    r0 = DILATED_OFFSETS[g] + a * KEY_TILE
                    p = jnp.exp2(sc_ref[j, r0:r0 + KEY_TILE, :] - ms[j])
                    l = l + jnp.sum(p, axis=0, keepdims=True)
                    acc = acc + jnp.dot(vt_ref[g * slots + j, starts[g] + a], p.astype(vt_ref.dtype),
                                        preferred_element_type=jnp.float32)
            o_ref[0, pl.ds(qs, KEY_TILE), j * HEAD_DIM:(j + 1) * HEAD_DIM] = (acc / l).T.astype(o_ref.dtype)
        return carry

    lax.fori_loop(0, o_ref.shape[1] // KEY_TILE, query_tile, 0)


def dilated_attention(qk, rest):
    b, s, _ = qk.shape
    n_tiles = s // KEY_TILE
    assert WINDOW_STEPS == KEY_TILE and all(d & (d - 1) == 0 for d in DILATIONS)
    assert s >= max(DILATED_ROWS)

    heads = len(DILATIONS) * HEADS_PER_GROUP

    def full(first_block):
        return [pl.BlockSpec((1, s, HEAD_DIM), functools.partial(lambda bi, blk: (bi, 0, blk), blk=first_block + h),
                             pipeline_mode=pl.Buffered(1)) for h in range(heads)]

    return pl.pallas_call(
        _dilated_kernel, grid=(b,),
        in_specs=full(0) + full(QK_K_BLOCK) + full(REST_V_BLOCK),
        out_specs=pl.BlockSpec((1, s, HEADS_PER_GROUP * HEAD_DIM), lambda bi: (bi, 0, 0)),
        out_shape=jax.ShapeDtypeStruct((b, s, HEADS_PER_GROUP * HEAD_DIM), jnp.bfloat16),
        scratch_shapes=[pltpu.VMEM((heads, n_tiles, HEAD_DIM, KEY_TILE), jnp.bfloat16),
                        pltpu.VMEM((2, sum(DILATED_ROWS), KEY_TILE), jnp.float32),
                        pltpu.VMEM((max(DILATED_ROWS), KEY_TILE), jnp.int32),
                        pltpu.VMEM((HEADS_PER_GROUP, sum(DILATED_ROWS), KEY_TILE), jnp.float32)],
        compiler_params=_params("parallel"), name="dilated_attention",
    )(*([qk] * (2 * heads) + [rest] * heads)).reshape(b * s, HEADS_PER_GROUP * HEAD_DIM)


def _moba_kernel(*refs, n_blocks):
    hs = range(MOBA_HEADS)
    q_refs, k_refs, v_refs = (refs[i * MOBA_HEADS:(i + 1) * MOBA_HEADS] for i in range(3))
    o_ref, kmean_ref, vt_ref, bias_ref, sc_ref, m_ref, l_ref, acc_ref = refs[3 * MOBA_HEADS:]
    for g in hs:
        for n in range(n_blocks):
            kb = k_refs[g][0, n * MOBA_BLOCK:(n + 1) * MOBA_BLOCK, :].astype(jnp.float32)
            kmean_ref[g, n:n + 1, :] = jnp.sum(kb, axis=0, keepdims=True) / MOBA_BLOCK
        _transpose_values(v_refs[g], vt_ref.at[g])
    neg = jnp.float32(-jnp.inf)
    rows = MOBA_CHUNK_BLOCKS * MOBA_BLOCK

    def query_block(own, carry):
        qs = pl.multiple_of(own * MOBA_BLOCK, MOBA_BLOCK)
        top = own // MOBA_CHUNK_BLOCKS
        qv = [q_refs[g][0, pl.ds(qs, MOBA_BLOCK), :] for g in hs]

        def chunk_scores(g, c, causal):
            st = pl.multiple_of(c * rows, rows)
            s = lax.dot_general(k_refs[g][0, pl.ds(st, rows), :], qv[g], _NT,
                                preferred_element_type=jnp.float32) * SCALE_LOG2E
            s = jnp.concatenate([s[u * MOBA_BLOCK:(u + 1) * MOBA_BLOCK]
                                 + bias_ref[g, pl.ds(c * MOBA_CHUNK_BLOCKS + u, 1), :]
                                 for u in range(MOBA_CHUNK_BLOCKS)], axis=0)
            if causal:
                ahead = lax.broadcasted_iota(jnp.int32, s.shape, 0) - lax.broadcasted_iota(jnp.int32, s.shape, 1)
                s = jnp.where(ahead <= (own - c * MOBA_CHUNK_BLOCKS) * MOBA_BLOCK, s, neg)
            return s

        gates = [lax.dot_general(kmean_ref[g], qv[g].astype(jnp.float32), _NT,
                                 precision=lax.Precision.HIGHEST, preferred_element_type=jnp.float32) for g in hs]
        for g in hs:
            blk = lax.broadcasted_iota(jnp.int32, gates[g].shape, 0)
            gate = jnp.where(blk < own, gates[g], neg)
            rank = jnp.zeros(gate.shape, jnp.int32)
            for mth in range(n_blocks):
                gm = gate[mth:mth + 1, :]
                lower = jnp.where(mth < blk, 1, 0)
                rank = rank + jnp.where(gm > gate, 1, 0) + jnp.where(gm == gate, lower, 0)
            rank = jnp.where(blk < own, rank, MOBA_TOPK)
            bias_ref[g] = jnp.where((rank < MOBA_TOPK) | (blk == own), 0.0, neg)
            m_ref[g] = jnp.full(m_ref.shape[1:], neg, jnp.float32)
            l_ref[g] = jnp.zeros(l_ref.shape[1:], jnp.float32)
            acc_ref[g] = jnp.zeros(acc_ref.shape[1:], jnp.float32)
        firsts = [chunk_scores(g, top, True) for g in hs]
        for g in hs:
            sc_ref[g] = firsts[g]
        first_max = [jnp.max(firsts[g], axis=0, keepdims=True) for g in hs]

        def consume(cur, col_max, produce_next):
            nxt_max = []
            for g in hs:
                m_old = m_ref[g]
                m_new = jnp.maximum(m_old, col_max[g])
                alpha = jnp.exp2(m_old - m_new)
                p = jnp.exp2(sc_ref[g] - m_new)
                if produce_next:
                    nxt = chunk_scores(g, cur - 1, False)
                    sc_ref[g] = nxt
                    nxt_max.append(jnp.max(nxt, axis=0, keepdims=True))
                l_ref[g] = alpha * l_ref[g] + jnp.sum(p, axis=0, keepdims=True)
                p = p.astype(vt_ref.dtype)
                pv = jnp.zeros(acc_ref.shape[1:], jnp.float32)
                for u in range(MOBA_CHUNK_BLOCKS):
                    pv = pv + jnp.dot(vt_ref[g, cur * MOBA_CHUNK_BLOCKS + u],
                                      p[u * MOBA_BLOCK:(u + 1) * MOBA_BLOCK, :], preferred_element_type=jnp.float32)
                acc_ref[g] = alpha * acc_ref[g] + pv
                m_ref[g] = m_new
            return tuple(nxt_max)

        last_max = lax.fori_loop(0, top, lambda i, col_max: consume(top - i, col_max, True), tuple(first_max))
        consume(0, last_max, False)
        for g in hs:
            o_ref[0, pl.ds(qs, MOBA_BLOCK), g * HEAD_DIM:(g + 1) * HEAD_DIM] = (
                (acc_ref[g] / l_ref[g]).T.astype(o_ref.dtype))
        return carry

    lax.fori_loop(0, n_blocks, query_block, 0)


def moba_attention(qk, rest):
    b, s, _ = qk.shape
    n_blocks = s // MOBA_BLOCK
    assert s % MOBA_BLOCK == 0 and n_blocks >= MOBA_TOPK and n_blocks % MOBA_CHUNK_BLOCKS == 0

    g = MOBA_HEADS
    assert N_HEADS_B % g == 0

    def full(first_block, j):
        return pl.BlockSpec((1, s, HEAD_DIM), lambda bi, h: (bi, 0, first_block + N_HEADS_A + h * g + j))

    heads = range(g)
    return pl.pallas_call(
        functools.partial(_moba_kernel, n_blocks=n_blocks), grid=(b, N_HEADS_B // g),
        in_specs=([full(0, j) for j in heads] + [full(QK_K_BLOCK, j) for j in heads]
                  + [full(REST_V_BLOCK, j) for j in heads]),
        out_specs=pl.BlockSpec((1, s, g * HEAD_DIM), lambda bi, h: (bi, 0, h)),
        out_shape=jax.ShapeDtypeStruct((b, s, N_HEADS_B * HEAD_DIM), jnp.bfloat16),
        scratch_shapes=[pltpu.VMEM((g, n_blocks, HEAD_DIM), jnp.float32),
                        pltpu.VMEM((g, n_blocks, HEAD_DIM, MOBA_BLOCK), jnp.bfloat16),
                        pltpu.VMEM((g, n_blocks, MOBA_BLOCK), jnp.float32),
                        pltpu.VMEM((g, MOBA_CHUNK_BLOCKS * MOBA_BLOCK, MOBA_BLOCK), jnp.float32),
                        pltpu.VMEM((g, 1, MOBA_BLOCK), jnp.float32), pltpu.VMEM((g, 1, MOBA_BLOCK), jnp.float32),
                        pltpu.VMEM((g, HEAD_DIM, MOBA_BLOCK), jnp.float32)],
        compiler_params=_params("parallel", "parallel"), name="moba_attention",
    )(*([qk] * (2 * g) + [rest] * g)).reshape(b * s, N_HEADS_B * HEAD_DIM)


STICK_TILE = 256
STICK_CHUNK = 256
STICK_HEADS = 4
STICK_CUTOFF = -104.0


def _stick_kernel(*refs):
    hs = range(STICK_HEADS)
    q_refs, k_refs, v_refs = (refs[i * STICK_HEADS:(i + 1) * STICK_HEADS] for i in range(3))
    o_ref, vt_ref, run_ref, acc_ref = refs[3 * STICK_HEADS:]
    for g in hs:
        _transpose_values(v_refs[g], vt_ref.at[g])
    rows = STICK_CHUNK
    chunks_per_q = STICK_TILE // rows
    sq = lax.broadcasted_iota(jnp.int32, (rows, rows), 0)
    sk = lax.broadcasted_iota(jnp.int32, (rows, rows), 1)
    later = jnp.where(sk > sq, 1.0, 0.0).astype(jnp.bfloat16)
    sub = lax.broadcasted_iota(jnp.int32, (rows, STICK_TILE), 0)
    lane = lax.broadcasted_iota(jnp.int32, (rows, STICK_TILE), 1)

    def tile(qv, kt, valid):
        st = pl.multiple_of(kt * rows, rows)
        zs = [lax.dot_general(k_refs[g][0, pl.ds(st, rows), :], qv[g], _NT,
                              preferred_element_type=jnp.float32) * SCALE for g in hs]
        logs = []
        for g in hs:
            log_1m = -(jnp.maximum(zs[g], 0.0) + jnp.log(1.0 + jnp.exp(-jnp.abs(zs[g]))))
            logs.append(log_1m if valid is None else jnp.where(valid, log_1m, 0.0))
        insides = []
        for g in hs:
            hi = logs[g].astype(jnp.bfloat16)
            lo = (logs[g] - hi.astype(jnp.float32)).astype(jnp.bfloat16)
            insides.append(jnp.dot(later, hi, preferred_element_type=jnp.float32)
                           + jnp.dot(later, lo, preferred_element_type=jnp.float32))
        for g in hs:
            a = jnp.exp(zs[g] + logs[g] + insides[g] + run_ref[g])
            if valid is not None:
                a = jnp.where(valid, a, 0.0)
            acc_ref[g] += jnp.dot(vt_ref[g, kt], a.astype(jnp.bfloat16), preferred_element_type=jnp.float32)
            run_ref[g] += insides[g][0:1, :] + logs[g][0:1, :]

    def alive():
        return (jnp.max(run_ref[...]) > STICK_CUTOFF).astype(jnp.int32)

    def query_tile(qi, carry):
        qs = pl.multiple_of(qi * STICK_TILE, STICK_TILE)
        qv = [q_refs[g][0, pl.ds(qs, STICK_TILE), :] for g in hs]
        run_ref[...] = jnp.zeros_like(run_ref)
        acc_ref[...] = jnp.zeros_like(acc_ref)
        for u in reversed(range(chunks_per_q)):
            tile(qv, qi * chunks_per_q + u, u * rows + sub < lane)

        def earlier(c):
            tile(qv, c[0], None)
            return c[0] - 1, alive()

        lax.while_loop(lambda c: (c[0] >= 0) & (c[1] > 0), earlier, (qi * chunks_per_q - 1, alive()))
        for g in hs:
            o_ref[0, pl.ds(qs, STICK_TILE), g * HEAD_DIM:(g + 1) * HEAD_DIM] = acc_ref[g].T.astype(o_ref.dtype)
        return carry

    lax.fori_loop(0, o_ref.shape[1] // STICK_TILE, query_tile, 0)


def stick_attention(rest):
    b, s, _ = rest.shape
    t = STICK_TILE

    g = STICK_HEADS
    assert N_HEADS_C % g == 0 and STICK_TILE % STICK_CHUNK == 0

    def full(first_block, j):
        return pl.BlockSpec((1, s, HEAD_DIM), lambda bi, h: (bi, 0, first_block + h * g + j))

    heads = range(g)
    return pl.pallas_call(
        _stick_kernel, grid=(b, N_HEADS_C // g),
        in_specs=([full(0, j) for j in heads] + [full(REST_KC_BLOCK, j) for j in heads]
                  + [full(REST_V_BLOCK + NORMED_HEADS, j) for j in heads]),
        out_specs=pl.BlockSpec((1, s, g * HEAD_DIM), lambda bi, h: (bi, 0, h)),
        out_shape=jax.ShapeDtypeStruct((b, s, N_HEADS_C * HEAD_DIM), jnp.bfloat16),
        scratch_shapes=[pltpu.VMEM((g, s // STICK_CHUNK, HEAD_DIM, STICK_CHUNK), jnp.bfloat16),
                        pltpu.VMEM((g, 1, t), jnp.float32), pltpu.VMEM((g, HEAD_DIM, t), jnp.float32)],
        compiler_params=_params("parallel", "parallel"), name="stick_attention",
    )(*([rest] * (3 * g))).reshape(b * s, N_HEADS_C * HEAD_DIM)


def _merge_kernel(oa_ref, ob_ref, oc_ref, g_ref, wa_ref, wb_ref, wc_ref, out_ref):
    d = out_ref.shape[1]
    ya = jnp.dot(oa_ref[...], wa_ref[...], preferred_element_type=jnp.float32)
    yb = jnp.dot(ob_ref[...], wb_ref[...], preferred_element_type=jnp.float32)
    yc = jnp.dot(oc_ref[...], wc_ref[...], preferred_element_type=jnp.float32)
    merged = (g_ref[:, 0:d].astype(jnp.float32) * ya + g_ref[:, d:2 * d].astype(jnp.float32) * yb
              + g_ref[:, 2 * d:3 * d].astype(jnp.float32) * yc)
    out_ref[...] = merged.astype(out_ref.dtype)


def merge_branches(o_a, o_b, o_c, gates, w_a, w_b, w_c):
    m = o_b.shape[0]
    d = w_a.shape[1]
    tm = 512

    def rows(a):
        return pl.BlockSpec((tm, a.shape[1]), lambda i: (i, 0))

    def whole(w):
        return pl.BlockSpec(w.shape, lambda i: (0, 0))

    return pl.pallas_call(
        _merge_kernel, grid=(m // tm,),
        in_specs=[rows(o_a), rows(o_b), rows(o_c), rows(gates), whole(w_a), whole(w_b), whole(w_c)],
        out_specs=pl.BlockSpec((tm, d), lambda i: (i, 0)), out_shape=jax.ShapeDtypeStruct((m, d), jnp.bfloat16),
        compiler_params=_params("parallel"), name="merge_branches",
    )(o_a, o_b, o_c, gates, w_a, w_b, w_c)


def _merge_out_kernel(oa_ref, ob_ref, oc_ref, g_ref, wa_ref, wb_ref, wc_ref, wo_ref, x_ref, gain_ref,
                      xo_ref, ho_ref):
    d = xo_ref.shape[1]
    ya = jnp.dot(oa_ref[...], wa_ref[...], preferred_element_type=jnp.float32)
    yb = jnp.dot(ob_ref[...], wb_ref[...], preferred_element_type=jnp.float32)
    yc = jnp.dot(oc_ref[...], wc_ref[...], preferred_element_type=jnp.float32)
    merged = (g_ref[:, 0:d].astype(jnp.float32) * ya + g_ref[:, d:2 * d].astype(jnp.float32) * yb
              + g_ref[:, 2 * d:3 * d].astype(jnp.float32) * yc)
    xn = x_ref[...] + jnp.dot(merged.astype(jnp.bfloat16), wo_ref[...], preferred_element_type=jnp.float32)
    xo_ref[...] = xn
    ho_ref[...] = _rms(xn, gain_ref[...]).astype(ho_ref.dtype)


def merge_out_project(o_a, o_b, o_c, gates, w_a, w_b, w_c, w_o, x, next_gain):
    m, d = x.shape
    tm = 256

    def rows(a):
        return pl.BlockSpec((tm, a.shape[1]), lambda i: (i, 0))

    def whole(w):
        return pl.BlockSpec(w.shape, lambda i: (0, 0), pipeline_mode=pl.Buffered(1))

    return pl.pallas_call(
        _merge_out_kernel, grid=(m // tm,),
        in_specs=[rows(o_a), rows(o_b), rows(o_c), rows(gates), whole(w_a), whole(w_b), whole(w_c), whole(w_o),
                  rows(x), pl.BlockSpec((1, d), lambda i: (0, 0))],
        out_specs=[pl.BlockSpec((tm, d), lambda i: (i, 0)), pl.BlockSpec((tm, d), lambda i: (i, 0))],
        out_shape=[jax.ShapeDtypeStruct((m, d), jnp.float32), jax.ShapeDtypeStruct((m, d), jnp.bfloat16)],
        compiler_params=_params("parallel"), name="merge_out_project",
    )(o_a, o_b, o_c, gates, w_a, w_b, w_c, w_o, x, next_gain.reshape(1, d))


def _out_proj_kernel(a_ref, w_ref, x_ref, g_ref, xo_ref, ho_ref):
    xn = x_ref[...] + jnp.dot(a_ref[...], w_ref[...], preferred_element_type=jnp.float32)
    xo_ref[...] = xn
    ho_ref[...] = _rms(xn, g_ref[...]).astype(ho_ref.dtype)


def out_project(a, w, x, next_gain):
    m, k = a.shape
    d = w.shape[1]
    tm = 512
    return pl.pallas_call(
        _out_proj_kernel, grid=(m // tm,),
        in_specs=[pl.BlockSpec((tm, k), lambda i: (i, 0)), pl.BlockSpec((k, d), lambda i: (0, 0)),
                  pl.BlockSpec((tm, d), lambda i: (i, 0)), pl.BlockSpec((1, d), lambda i: (0, 0))],
        out_specs=[pl.BlockSpec((tm, d), lambda i: (i, 0)), pl.BlockSpec((tm, d), lambda i: (i, 0))],
        out_shape=[jax.ShapeDtypeStruct((m, d), jnp.float32), jax.ShapeDtypeStruct((m, d), jnp.bfloat16)],
        compiler_params=_params("parallel"), name="out_project",
    )(a, w, x, next_gain.reshape(1, d))


def _mem_kv_kernel(mem_ref, ln_ref, w_ref, gk_ref, kv_ref):
    hm = _rms(mem_ref[...], ln_ref[...]).astype(jnp.bfloat16)
    kv = jnp.dot(hm, w_ref[...], preferred_element_type=jnp.float32)
    half = kv.shape[1] // 2
    for hd in range(N_HEADS_MEM):
        sl = slice(hd * HEAD_DIM, (hd + 1) * HEAD_DIM)
        kv_ref[:, sl] = _rms(kv[:, sl], gk_ref[...]).astype(kv_ref.dtype)
    kv_ref[:, half:] = kv[:, half:].astype(kv_ref.dtype)


def mem_kv(mem2d, ln, wm_kv, gain_k):
    n, d = mem2d.shape
    w = wm_kv.shape[1]
    return pl.pallas_call(
        _mem_kv_kernel, grid=(1,),
        in_specs=[pl.BlockSpec((n, d), lambda i: (0, 0)), pl.BlockSpec((1, d), lambda i: (0, 0)),
                  pl.BlockSpec((d, w), lambda i: (0, 0)), pl.BlockSpec((1, HEAD_DIM), lambda i: (0, 0))],
        out_specs=pl.BlockSpec((n, w), lambda i: (0, 0)),
        out_shape=jax.ShapeDtypeStruct((n, w), jnp.bfloat16),
        compiler_params=_params("arbitrary"), name="mem_kv",
    )(mem2d, ln.reshape(1, d), wm_kv, gain_k.reshape(1, HEAD_DIM))


def _mem_attn_kernel(h_ref, wq_ref, gq_ref, kv_ref, wo_ref, x_ref, g_ref, xo_ref, ho_ref):
    qf = jnp.dot(h_ref[...], wq_ref[...], preferred_element_type=jnp.float32)
    half = kv_ref.shape[2] // 2
    outs = []
    for hd in range(N_HEADS_MEM):
        sl = slice(hd * HEAD_DIM, (hd + 1) * HEAD_DIM)
        qh = _rms(qf[:, sl], gq_ref[...]).astype(jnp.bfloat16)
        s = lax.dot_general(qh, kv_ref[0, :, sl], _NT, preferred_element_type=jnp.float32) * SCALE
        e = jnp.exp(s - jnp.max(s, axis=1, keepdims=True))
        vh = kv_ref[0, :, half + hd * HEAD_DIM:half + (hd + 1) * HEAD_DIM]
        o = jnp.dot(e.astype(jnp.bfloat16), vh, preferred_element_type=jnp.float32)
        outs.append((o / jnp.sum(e, axis=1, keepdims=True)).astype(jnp.bfloat16))
    o_all = jnp.concatenate(outs, axis=1)
    xn = x_ref[...] + jnp.dot(o_all, wo_ref[...], preferred_element_type=jnp.float32)
    xo_ref[...] = xn
    ho_ref[...] = _rms(xn, g_ref[...]).astype(ho_ref.dtype)


def mem_attention(h, wm_q, gain_q, kv, wm_o, x, next_gain, seq):
    m, d = h.shape
    wq = wm_q.shape[1]
    tm = 512
    per_batch = seq // tm
    return pl.pallas_call(
        _mem_attn_kernel, grid=(m // tm,),
        in_specs=[pl.BlockSpec((tm, d), lambda i: (i, 0)), pl.BlockSpec((d, wq), lambda i: (0, 0)),
                  pl.BlockSpec((1, HEAD_DIM), lambda i: (0, 0)),
                  pl.BlockSpec((1,) + kv.shape[1:], lambda i: (i // per_batch, 0, 0)),
                  pl.BlockSpec((wq, d), lambda i: (0, 0)), pl.BlockSpec((tm, d), lambda i: (i, 0)),
                  pl.BlockSpec((1, d), lambda i: (0, 0))],
        out_specs=[pl.BlockSpec((tm, d), lambda i: (i, 0)), pl.BlockSpec((tm, d), lambda i: (i, 0))],
        out_shape=[jax.ShapeDtypeStruct((m, d), jnp.float32), jax.ShapeDtypeStruct((m, d), jnp.bfloat16)],
        compiler_params=_params("parallel"), name="mem_attention",
    )(h, wm_q, gain_q.reshape(1, HEAD_DIM), kv, wm_o, x, next_gain.reshape(1, d))


def _shift_rows(u, prev, k):
    rolled = pltpu.roll(u, k, axis=0)
    row = lax.broadcasted_iota(jnp.int32, prev.shape, 0)
    head = jnp.where(row < k, pltpu.roll(prev, k, axis=0), rolled[:HALO])
    return jnp.concatenate([head, rolled[HALO:]], axis=0)


def _ffn_kernel(h_ref, wg_ref, wv_ref, cwg_ref, cwv_ref, cbg_ref, cbv_ref, wd_ref, x_ref, o_ref,
                halo_ref, *, tiles_per_seq):
    i, f = pl.program_id(0), pl.program_id(1)
    tm = h_ref.shape[0]
    keep = jnp.where(i % tiles_per_seq == 0, 0.0, 1.0)
    prevs = [jnp.where(keep > 0.0, halo_ref[f, part], 0.0) for part in range(2)]

    @pl.when(f == 0)
    def _():
        o_ref[...] = x_ref[...]

    def up(r):
        rows = slice(r * FF_ROWS, (r + 1) * FF_ROWS)
        return [jnp.dot(h_ref[rows, :], w_ref[...], preferred_element_type=jnp.float32) for w_ref in (wg_ref, wv_ref)]

    n_chunks = tm // FF_ROWS
    us = up(0)
    for r in range(n_chunks):
        ahead = up(r + 1) if r + 1 < n_chunks else None
        ys = []
        for part, (cw_ref, cb_ref) in enumerate(((cwg_ref, cbg_ref), (cwv_ref, cbv_ref))):
            u = us[part]
            ys.append(cw_ref[0:1, :] * _shift_rows(u, prevs[part], 2) + cw_ref[1:2, :] * _shift_rows(u, prevs[part], 1)
                      + cw_ref[2:3, :] * u + cb_ref[...])
            prevs[part] = u[FF_ROWS - HALO:, :]
        act = (ys[0] * _sigmoid(ys[0]) * ys[1]).astype(jnp.bfloat16)
        rows = slice(r * FF_ROWS, (r + 1) * FF_ROWS)
        o_ref[rows, :] += jnp.dot(act, wd_ref[...], preferred_element_type=jnp.float32)
        us = ahead
    for part in range(2):
        halo_ref[f, part] = prevs[part]


def _cast_up_kernel(*refs, valid_blocks):
    *w_refs, o_ref = refs
    f = pl.program_id(1)
    for j, w_ref in enumerate(w_refs):
        inside = f * len(w_refs) + j < valid_blocks
        o_ref[:, j * LANES:(j + 1) * LANES] = jnp.where(inside, w_ref[...], 0.0).astype(o_ref.dtype)


def cast_up_weights(w_up, layer, fp):
    _, d, two_ff = w_up.shape
    d_ff = two_ff // 2
    assert d_ff % LANES == 0 and fp % FF_TILE == 0
    half_blocks = d_ff // LANES
    per_tile = FF_TILE // LANES

    def in_spec(j):
        return pl.BlockSpec(
            (None, d, LANES),
            lambda p, f: (layer, 0, jnp.minimum(p * half_blocks + f * per_tile + j, 2 * half_blocks - 1)))

    return pl.pallas_call(
        functools.partial(_cast_up_kernel, valid_blocks=half_blocks), grid=(2, fp // FF_TILE),
        in_specs=[in_spec(j) for j in range(per_tile)],
        out_specs=pl.BlockSpec((None, None, d, FF_TILE), lambda p, f: (p, f, 0, 0)),
        out_shape=jax.ShapeDtypeStruct((2, fp // FF_TILE, d, FF_TILE), jnp.bfloat16),
        compiler_params=_params("parallel", "parallel"), name="cast_up_weights",
    )(*([w_up] * per_tile))


def _cast_down_kernel(w_ref, o_ref, *, valid_blocks):
    o_ref[...] = jnp.where(pl.program_id(0) < valid_blocks, w_ref[...], 0.0).astype(o_ref.dtype)


def cast_down_weights(w_down, layer, fp):
    _, d_ff, d = w_down.shape
    assert d_ff % LANES == 0
    blocks = d_ff // LANES
    return pl.pallas_call(
        functools.partial(_cast_down_kernel, valid_blocks=blocks), grid=(fp // LANES,),
        in_specs=[pl.BlockSpec((None, LANES, d), lambda r: (layer, jnp.minimum(r, blocks - 1), 0))],
        out_specs=pl.BlockSpec((LANES, d), lambda r: (r, 0)),
        out_shape=jax.ShapeDtypeStruct((fp, d), jnp.bfloat16),
        compiler_params=_params("parallel"), name="cast_down_weights",
    )(w_down)


def conv_ffn(h, w_gv, cw_g, cw_v, cb_g, cb_v, w_down, x, seq):
    m, d = h.shape
    fp = w_gv.shape[1] * w_gv.shape[3]
    assert w_gv.shape[3] == FF_TILE
    tm, tf = 1024, FF_TILE
    nf = fp // tf
    assert CONV_WIDTH - 1 <= HALO and seq % tm == 0 and tm % FF_ROWS == 0

    def cols(rows):
        return pl.BlockSpec((rows, tf), lambda i, f: (0, f))

    def up_half(part):
        return pl.BlockSpec((None, None, d, tf), lambda i, f: (part, f, 0, 0))

    return pl.pallas_call(
        functools.partial(_ffn_kernel, tiles_per_seq=seq // tm), grid=(m // tm, nf),
        in_specs=[pl.BlockSpec((tm, d), lambda i, f: (i, 0)), up_half(0), up_half(1), cols(CONV_WIDTH),
                  cols(CONV_WIDTH), cols(1), cols(1), pl.BlockSpec((tf, d), lambda i, f: (f, 0)),
                  pl.BlockSpec((tm, d), lambda i, f: (i, 0))],
        out_specs=pl.BlockSpec((tm, d), lambda i, f: (i, 0)),
        out_shape=jax.ShapeDtypeStruct((m, d), jnp.float32),
        scratch_shapes=[pltpu.VMEM((nf, 2, HALO, tf), jnp.float32)],
        compiler_params=_params("arbitrary", "arbitrary"), name="conv_ffn",
    )(h, w_gv, w_gv, cw_g, cw_v, cb_g, cb_v, w_down, x)


CAST_BLOCK_BYTES = 4 * 1024 * 1024


def _cast_kernel(w_ref, o_ref):
    o_ref[...] = w_ref[...].astype(o_ref.dtype)


def cast_layer(w, layer):
    _, r, c = w.shape
    tr = r
    while tr * c * 4 > CAST_BLOCK_BYTES and tr % 16 == 0:
        tr //= 2
    assert r % tr == 0
    return pl.pallas_call(
        _cast_kernel, grid=(r // tr,),
        in_specs=[pl.BlockSpec((None, tr, c), lambda i: (layer, i, 0))],
        out_specs=pl.BlockSpec((tr, c), lambda i: (i, 0)),
        out_shape=jax.ShapeDtypeStruct((r, c), jnp.bfloat16),
        compiler_params=_params("parallel"), name="cast_layer",
    )(w)


def _pad_cols(a, width):
    return jnp.pad(a, ((0, 0), (0, width - a.shape[1])))


def kernel(x, mem, positions, ln_mix, w_qkv, qk_gain, w_br_a, w_br_b, w_br_c, w_gate, b_gate, w_o,
           ln_mem_q, ln_mem_kv, wm_q, wm_kv, wm_o, mem_qk_gain, ln_ffn, w_up, conv_w, conv_b, w_down):
    b, s, d = x.shape
    depth = ln_mix.shape[0]
    d_ff = w_down.shape[1]
    fp = -(-d_ff // FF_TILE) * FF_TILE

    tables = rope_tables(positions)
    xf = x.reshape(b * s, d)
    mem2d = mem.reshape(b * mem.shape[1], d)
    h = rmsnorm_bf16(xf, ln_mix[0])
    for l in range(depth):
        gain_cols = jnp.concatenate(
            [jnp.tile(qk_gain[l, 0], N_HEADS_A), jnp.tile(qk_gain[l, 2], N_HEADS_B),
             jnp.tile(qk_gain[l, 1], N_HEADS_A), jnp.tile(qk_gain[l, 3], N_HEADS_B)]).reshape(1, -1)
        qk, rest = qkv_project(h, cast_layer(w_qkv, l), gain_cols, tables)
        qk, rest = qk.reshape(b, s, -1), rest.reshape(b, s, -1)
        gates = gate_project(h, cast_layer(w_gate, l), b_gate[l])
        o_a = dilated_attention(qk, rest)
        o_b = moba_attention(qk, rest)
        o_c = stick_attention(rest)
        xf, h = merge_out_project(o_a, o_b, o_c, gates, cast_layer(w_br_a, l), cast_layer(w_br_b, l),
                                  cast_layer(w_br_c, l), cast_layer(w_o, l), xf, ln_mem_q[l])

        kv = mem_kv(mem2d, ln_mem_kv[l], cast_layer(wm_kv, l), mem_qk_gain[l, 1])
        kv = kv.reshape(b, mem.shape[1], kv.shape[1])
        xf, h = mem_attention(h, cast_layer(wm_q, l), mem_qk_gain[l, 0], kv, cast_layer(wm_o, l), xf,
                              ln_ffn[l], s)

        cw_g, cw_v = _pad_cols(conv_w[l, :, :d_ff], fp), _pad_cols(conv_w[l, :, d_ff:], fp)
        cb_g = _pad_cols(conv_b[l, :d_ff].reshape(1, d_ff), fp)
        cb_v = _pad_cols(conv_b[l, d_ff:].reshape(1, d_ff), fp)
        xf = conv_ffn(h, cast_up_weights(w_up, l, fp), cw_g, cw_v, cb_g, cb_v,
                      cast_down_weights(w_down, l, fp), xf, s)
        if l + 1 < depth:
            h = rmsnorm_bf16(xf, ln_mix[l + 1])
    return xf.reshape(b, s, d)
```
